```python
import math
import jax, jax.numpy as jnp
from jax import lax
import numpy as np

D_MODEL = 1024
BATCH = 2
SEQ = 8192
DEPTH = 4

GRID_W = 64
NA_HEADS = 8
NA_HEAD_DIM = 64
NA_WIDTH = NA_HEADS * NA_HEAD_DIM
NA_KR_MAX = 8
NA_KW = 16
HY_WIDTH = 512
HY_SHORT = 3
HY_POS_BANDS = 16
HY_POS_DIM = 1 + 2 * HY_POS_BANDS
HY_FFN_HIDDEN = 64
HY_DECAY_TARGET = 1e-2
HY_FAST_PCT = 0.3
HY_SLOW_PCT = 1.5
HY_FILTER_SCALE = 0.03
N_BRANCHES = 2
IN_WIDTH = 3 * NA_WIDTH + 3 * HY_WIDTH + N_BRANCHES * D_MODEL
PEER_HEADS = 8
PEER_NKEYS = 128
PEER_N = PEER_NKEYS * PEER_NKEYS
PEER_DKEY = 256
PEER_TOPK = 16
PEER_CHUNK = 128
EPS = 1e-6

kernel_name = "hybrid_na_hyena_peer_encoder"


def rmsnorm(x, g):
    xf = x.astype(jnp.float32)
    y = xf * lax.rsqrt(jnp.mean(xf * xf, axis=-1, keepdims=True) + EPS)
    return (y * g.astype(jnp.float32)).astype(x.dtype)


def neighbourhood_attention(q, k, v, rpb):
    B, S, H, Dh = q.shape
    rows = S // GRID_W
    kr = min(NA_KR_MAX, rows)

    def to_grid(t):
        return t.reshape(B, rows, GRID_W, H, Dh).transpose(0, 3, 1, 2, 4)

    qg, kg, vg = to_grid(q), to_grid(k), to_grid(v)
    r = jnp.arange(rows)
    r0 = jnp.clip(r - kr // 2, 0, rows - kr)
    row_idx = r0[:, None] + jnp.arange(kr)[None, :]
    c = jnp.arange(GRID_W)
    c0 = jnp.clip(c - NA_KW // 2, 0, GRID_W - NA_KW)
    col_in = (c[None, :] >= c0[:, None]) & (c[None, :] < c0[:, None] + NA_KW)

    k_rows = kg[:, :, row_idx]
    v_rows = vg[:, :, row_idx]
    s = jnp.einsum('bhrqd,bhrjkd->bhrqjk', qg, k_rows).astype(jnp.float32) * (Dh ** -0.5)

    dr_idx = row_idx - r[:, None] + (NA_KR_MAX - 1)
    dc_idx = jnp.clip(c[None, :] - c[:, None] + (NA_KW - 1), 0, 2 * NA_KW - 2)
    bias = rpb.astype(jnp.float32)[:, dr_idx[:, None, :, None], dc_idx[None, :, None, :]]
    s = s + bias[None]
    s = jnp.where(col_in[:, None, :], s, -jnp.inf)
    p = jax.nn.softmax(s.reshape(B, H, rows, GRID_W, kr * GRID_W), axis=-1)
    p = p.reshape(B, H, rows, GRID_W, kr, GRID_W).astype(v.dtype)
    o = jnp.einsum('bhrqjk,bhrjkd->bhrqd', p, v_rows)
    return o.transpose(0, 2, 3, 1, 4).reshape(B, S, H * Dh)


def short_conv(u, w, b):
    up = jnp.pad(u, ((0, 0), (1, 1), (0, 0)))
    return up[:, :-2] * w[0] + up[:, 1:-1] * w[1] + up[:, 2:] * w[2] + b


def hyena_pos_features(L):
    t = jnp.linspace(0.0, 1.0, L, dtype=jnp.float32)[:, None]
    w = 2.0 * math.pi * jnp.arange(L, dtype=jnp.float32)[:, None] / L
    f = jnp.linspace(1e-4, HY_POS_BANDS - 1, HY_POS_BANDS, dtype=jnp.float32)[None, :]
    return jnp.concatenate([t, jnp.cos(f * w), -jnp.sin(f * w)], axis=-1), t


def hyena_filters(z, t, w1, b1, freq, w2, b2, w3):
    f32 = jnp.float32
    fr = freq.astype(f32)
    h = jnp.sin(fr * (z @ w1.astype(f32) + b1.astype(f32)))
    h = jnp.sin(fr * (h @ w2.astype(f32) + b2.astype(f32)))
    h = h @ w3.astype(f32)
    max_decay = math.log(HY_DECAY_TARGET) / HY_FAST_PCT
    min_decay = math.log(HY_DECAY_TARGET) / HY_SLOW_PCT
    deltas = jnp.linspace(min_decay, max_decay, HY_WIDTH, dtype=f32)
    decay = jnp.exp(-t * jnp.abs(deltas)[None, :])
    h = h.reshape(-1, 2, HY_WIDTH) * decay[:, None, :]
    return h[:, 0], h[:, 1]


def fft_conv(u, h):
    L = u.shape[1]
    n = 2 * L
    U = jnp.fft.rfft(u, n=n, axis=1)
    Hf = jnp.fft.rfft(h, n=n, axis=0)
    return jnp.fft.irfft(U * Hf[None], n=n, axis=1)[:, :L]


def hyena_mixer(u, h_fwd, h_bwd, skip_bias):
    x0, x1, vv = jnp.split(u.astype(jnp.float32), 3, axis=-1)
    z = vv * x1
    y = (fft_conv(z, h_fwd)
         + jnp.flip(fft_conv(jnp.flip(z, axis=1), h_bwd), axis=1)
         + z * skip_bias.astype(jnp.float32))
    return (y * x0).astype(u.dtype)


def peer(h, wq, subkeys, u_tab, v_tab):
    B, S, D = h.shape
    T = B * S
    half = PEER_DKEY // 2

    def chunk(xc):
        C = xc.shape[0]
        q = (xc @ wq).reshape(C, PEER_HEADS, 2, half)
        s = jnp.einsum('thpd,hpkd->thpk', q, subkeys).astype(jnp.float32)
        sv, si = lax.top_k(s, PEER_TOPK)
        cand = (sv[:, :, 0, :, None] + sv[:, :, 1, None, :]).reshape(C, PEER_HEADS, PEER_TOPK * PEER_TOPK)
        cid = (si[:, :, 0, :, None] * PEER_NKEYS + si[:, :, 1, None, :]).reshape(C, PEER_HEADS, PEER_TOPK * PEER_TOPK)
        fs, fi = lax.top_k(cand, PEER_TOPK)
        eid = jnp.take_along_axis(cid, fi, axis=-1)
        g = jax.nn.softmax(fs, axis=-1)
        u = u_tab[eid]
        a = jax.nn.gelu(jnp.einsum('tc,thkc->thk', xc, u).astype(jnp.float32))
        vg = v_tab[eid]
        return jnp.einsum('thk,thkc->tc', (g * a).astype(xc.dtype), vg)

    out = lax.map(chunk, h.reshape(T // PEER_CHUNK, PEER_CHUNK, D))
    return out.reshape(B, S, D)


def setup_inputs(seed: int = 0) -> dict:
    key = jax.random.key(seed)
    ks = iter(jax.random.split(key, 32))

    def nrm(shape, scale):
        return jax.random.normal(next(ks), shape, jnp.float32) * scale

    return {
        "x": nrm((BATCH, SEQ, D_MODEL), 1.0),
        "norm1_g": 1.0 + nrm((DEPTH, D_MODEL), 0.02),
        "w_in": nrm((DEPTH, D_MODEL, IN_WIDTH), D_MODEL ** -0.5),
        "gate_b": nrm((DEPTH, N_BRANCHES, D_MODEL), 0.02),
        "na_q_g": 1.0 + nrm((DEPTH, NA_HEAD_DIM), 0.02),
        "na_k_g": 1.0 + nrm((DEPTH, NA_HEAD_DIM), 0.02),
        "na_rpb": nrm((DEPTH, NA_HEADS, 2 * NA_KR_MAX - 1, 2 * NA_KW - 1), 0.02),
        "hy_conv_w": nrm((DEPTH, HY_SHORT, 3 * HY_WIDTH), HY_SHORT ** -0.5),
        "hy_conv_b": nrm((DEPTH, 3 * HY_WIDTH), 0.02),
        "hy_w1": nrm((DEPTH, HY_POS_DIM, HY_FFN_HIDDEN), HY_POS_DIM ** -0.5),
        "hy_b1": nrm((DEPTH, HY_FFN_HIDDEN), 0.02),
        "hy_freq": 1.0 + nrm((DEPTH, HY_FFN_HIDDEN), 0.02),
        "hy_w2": nrm((DEPTH, HY_FFN_HIDDEN, HY_FFN_HIDDEN), HY_FFN_HIDDEN ** -0.5),
        "hy_b2": nrm((DEPTH, HY_FFN_HIDDEN), 0.02),
        "hy_w3": nrm((DEPTH, HY_FFN_HIDDEN, 2 * HY_WIDTH), HY_FILTER_SCALE * HY_FFN_HIDDEN ** -0.5),
        "hy_bias": nrm((DEPTH, HY_WIDTH), 1.0),
        "w_up_na": nrm((DEPTH, NA_WIDTH, D_MODEL), NA_WIDTH ** -0.5),
        "w_up_hy": nrm((DEPTH, HY_WIDTH, D_MODEL), HY_WIDTH ** -0.5),
        "w_out": nrm((DEPTH, D_MODEL, D_MODEL), D_MODEL ** -0.5),
        "norm2_g": 1.0 + nrm((DEPTH, D_MODEL), 0.02),
        "peer_wq": nrm((DEPTH, D_MODEL, PEER_HEADS * PEER_DKEY), D_MODEL ** -0.5),
        "peer_subkeys": nrm((DEPTH, PEER_HEADS, 2, PEER_NKEYS, PEER_DKEY // 2), (PEER_DKEY // 2) ** -0.5),
        "peer_u": nrm((DEPTH, PEER_N, D_MODEL), D_MODEL ** -0.5),
        "peer_v": nrm((DEPTH, PEER_N, D_MODEL), D_MODEL ** -0.5),
    }


def reference(x, norm1_g, w_in, gate_b, na_q_g, na_k_g, na_rpb, hy_conv_w, hy_conv_b,
              hy_w1, hy_b1, hy_freq, hy_w2, hy_b2, hy_w3, hy_bias, w_up_na, w_up_hy,
              w_out, norm2_g, peer_wq, peer_subkeys, peer_u, peer_v):
    B, S, D = x.shape
    pos_z, pos_t = hyena_pos_features(S)
    splits = [NA_WIDTH, 2 * NA_WIDTH, 3 * NA_WIDTH, 3 * NA_WIDTH + 3 * HY_WIDTH]
    for l in range(DEPTH):
        hn = rmsnorm(x, norm1_g[l])
        proj = hn @ w_in[l]
        q, k, v, hy_in, gates = jnp.split(proj, splits, axis=-1)
        q = rmsnorm(q.reshape(B, S, NA_HEADS, NA_HEAD_DIM), na_q_g[l])
        k = rmsnorm(k.reshape(B, S, NA_HEADS, NA_HEAD_DIM), na_k_g[l])
        v = v.reshape(B, S, NA_HEADS, NA_HEAD_DIM)
        a_out = neighbourhood_attention(q, k, v, na_rpb[l])

        u = short_conv(hy_in, hy_conv_w[l], hy_conv_b[l])
        h_fwd, h_bwd = hyena_filters(pos_z, pos_t, hy_w1[l], hy_b1[l], hy_freq[l],
                                     hy_w2[l], hy_b2[l], hy_w3[l])
        b_out = hyena_mixer(u, h_fwd, h_bwd, hy_bias[l])

        g = jax.nn.sigmoid((gates.reshape(B, S, N_BRANCHES, D) + gate_b[l]).astype(jnp.float32)).astype(x.dtype)
        merged = g[:, :, 0] * (a_out @ w_up_na[l]) + g[:, :, 1] * (b_out @ w_up_hy[l])
        x = x + merged @ w_out[l]
        x = x + peer(rmsnorm(x, norm2_g[l]), peer_wq[l], peer_subkeys[l], peer_u[l], peer_v[l])
    return x
```

```python
import functools
import math

import numpy as np
import jax
import jax.numpy as jnp
from jax import lax
from jax.experimental import pallas as pl
from jax.experimental.pallas import tpu as pltpu

F32 = jnp.float32
BF16 = jnp.bfloat16

GRID_W = 64
NA_HEADS = 8
NA_HEAD_DIM = 64
NA_WIDTH = NA_HEADS * NA_HEAD_DIM
NA_KR = 8
NA_KW = 16
HY_WIDTH = 512
HY_POS_BANDS = 16
HY_POS_DIM = 1 + 2 * HY_POS_BANDS
HY_FFN_HIDDEN = 64
HY_DECAY_TARGET = 1e-2
HY_FAST_PCT = 0.3
HY_SLOW_PCT = 1.5
PEER_HEADS = 8
PEER_NKEYS = 128
PEER_TOPK = 16
PEER_HALF = 128
EPS = 1e-6
NEG = -1e30

LANES = 128
VMEM_LIMIT_BYTES = 56 * 1024 * 1024
FFT_N2 = 128


def _params(*sem):
    return pltpu.CompilerParams(dimension_semantics=sem, vmem_limit_bytes=VMEM_LIMIT_BYTES)


def _split(x):
    hi = x.astype(BF16)
    lo = (x - hi.astype(F32)).astype(BF16)
    return hi, lo


def _dot(a, b):
    return jnp.dot(a, b, preferred_element_type=F32)


def _dot3(ah, al, bh, bl):
    return _dot(ah, bh) + _dot(ah, bl) + _dot(al, bh)


def _rmsnorm_kernel(x_ref, g_ref, o_ref, ot_ref):
    x = x_ref[...]
    y = x * lax.rsqrt(jnp.mean(x * x, axis=-1, keepdims=True) + EPS)
    y = y * g_ref[...]
    o_ref[...] = y.astype(BF16)
    ot_ref[...] = y.T.astype(BF16)


def _rmsnorm(x, g, tm=512):
    T, D = x.shape
    return pl.pallas_call(
        _rmsnorm_kernel,
        grid=(T // tm,),
        in_specs=[pl.BlockSpec((tm, D), lambda i: (i, 0)),
                  pl.BlockSpec((1, D), lambda i: (0, 0))],
        out_specs=[pl.BlockSpec((tm, D), lambda i: (i, 0)),
                   pl.BlockSpec((D, tm), lambda i: (0, i))],
        out_shape=[jax.ShapeDtypeStruct((T, D), BF16), jax.ShapeDtypeStruct((D, T), BF16)],
        compiler_params=_params("parallel"),
        name="rmsnorm",
    )(x, g.reshape(1, D))


def _mm_kernel(epi, n_extra, a_ref, w_ref, *rest):
    o_ref = rest[n_extra]
    acc = _dot(a_ref[...], w_ref[...])
    o_ref[...] = epi(acc, *[r[...] for r in rest[:n_extra]]).astype(o_ref.dtype)


def _mm(a, w, *, out_dtype, epi=lambda acc: acc, extras=(), tm=1024, tn=512, name="mm"):
    M, K = a.shape
    N = w.shape[1]
    tm, tn = min(tm, M), min(tn, N)
    in_specs = [pl.BlockSpec((tm, K), lambda i, j: (i, 0)),
                pl.BlockSpec((K, tn), lambda i, j: (0, j))]
    args = [a, w]
    for arr, kind in extras:
        if kind == "col":
            in_specs.append(pl.BlockSpec((1, tn), lambda i, j: (0, j)))
        elif kind == "tile":
            in_specs.append(pl.BlockSpec((tm, tn), lambda i, j: (i, j)))
        else:
            in_specs.append(pl.BlockSpec(arr.shape, lambda i, j, nd=arr.ndim: (0,) * nd))
        args.append(arr)
    return pl.pallas_call(
        functools.partial(_mm_kernel, epi, len(extras)),
        grid=(M // tm, N // tn),
        in_specs=in_specs,
        out_specs=pl.BlockSpec((tm, tn), lambda i, j: (i, j)),
        out_shape=jax.ShapeDtypeStruct((M, N), out_dtype),
        compiler_params=_params("parallel", "parallel"),
        name=name,
    )(*args)


def _epi_head_rmsnorm(acc, gain, head_sum):
    hi, lo = _split(acc * acc)
    ms = (_dot(hi, head_sum) + _dot(lo, head_sum)) * (1.0 / NA_HEAD_DIM)
    return acc * lax.rsqrt(ms + EPS) * gain


def _epi_gate(acc, bias):
    return jax.nn.sigmoid(acc + bias)


def _epi_residual(acc, res):
    return res + acc


def _merge_kernel(a_ref, b_ref, wa_ref, wb_ref, g_ref, o_ref):
    d = o_ref.shape[1]
    ya = _dot(a_ref[...], wa_ref[...])
    yb = _dot(b_ref[...], wb_ref[...])
    g = g_ref[...].astype(F32)
    o_ref[...] = (g[:, :d] * ya + g[:, d:] * yb).astype(o_ref.dtype)


def _merge(a_out, b_out, w_na, w_hy, gates, tm=512):
    T, wa = a_out.shape
    D = w_na.shape[1]
    return pl.pallas_call(
        _merge_kernel,
        grid=(T // tm,),
        in_specs=[pl.BlockSpec((tm, wa), lambda i: (i, 0)),
                  pl.BlockSpec((tm, b_out.shape[1]), lambda i: (i, 0)),
                  pl.BlockSpec(w_na.shape, lambda i: (0, 0)),
                  pl.BlockSpec(w_hy.shape, lambda i: (0, 0)),
                  pl.BlockSpec((tm, 2 * D), lambda i: (i, 0))],
        out_specs=pl.BlockSpec((tm, D), lambda i: (i, 0)),
        out_shape=jax.ShapeDtypeStruct((T, D), BF16),
        compiler_params=_params("parallel"),
        name="merge",
    )(a_out, b_out, w_na, w_hy, gates)


def _na_kernel(rows, rb, q_ref, k_ref, v_ref, bias_ref, o_ref):
    blk = pl.program_id(2)
    lane = lax.broadcasted_iota(jnp.int32, (GRID_W, LANES), 1)
    nk = NA_KR * GRID_W
    for i in range(rb):
        r = blk * rb + i
        r0 = jnp.clip(r - NA_KR // 2, 0, rows - NA_KR)
        dr0 = r0 - r + (NA_KR - 1)
        start = pl.multiple_of(r0 * GRID_W, GRID_W)
        qr = q_ref[i * GRID_W:(i + 1) * GRID_W, :]
        kw = k_ref[pl.ds(start, nk), :]
        vw = v_ref[pl.ds(start, nk), :]
        outs = []
        for hh in range(2):
            in_head = (lane >= NA_HEAD_DIM) if hh else (lane < NA_HEAD_DIM)
            qm = jnp.where(in_head, qr, jnp.zeros_like(qr))
            s = lax.dot_general(qm, kw, (((1,), (1,)), ((), ())), preferred_element_type=F32)
            s = s + bias_ref[hh, dr0]
            m = jnp.max(s, axis=-1, keepdims=True)
            p = jnp.exp(s - m)
            p = p / jnp.sum(p, axis=-1, keepdims=True)
            outs.append(_dot(p.astype(BF16), vw))
        o_ref[i * GRID_W:(i + 1) * GRID_W, :] = jnp.where(lane < NA_HEAD_DIM, outs[0], outs[1]).astype(o_ref.dtype)


def _na_bias_table(rpb):
    c = jnp.arange(GRID_W)
    c0 = jnp.clip(c - NA_KW // 2, 0, GRID_W - NA_KW)
    col_in = (c[None, :] >= c0[:, None]) & (c[None, :] < c0[:, None] + NA_KW)
    dc_idx = jnp.clip(c[None, :] - c[:, None] + (NA_KW - 1), 0, 2 * NA_KW - 2)
    dr = jnp.arange(NA_KR)[:, None] + jnp.arange(NA_KR)[None, :]
    b = rpb.astype(F32)[:, dr[:, None, :, None], dc_idx[None, :, None, :]]
    b = jnp.where(col_in[None, None, :, None, :], b, NEG)
    return b.reshape(NA_HEADS, NA_KR, GRID_W, NA_KR * GRID_W)


def _na(qk, v, bias, B, S, rb=8):
    rows = S // GRID_W
    assert rows >= NA_KR and rows % rb == 0
    nblk = rows // rb
    tq = rb * GRID_W
    kofs = NA_WIDTH // LANES
    return pl.pallas_call(
        functools.partial(_na_kernel, rows, rb),
        grid=(B, NA_HEADS // 2, nblk),
        in_specs=[pl.BlockSpec((tq, LANES), lambda b, p, r: (b * nblk + r, p)),
                  pl.BlockSpec((S, LANES), lambda b, p, r: (b, kofs + p)),
                  pl.BlockSpec((S, LANES), lambda b, p, r: (b, p)),
                  pl.BlockSpec((2, NA_KR, GRID_W, NA_KR * GRID_W), lambda b, p, r: (p, 0, 0, 0))],
        out_specs=pl.BlockSpec((tq, LANES), lambda b, p, r: (b * nblk + r, p)),
        out_shape=jax.ShapeDtypeStruct((B * S, NA_WIDTH), BF16),
        compiler_params=_params("parallel", "parallel", "parallel"),
        name="na_attention",
    )(qk, qk, v, bias)


def _hy_prep_kernel(S, hy_ref, prev_ref, next_ref, w_ref, b_ref, z_ref, x0_ref):
    tm, C = hy_ref.shape
    tok0 = pl.program_id(0) * tm
    h = hy_ref[...]
    row = lax.broadcasted_iota(jnp.int32, (tm, C), 0)
    prev_row = jnp.where(tok0 % S == 0, 0.0, prev_ref[7:8, :])
    next_row = jnp.where((tok0 + tm) % S == 0, 0.0, next_ref[0:1, :])
    up = jnp.where(row == 0, prev_row, pltpu.roll(h, 1, axis=0))
    dn = jnp.where(row == tm - 1, next_row, pltpu.roll(h, tm - 1, axis=0))
    u = up * w_ref[0:1, :] + h * w_ref[1:2, :] + dn * w_ref[2:3, :] + b_ref[...]
    c = C // 3
    x0_ref[...] = u[:, :c]
    z_ref[...] = u[:, 2 * c:] * u[:, c:2 * c]


def _hy_prep(hy, conv_w, conv_b, S, tm=512):
    T, C = hy.shape
    nb = tm // 8
    last = T // 8 - 1
    return pl.pallas_call(
        functools.partial(_hy_prep_kernel, S),
        grid=(T // tm,),
        in_specs=[pl.BlockSpec((tm, C), lambda i: (i, 0)),
                  pl.BlockSpec((8, C), lambda i: (jnp.maximum(i * nb - 1, 0), 0)),
                  pl.BlockSpec((8, C), lambda i: (jnp.minimum((i + 1) * nb, last), 0)),
                  pl.BlockSpec((3, C), lambda i: (0, 0)),
                  pl.BlockSpec((1, C), lambda i: (0, 0))],
        out_specs=[pl.BlockSpec((tm, C // 3), lambda i: (i, 0)),
                   pl.BlockSpec((tm, C // 3), lambda i: (i, 0))],
        out_shape=[jax.ShapeDtypeStruct((T, C // 3), F32), jax.ShapeDtypeStruct((T, C // 3), F32)],
        compiler_params=_params("parallel"),
        name="hyena_prep",
    )(hy, hy, hy, conv_w, conv_b.reshape(1, C))


def _hy_filter_kernel(L, pos_ref, dec_ref, w1_ref, b1_ref, fr_ref, w2_ref, b2_ref, w3_ref, g_ref):
    tr, C = g_ref.shape
    fr = fr_ref[...]
    w1h, w1l = _split(w1_ref[...])
    w2h, w2l = _split(w2_ref[...])
    w3h, w3l = _split(w3_ref[...])
    ph, plo = _split(pos_ref[...])
    h = jnp.sin(fr * (_dot3(ph, plo, w1h, w1l) + _dot(plo, w1l) + b1_ref[...]))
    hh, hl = _split(h)
    h = jnp.sin(fr * (_dot3(hh, hl, w2h, w2l) + _dot(hl, w2l) + b2_ref[...]))
    hh, hl = _split(h)
    h = _dot3(hh, hl, w3h, w3l) + _dot(hl, w3l)
    dec = dec_ref[...]
    hf = h[:, :C] * dec
    hb = h[:, C:] * dec
    n = pl.program_id(0) * tr + lax.broadcasted_iota(jnp.int32, (tr, C), 0)
    g = jnp.where(n < L, hf, hb)
    g = jnp.where(n == L, 0.0, g)
    g_ref[...] = jnp.where(n == 0, hf + hb, g)


def _hy_filter(L, pos_ext, dec_ext, w1p, b1, freq, w2, b2, w3, tr=1024):
    N = 2 * L
    tr = min(tr, N)
    H = HY_FFN_HIDDEN
    full = lambda shape: pl.BlockSpec(shape, lambda i: (0, 0))
    return pl.pallas_call(
        functools.partial(_hy_filter_kernel, L),
        grid=(N // tr,),
        in_specs=[pl.BlockSpec((tr, pos_ext.shape[1]), lambda i: (i, 0)),
                  pl.BlockSpec((tr, HY_WIDTH), lambda i: (i, 0)),
                  full(w1p.shape), full((1, H)), full((1, H)), full((H, H)), full((1, H)),
                  full((H, 2 * HY_WIDTH))],
        out_specs=pl.BlockSpec((tr, HY_WIDTH), lambda i: (i, 0)),
        out_shape=jax.ShapeDtypeStruct((N, HY_WIDTH), F32),
        compiler_params=_params("parallel"),
        name="hyena_filter",
    )(pos_ext, dec_ext, w1p, b1.reshape(1, H), freq.reshape(1, H), w2, b2.reshape(1, H), w3)


def _dft_pair_kernel(ma_ref, mb_ref, x_ref, or_ref, oi_ref):
    xh, xl = _split(x_ref[...])
    or_ref[...] = _dot3(ma_ref[0], ma_ref[1], xh, xl)
    oi_ref[...] = _dot3(mb_ref[0], mb_ref[1], xh, xl)


def _dft_pair(ma, mb, x, tc=2048):
    R = ma.shape[1]
    K, C = x.shape
    return pl.pallas_call(
        _dft_pair_kernel,
        grid=(C // tc,),
        in_specs=[pl.BlockSpec(ma.shape, lambda j: (0, 0, 0)),
                  pl.BlockSpec(mb.shape, lambda j: (0, 0, 0)),
                  pl.BlockSpec((K, tc), lambda j: (0, j))],
        out_specs=[pl.BlockSpec((R, tc), lambda j: (0, j)), pl.BlockSpec((R, tc), lambda j: (0, j))],
        out_shape=[jax.ShapeDtypeStruct((R, C), F32), jax.ShapeDtypeStruct((R, C), F32)],
        compiler_params=_params("parallel"),
        name="dft_outer",
    )(ma, mb, x)


def _spec_filter_kernel(ar_ref, ai_ref, wf_ref, g_ref):
    ah, al = _split(jnp.concatenate([ar_ref[0], ai_ref[0]], axis=0))
    g_ref[0] = _dot3(wf_ref[0, 0], wf_ref[0, 1], ah, al)


def _spec_filter(ar, ai, wf):
    n1, n2, C = ar.shape
    return pl.pallas_call(
        _spec_filter_kernel,
        grid=(n1,),
        in_specs=[pl.BlockSpec((1, n2, C), lambda k: (k, 0, 0)),
                  pl.BlockSpec((1, n2, C), lambda k: (k, 0, 0)),
                  pl.BlockSpec((1, 2, 2 * n2, 2 * n2), lambda k: (k, 0, 0, 0))],
        out_specs=pl.BlockSpec((1, 2 * n2, C), lambda k: (k, 0, 0)),
        out_shape=jax.ShapeDtypeStruct((n1, 2 * n2, C), F32),
        compiler_params=_params("parallel"),
        name="filter_spectrum",
    )(ar, ai, wf)


def _spec_conv_kernel(ar_ref, ai_ref, wf_ref, wi_ref, g_ref, br_ref, bi_ref):
    n2 = ar_ref.shape[1]
    ah, al = _split(jnp.concatenate([ar_ref[0], ai_ref[0]], axis=0))
    x = _dot3(wf_ref[0, 0], wf_ref[0, 1], ah, al)
    xr, xi = x[:n2], x[n2:]
    gr, gi = g_ref[0, :n2], g_ref[0, n2:]
    y = jnp.concatenate([xr * gr - xi * gi, xr * gi + xi * gr], axis=0)
    yh, yl = _split(y)
    b = _dot3(wi_ref[0, 0], wi_ref[0, 1], yh, yl)
    br_ref[0] = b[:n2]
    bi_ref[0] = b[n2:]


def _spec_conv(ar, ai, wf, wi, gspec):
    n1, n2, C = ar.shape
    blk = pl.BlockSpec((1, n2, C), lambda k: (k, 0, 0))
    wblk = pl.BlockSpec((1, 2, 2 * n2, 2 * n2), lambda k: (k, 0, 0, 0))
    return pl.pallas_call(
        _spec_conv_kernel,
        grid=(n1,),
        in_specs=[blk, blk, wblk, wblk, pl.BlockSpec((1, 2 * n2, C), lambda k: (k, 0, 0))],
        out_specs=[blk, blk],
        out_shape=[jax.ShapeDtypeStruct((n1, n2, C), F32), jax.ShapeDtypeStruct((n1, n2, C), F32)],
        compiler_params=_params("parallel"),
        name="spectrum_conv",
    )(ar, ai, wf, wi, gspec)


def _idft_out_kernel(ma_ref, mb_ref, br_ref, bi_ref, z_ref, x0_ref, bias_ref, o_ref):
    brh, brl = _split(br_ref[...])
    bih, bil = _split(bi_ref[...])
    y = _dot3(ma_ref[0], ma_ref[1], brh, brl) + _dot3(mb_ref[0], mb_ref[1], bih, bil)
    o_ref[...] = ((y + z_ref[...] * bias_ref[...]) * x0_ref[...]).astype(o_ref.dtype)


def _idft_out(ma, mb, br, bi, z2, x02, bias_t, tc=2048):
    R = ma.shape[1]
    K, C = br.shape
    col = lambda rws: pl.BlockSpec((rws, tc), lambda j: (0, j))
    return pl.pallas_call(
        _idft_out_kernel,
        grid=(C // tc,),
        in_specs=[pl.BlockSpec(ma.shape, lambda j: (0, 0, 0)), pl.BlockSpec(mb.shape, lambda j: (0, 0, 0)),
                  col(K), col(K), col(R), col(R), pl.BlockSpec((1, tc), lambda j: (0, 0))],
        out_specs=col(R),
        out_shape=jax.ShapeDtypeStruct((R, C), BF16),
        compiler_params=_params("parallel"),
        name="idft_outer",
    )(ma, mb, br, bi, z2, x02, bias_t)


def _split_const(m):
    m32 = jnp.asarray(np.asarray(m, np.float32))
    hi, lo = _split(m32)
    return jnp.stack([hi, lo])


def _fft_constants(L):
    N = 2 * L
    n2 = FFT_N2
    n1 = N // n2
    h = n1 // 2
    k1 = np.arange(n1)[:, None].astype(np.float64)
    ang = 2.0 * np.pi * k1 * np.arange(n1)[None, :] / n1
    c, s = np.cos(ang), np.sin(ang)
    za = np.concatenate([c[:, :h], s[:, :h]], axis=1)
    zb = np.concatenate([-s[:, :h], c[:, :h]], axis=1)
    ga, gb = c, -s
    ya = np.concatenate([c[:h], s[:h]], axis=0)
    yb = np.concatenate([-s[:h], c[:h]], axis=0)
    kk = np.arange(n2)[:, None].astype(np.float64)
    nn = np.arange(n2)[None, :].astype(np.float64)
    base = 2.0 * np.pi * kk * nn / n2
    tw = 2.0 * np.pi * np.arange(n1)[:, None, None] * nn[None] / N
    f = lambda m: jnp.asarray(np.asarray(m, np.float32))
    cb, sb, ct, st = f(np.cos(base))[None], f(np.sin(base))[None], f(np.cos(tw)), f(np.sin(tw))
    pr = cb * ct - sb * st
    pi = -(sb * ct + cb * st)
    wf = jnp.concatenate([jnp.concatenate([pr, -pi], axis=2), jnp.concatenate([pi, pr], axis=2)], axis=1)
    qr = jnp.transpose(pr, (0, 2, 1)) * (1.0 / N)
    qi = jnp.transpose(-pi, (0, 2, 1)) * (1.0 / N)
    wi = jnp.concatenate([jnp.concatenate([qr, -qi], axis=2), jnp.concatenate([qi, qr], axis=2)], axis=1)
    sp = lambda m: _split_const(m)
    wfs = jnp.stack([*_split(wf)], axis=1)
    wis = jnp.stack([*_split(wi)], axis=1)
    return dict(za=sp(za), zb=sp(zb), ga=sp(ga), gb=sp(gb), ya=sp(ya), yb=sp(yb), wf=wfs, wi=wis, n1=n1)


def _hy_pos_tables(L):
    t = jnp.linspace(0.0, 1.0, L, dtype=F32)[:, None]
    w = 2.0 * math.pi * jnp.arange(L, dtype=F32)[:, None] / L
    f = jnp.linspace(1e-4, HY_POS_BANDS - 1, HY_POS_BANDS, dtype=F32)[None, :]
    z = jnp.concatenate([t, jnp.cos(f * w), -jnp.sin(f * w)], axis=-1)
    max_decay = math.log(HY_DECAY_TARGET) / HY_FAST_PCT
    min_decay = math.log(HY_DECAY_TARGET) / HY_SLOW_PCT
    deltas = jnp.linspace(min_decay, max_decay, HY_WIDTH, dtype=F32)
    decay = jnp.exp(-t * jnp.abs(deltas)[None, :])
    idx = jnp.concatenate([jnp.arange(L), jnp.array([0]), jnp.arange(L - 1, 0, -1)])
    zp = jnp.pad(z, ((0, 0), (0, HY_FFN_HIDDEN - HY_POS_DIM)))
    return zp[idx], decay[idx]


_TRI_ROWS = 16 + 8 * 7 + 8


def _tri_tables(tm):
    flat = [b for b in range(16)]
    for a in range(1, 8):
        flat += [16 * a + b for b in range(8)]
    flat += [16 * a for a in range(8, 16)]
    return jnp.asarray(np.tile(np.asarray(flat, np.float32)[:, None], (1, tm)))


def _topk_rows(s, k):
    n, tm = s.shape
    rowf = lax.broadcasted_iota(jnp.int32, (n, tm), 0).astype(F32)
    rowk = lax.broadcasted_iota(jnp.int32, (k, tm), 0)
    vals = jnp.zeros((k, tm), F32)
    rank = jnp.full((n, tm), float(k), F32)
    idxs = []
    for r in range(k):
        m = jnp.max(s, axis=0, keepdims=True)
        idx = jnp.min(jnp.where(s == m, rowf, float(n)), axis=0, keepdims=True)
        sel = rowf == idx
        s = jnp.where(sel, -jnp.inf, s)
        rank = jnp.where(sel, float(r), rank)
        vals = jnp.where(rowk == r, m, vals)
        idxs.append(idx)
    return vals, rank, idxs


def _peer_topk_kernel(q_ref, sk_ref, flat_ref, l_ref, e1_ref, rb_ref, e2_ref):
    K = PEER_TOPK
    tm = q_ref.shape[0]
    q = q_ref[...]
    svs, ranks, idxs, scores = [], [], [], []
    for p in range(2):
        qh, ql = _split(q[:, p * PEER_HALF:(p + 1) * PEER_HALF])
        kh, kl = _split(sk_ref[0, p])
        nt = (((1,), (1,)), ((), ()))
        dg = lambda a, b: lax.dot_general(a, b, nt, preferred_element_type=F32)
        s = dg(kh, qh) + dg(kh, ql) + dg(kl, qh)
        v, rk, ix = _topk_rows(s, K)
        scores.append(s); svs.append(v); ranks.append(rk); idxs.append(ix)
    sv1, sv2 = svs
    def cells(t1, t2, op):
        pieces = [op(t1[0:1], t2)]
        pieces += [op(t1[a:a + 1], t2[0:8]) for a in range(1, 8)]
        pieces += [op(t1[8:16], t2[0:1])]
        return jnp.concatenate(pieces, axis=0)
    cand = cells(sv1, sv2, lambda x, y: x + y)
    flat = flat_ref[...]
    row16 = lax.broadcasted_iota(jnp.int32, (K, tm), 0).astype(F32)
    length = jnp.zeros((K, tm), F32)
    for r in range(K):
        m = jnp.max(cand, axis=0, keepdims=True)
        f = jnp.min(jnp.where(cand == m, flat, 1e9), axis=0, keepdims=True)
        cand = jnp.where(flat == f, -jnp.inf, cand)
        a = jnp.floor(f * (1.0 / K))
        length = jnp.where(row16 == a, length + 1.0, length)
    es1 = jnp.exp(sv1 - sv1[0:1])
    es2 = jnp.exp(sv2 - sv2[0:1])
    ecand = cells(es1, es2, lambda x, y: x * y)
    z = jnp.sum(jnp.where(cand == -jnp.inf, ecand, 0.0), axis=0, keepdims=True)
    rank1 = ranks[0]
    lfull = jnp.zeros_like(rank1)
    for r in range(K):
        lfull = jnp.where(rank1 == float(r), length[r:r + 1], lfull)
    l_ref[0] = lfull
    e1_ref[0] = jnp.exp(scores[0] - sv1[0:1]) / z
    rb_ref[0] = ranks[1]
    e2_ref[0] = jnp.exp(scores[1] - sv2[0:1])


def _peer_topk(q, subkeys, tm=256):
    T = q.shape[0]
    H = PEER_HEADS
    tm = min(tm, T)
    out = jax.ShapeDtypeStruct((H, PEER_NKEYS, T), F32)
    oblk = pl.BlockSpec((1, PEER_NKEYS, tm), lambda i, h: (h, 0, i))
    return pl.pallas_call(
        _peer_topk_kernel,
        grid=(T // tm, H),
        in_specs=[pl.BlockSpec((tm, 2 * PEER_HALF), lambda i, h: (i, h)),
                  pl.BlockSpec((1, 2, PEER_NKEYS, PEER_HALF), lambda i, h: (h, 0, 0, 0)),
                  pl.BlockSpec((_TRI_ROWS, tm), lambda i, h: (0, 0))],
        out_specs=[oblk, oblk, oblk, oblk],
        out_shape=[out, out, out, out],
        compiler_params=_params("parallel", "parallel"),
        name="peer_topk",
    )(q, subkeys, _tri_tables(tm))


def _gelu_tanh(x):
    return 0.5 * x * (1.0 + jnp.tanh(math.sqrt(2.0 / math.pi) * (x + 0.044715 * (x * x * x))))


def _peer_dense_kernel(ni, hnT_ref, u_ref, vT_ref, l_ref, e1_ref, rb_ref, e2_ref, x_ref, o_ref,
                       acc_ref, act_ref):
    j = pl.program_id(1)
    nk = PEER_NKEYS

    @pl.when(j == 0)
    def _():
        acc_ref[...] = jnp.zeros_like(acc_ref)

    aT = _dot(u_ref[...], hnT_ref[...])
    i0 = pl.multiple_of(j * ni, ni)
    for ii in range(ni):
        w = None
        for h in range(PEER_HEADS):
            lrow = l_ref[h, pl.ds(i0 + ii, 1), :]
            erow = e1_ref[h, pl.ds(i0 + ii, 1), :]
            wh = jnp.where(rb_ref[h] < lrow, e2_ref[h], 0.0) * erow
            w = wh if w is None else w + wh
        a = aT[ii * nk:(ii + 1) * nk, :]
        act_ref[ii * nk:(ii + 1) * nk, :] = (_gelu_tanh(a) * w).astype(BF16)
    acc_ref[...] += _dot(vT_ref[...], act_ref[...])

    @pl.when(j == pl.num_programs(1) - 1)
    def _():
        o_ref[...] = x_ref[...] + acc_ref[...].T


def _peer_dense(hnT, u_bf, vT_bf, tabs, x, tm=512, ni=8):
    D, T = hnT.shape
    NE = u_bf.shape[0]
    te = ni * PEER_NKEYS
    tm = min(tm, T)
    tab = pl.BlockSpec((PEER_HEADS, PEER_NKEYS, tm), lambda i, j: (0, 0, i))
    return pl.pallas_call(
        functools.partial(_peer_dense_kernel, ni),
        grid=(T // tm, NE // te),
        in_specs=[pl.BlockSpec((D, tm), lambda i, j: (0, i)),
                  pl.BlockSpec((te, D), lambda i, j: (j, 0)),
                  pl.BlockSpec((D, te), lambda i, j: (0, j)),
                  tab, tab, tab, tab,
                  pl.BlockSpec((tm, D), lambda i, j: (i, 0))],
        out_specs=pl.BlockSpec((tm, D), lambda i, j: (i, 0)),
        out_shape=jax.ShapeDtypeStruct((T, D), F32),
        scratch_shapes=[pltpu.VMEM((D, tm), F32), pltpu.VMEM((te, tm), BF16)],
        compiler_params=_params("parallel", "arbitrary"),
        name="peer_experts",
    )(hnT, u_bf, vT_bf, *tabs, x)


def kernel(x, norm1_g, w_in, gate_b, na_q_g, na_k_g, na_rpb, hy_conv_w, hy_conv_b, hy_w1, hy_b1, hy_freq,
           hy_w2, hy_b2, hy_w3, hy_bias, w_up_na, w_up_hy, w_out, norm2_g, peer_wq, peer_subkeys, peer_u,
           peer_v):
    B, S, D = x.shape
    assert B == 2, "the long convolution packs the two batches as one complex sequence"
    depth = w_in.shape[0]
    T = B * S
    xt = x.reshape(T, D)

    fc = _fft_constants(S)
    n1 = fc["n1"]
    pos_ext, dec_ext = _hy_pos_tables(S)
    head_sum = jnp.asarray(np.kron(np.eye(NA_HEADS), np.ones((NA_HEAD_DIM, NA_HEAD_DIM))), BF16)
    o_qk, o_v, o_hy = 2 * NA_WIDTH, 3 * NA_WIDTH, 3 * NA_WIDTH + 3 * HY_WIDTH
    tc = 2048
    ncol = FFT_N2 * HY_WIDTH

    for l in range(depth):
        hn, _ = _rmsnorm(xt, norm1_g[l])
        w = w_in[l].astype(BF16)
        qk_gain = jnp.concatenate([jnp.tile(na_q_g[l], NA_HEADS) * (NA_HEAD_DIM ** -0.5),
                                   jnp.tile(na_k_g[l], NA_HEADS)]).reshape(1, o_qk).astype(F32)
        qk = _mm(hn, w[:, :o_qk], out_dtype=BF16, epi=_epi_head_rmsnorm,
                 extras=[(qk_gain, "col"), (head_sum, "full")], name="proj_qk")
        v = _mm(hn, w[:, o_qk:o_v], out_dtype=BF16, name="proj_v")
        hy = _mm(hn, w[:, o_v:o_hy], out_dtype=F32, name="proj_hy")
        gates = _mm(hn, w[:, o_hy:], out_dtype=BF16, epi=_epi_gate,
                    extras=[(gate_b[l].reshape(1, 2 * D), "col")], name="proj_gates")

        a_out = _na(qk, v, _na_bias_table(na_rpb[l]), B, S)

        z, x0 = _hy_prep(hy, hy_conv_w[l], hy_conv_b[l], S)
        w1p = jnp.pad(hy_w1[l], ((0, HY_FFN_HIDDEN - HY_POS_DIM), (0, 0)))
        g = _hy_filter(S, pos_ext, dec_ext, w1p, hy_b1[l], hy_freq[l], hy_w2[l], hy_b2[l], hy_w3[l])
        gr, gi = _dft_pair(fc["ga"], fc["gb"], g.reshape(n1, ncol), tc)
        gspec = _spec_filter(gr.reshape(n1, FFT_N2, HY_WIDTH), gi.reshape(n1, FFT_N2, HY_WIDTH), fc["wf"])
        z2 = z.reshape(n1, ncol)
        ar, ai = _dft_pair(fc["za"], fc["zb"], z2, tc)
        br, bi = _spec_conv(ar.reshape(n1, FFT_N2, HY_WIDTH), ai.reshape(n1, FFT_N2, HY_WIDTH),
                            fc["wf"], fc["wi"], gspec)
        bias_t = jnp.tile(hy_bias[l], tc // HY_WIDTH).reshape(1, tc)
        b_out = _idft_out(fc["ya"], fc["yb"], br.reshape(n1, ncol), bi.reshape(n1, ncol),
                          z2, x0.reshape(n1, ncol), bias_t, tc).reshape(T, HY_WIDTH)

        merged = _merge(a_out, b_out, w_up_na[l].astype(BF16), w_up_hy[l].astype(BF16), gates)
        xt = _mm(merged, w_out[l].astype(BF16), out_dtype=F32, epi=_epi_residual,
                 extras=[(xt, "tile")], name="proj_out")

        hn2, hn2T = _rmsnorm(xt, norm2_g[l])
        q = _mm(hn2, peer_wq[l].astype(BF16), out_dtype=F32, name="peer_query")
        tabs = _peer_topk(q, peer_subkeys[l])
        xt = _peer_dense(hn2T, peer_u[l].astype(BF16), peer_v[l].T.astype(BF16), tabs, xt)
    return xt.reshape(B, S, D)
```

```python
import functools
import math

import numpy as np
import jax
import jax.numpy as jnp
from jax import lax
from jax.experimental import pallas as pl
from jax.experimental.pallas import tpu as pltpu

F32 = jnp.float32
BF16 = jnp.bfloat16

GRID_W = 64
NA_HEADS = 8
NA_HEAD_DIM = 64
NA_WIDTH = NA_HEADS * NA_HEAD_DIM
NA_KR = 8
NA_KW = 16
HY_WIDTH = 512
HY_POS_BANDS = 16
HY_POS_DIM = 1 + 2 * HY_POS_BANDS
HY_FFN_HIDDEN = 64
HY_DECAY_TARGET = 1e-2
HY_FAST_PCT = 0.3
HY_SLOW_PCT = 1.5
PEER_HEADS = 8
PEER_NKEYS = 128
PEER_TOPK = 16
PEER_HALF = 128
EPS = 1e-6
NEG = -1e30

LANES = 128
VMEM_LIMIT_BYTES = 56 * 1024 * 1024
FFT_N2 = 128
BF16_ROWS = 16


def _params(*sem):
    return pltpu.CompilerParams(dimension_semantics=sem, vmem_limit_bytes=VMEM_LIMIT_BYTES)


def _split(x):
    hi = x.astype(BF16)
    lo = (x - hi.astype(F32)).astype(BF16)
    return hi, lo


def _dot(a, b):
    return jnp.dot(a, b, preferred_element_type=F32)


def _dot3(ah, al, bh, bl):
    return _dot(ah, bh) + _dot(ah, bl) + _dot(al, bh)


def _rmsnorm_kernel(x_ref, g_ref, o_ref, ot_ref):
    x = x_ref[...]
    y = x * lax.rsqrt(jnp.mean(x * x, axis=-1, keepdims=True) + EPS)
    y = y * g_ref[...]
    o_ref[...] = y.astype(BF16)
    ot_ref[...] = y.T.astype(BF16)


def _rmsnorm(x, g, tm=512):
    T, D = x.shape
    return pl.pallas_call(
        _rmsnorm_kernel,
        grid=(T // tm,),
        in_specs=[pl.BlockSpec((tm, D), lambda i: (i, 0)),
                  pl.BlockSpec((1, D), lambda i: (0, 0))],
        out_specs=[pl.BlockSpec((tm, D), lambda i: (i, 0)),
                   pl.BlockSpec((D, tm), lambda i: (0, i))],
        out_shape=[jax.ShapeDtypeStruct((T, D), BF16), jax.ShapeDtypeStruct((D, T), BF16)],
        compiler_params=_params("parallel"),
        name="rmsnorm",
    )(x, g.reshape(1, D))


def _mm_kernel(epi, n_extra, a_ref, w_ref, *rest):
    o_ref = rest[n_extra]
    acc = _dot(a_ref[...], w_ref[...])
    o_ref[...] = epi(acc, *[r[...] for r in rest[:n_extra]]).astype(o_ref.dtype)


def _mm(a, w, *, out_dtype, epi=lambda acc: acc, extras=(), tm=1024, tn=512, name="mm"):
    M, K = a.shape
    N = w.shape[1]
    tm, tn = min(tm, M), min(tn, N)
    in_specs = [pl.BlockSpec((tm, K), lambda i, j: (i, 0)),
                pl.BlockSpec((K, tn), lambda i, j: (0, j))]
    args = [a, w]
    for arr, kind in extras:
        if kind == "col":
            in_specs.append(pl.BlockSpec((1, tn), lambda i, j: (0, j)))
        elif kind == "tile":
            in_specs.append(pl.BlockSpec((tm, tn), lambda i, j: (i, j)))
        else:
            in_specs.append(pl.BlockSpec(arr.shape, lambda i, j, nd=arr.ndim: (0,) * nd))
        args.append(arr)
    return pl.pallas_call(
        functools.partial(_mm_kernel, epi, len(extras)),
        grid=(M // tm, N // tn),
        in_specs=in_specs,
        out_specs=pl.BlockSpec((tm, tn), lambda i, j: (i, j)),
        out_shape=jax.ShapeDtypeStruct((M, N), out_dtype),
        compiler_params=_params("parallel", "parallel"),
        name=name,
    )(*args)


def _epi_head_rmsnorm(acc, gain, head_sum):
    hi, lo = _split(acc * acc)
    ms = (_dot(hi, head_sum) + _dot(lo, head_sum)) * (1.0 / NA_HEAD_DIM)
    return acc * lax.rsqrt(ms + EPS) * gain


def _epi_gate(acc, bias):
    return jax.nn.sigmoid(acc + bias)


def _epi_residual(acc, res):
    return res + acc


def _merge_kernel(a_ref, b_ref, wa_ref, wb_ref, g_ref, o_ref):
    d = o_ref.shape[1]
    ya = _dot(a_ref[...], wa_ref[...])
    yb = _dot(b_ref[...], wb_ref[...])
    g = g_ref[...].astype(F32)
    o_ref[...] = (g[:, :d] * ya + g[:, d:] * yb).astype(o_ref.dtype)


def _merge(a_out, b_out, w_na, w_hy, gates, tm=512):
    T, wa = a_out.shape
    D = w_na.shape[1]
    return pl.pallas_call(
        _merge_kernel,
        grid=(T // tm,),
        in_specs=[pl.BlockSpec((tm, wa), lambda i: (i, 0)),
                  pl.BlockSpec((tm, b_out.shape[1]), lambda i: (i, 0)),
                  pl.BlockSpec(w_na.shape, lambda i: (0, 0)),
                  pl.BlockSpec(w_hy.shape, lambda i: (0, 0)),
                  pl.BlockSpec((tm, 2 * D), lambda i: (i, 0))],
        out_specs=pl.BlockSpec((tm, D), lambda i: (i, 0)),
        out_shape=jax.ShapeDtypeStruct((T, D), BF16),
        compiler_params=_params("parallel"),
        name="merge",
    )(a_out, b_out, w_na, w_hy, gates)


def _na_kernel(rows, rb, q_ref, k_ref, v_ref, bias_ref, o_ref, s_ref, p_ref):
    blk = pl.program_id(2)
    lane = lax.broadcasted_iota(jnp.int32, (GRID_W, LANES), 1)
    nk = NA_KR * GRID_W
    starts = []
    for i in range(rb):
        r = blk * rb + i
        r0 = jnp.clip(r - NA_KR // 2, 0, rows - NA_KR)
        dr0 = r0 - r + (NA_KR - 1)
        start = pl.multiple_of(r0 * GRID_W, GRID_W)
        starts.append(start)
        qr = q_ref[i * GRID_W:(i + 1) * GRID_W, :]
        kw = k_ref[pl.ds(start, nk), :]
        for hh in range(2):
            in_head = (lane >= NA_HEAD_DIM) if hh else (lane < NA_HEAD_DIM)
            qm = jnp.where(in_head, qr, jnp.zeros_like(qr))
            s = lax.dot_general(qm, kw, (((1,), (1,)), ((), ())), preferred_element_type=F32)
            s_ref[(2 * i + hh) * GRID_W:(2 * i + hh + 1) * GRID_W, :] = s + bias_ref[hh, dr0]
    s = s_ref[...]
    p = jnp.exp(s - jnp.max(s, axis=-1, keepdims=True))
    p_ref[...] = (p * (1.0 / jnp.sum(p, axis=-1, keepdims=True))).astype(BF16)
    for i in range(rb):
        vw = v_ref[pl.ds(starts[i], nk), :]
        o0 = _dot(p_ref[(2 * i) * GRID_W:(2 * i + 1) * GRID_W, :], vw)
        o1 = _dot(p_ref[(2 * i + 1) * GRID_W:(2 * i + 2) * GRID_W, :], vw)
        o_ref[i * GRID_W:(i + 1) * GRID_W, :] = jnp.where(lane < NA_HEAD_DIM, o0, o1).astype(o_ref.dtype)


def _na_bias_table(rpb):
    c = np.arange(GRID_W)
    c0 = np.clip(c - NA_KW // 2, 0, GRID_W - NA_KW)
    col_in = (c[None, :] >= c0[:, None]) & (c[None, :] < c0[:, None] + NA_KW)
    dc_idx = np.clip(c[None, :] - c[:, None] + (NA_KW - 1), 0, 2 * NA_KW - 2)
    onehot = jnp.asarray((dc_idx[:, :, None] == np.arange(2 * NA_KW - 1)).astype(np.float32))
    rows = jnp.stack([rpb.astype(F32)[:, d:d + NA_KR] for d in range(NA_KR)], axis=1)
    b = jnp.einsum("hdjc,qkc->hdqjk", rows, onehot, precision=lax.Precision.HIGHEST)
    b = jnp.where(jnp.asarray(col_in)[None, None, :, None, :], b, NEG)
    return b.reshape(NA_HEADS, NA_KR, GRID_W, NA_KR * GRID_W)


def _na(qk, v, bias, B, S, rb=8):
    rows = S // GRID_W
    assert rows >= NA_KR and rows % rb == 0
    nblk = rows // rb
    tq = rb * GRID_W
    kofs = NA_WIDTH // LANES
    return pl.pallas_call(
        functools.partial(_na_kernel, rows, rb),
        grid=(B, NA_HEADS // 2, nblk),
        in_specs=[pl.BlockSpec((tq, LANES), lambda b, p, r: (b * nblk + r, p)),
                  pl.BlockSpec((S, LANES), lambda b, p, r: (b, kofs + p)),
                  pl.BlockSpec((S, LANES), lambda b, p, r: (b, p)),
                  pl.BlockSpec((2, NA_KR, GRID_W, NA_KR * GRID_W), lambda b, p, r: (p, 0, 0, 0))],
        out_specs=pl.BlockSpec((tq, LANES), lambda b, p, r: (b * nblk + r, p)),
        out_shape=jax.ShapeDtypeStruct((B * S, NA_WIDTH), BF16),
        scratch_shapes=[pltpu.VMEM((2 * tq, NA_KR * GRID_W), F32), pltpu.VMEM((2 * tq, NA_KR * GRID_W), BF16)],
        compiler_params=_params("parallel", "parallel", "parallel"),
        name="na_attention",
    )(qk, qk, v, bias)


def _hy_prep_kernel(S, hy_ref, prev_ref, next_ref, w_ref, b_ref, z_ref, x0_ref):
    tm, C = hy_ref.shape
    tok0 = pl.program_id(0) * tm
    h = hy_ref[...]
    row = lax.broadcasted_iota(jnp.int32, (tm, C), 0)
    prev_row = jnp.where(tok0 % S == 0, 0.0, prev_ref[7:8, :])
    next_row = jnp.where((tok0 + tm) % S == 0, 0.0, next_ref[0:1, :])
    up = jnp.where(row == 0, prev_row, pltpu.roll(h, 1, axis=0))
    dn = jnp.where(row == tm - 1, next_row, pltpu.roll(h, tm - 1, axis=0))
    u = up * w_ref[0:1, :] + h * w_ref[1:2, :] + dn * w_ref[2:3, :] + b_ref[...]
    c = C // 3
    x0_ref[...] = u[:, :c]
    z_ref[...] = u[:, 2 * c:] * u[:, c:2 * c]


def _hy_prep(hy, conv_w, conv_b, S, tm=512):
    T, C = hy.shape
    nb = tm // 8
    last = T // 8 - 1
    return pl.pallas_call(
        functools.partial(_hy_prep_kernel, S),
        grid=(T // tm,),
        in_specs=[pl.BlockSpec((tm, C), lambda i: (i, 0)),
                  pl.BlockSpec((8, C), lambda i: (jnp.maximum(i * nb - 1, 0), 0)),
                  pl.BlockSpec((8, C), lambda i: (jnp.minimum((i + 1) * nb, last), 0)),
                  pl.BlockSpec((3, C), lambda i: (0, 0)),
                  pl.BlockSpec((1, C), lambda i: (0, 0))],
        out_specs=[pl.BlockSpec((tm, C // 3), lambda i: (i, 0)),
                   pl.BlockSpec((tm, C // 3), lambda i: (i, 0))],
        out_shape=[jax.ShapeDtypeStruct((T, C // 3), F32), jax.ShapeDtypeStruct((T, C // 3), F32)],
        compiler_params=_params("parallel"),
        name="hyena_prep",
    )(hy, hy, hy, conv_w, conv_b.reshape(1, C))


def _hy_filter_kernel(L, pos_ref, dec_ref, w1_ref, b1_ref, fr_ref, w2_ref, b2_ref, w3_ref, g_ref):
    tr, C = g_ref.shape
    fr = fr_ref[...]
    w1h, w1l = _split(w1_ref[...])
    w2h, w2l = _split(w2_ref[...])
    w3h, w3l = _split(w3_ref[...])
    ph, plo = _split(pos_ref[...])
    h = jnp.sin(fr * (_dot3(ph, plo, w1h, w1l) + _dot(plo, w1l) + b1_ref[...]))
    hh, hl = _split(h)
    h = jnp.sin(fr * (_dot3(hh, hl, w2h, w2l) + _dot(hl, w2l) + b2_ref[...]))
    hh, hl = _split(h)
    h = _dot3(hh, hl, w3h, w3l) + _dot(hl, w3l)
    dec = dec_ref[...]
    hf = h[:, :C] * dec
    hb = h[:, C:] * dec
    n = pl.program_id(0) * tr + lax.broadcasted_iota(jnp.int32, (tr, C), 0)
    g = jnp.where(n < L, hf, hb)
    g = jnp.where(n == L, 0.0, g)
    g_ref[...] = jnp.where(n == 0, hf + hb, g)


def _hy_filter(L, pos_ext, dec_ext, w1p, b1, freq, w2, b2, w3, tr=1024):
    N = 2 * L
    tr = min(tr, N)
    H = HY_FFN_HIDDEN
    full = lambda shape: pl.BlockSpec(shape, lambda i: (0, 0))
    return pl.pallas_call(
        functools.partial(_hy_filter_kernel, L),
        grid=(N // tr,),
        in_specs=[pl.BlockSpec((tr, pos_ext.shape[1]), lambda i: (i, 0)),
                  pl.BlockSpec((tr, HY_WIDTH), lambda i: (i, 0)),
                  full(w1p.shape), full((1, H)), full((1, H)), full((H, H)), full((1, H)),
                  full((H, 2 * HY_WIDTH))],
        out_specs=pl.BlockSpec((tr, HY_WIDTH), lambda i: (i, 0)),
        out_shape=jax.ShapeDtypeStruct((N, HY_WIDTH), F32),
        compiler_params=_params("parallel"),
        name="hyena_filter",
    )(pos_ext, dec_ext, w1p, b1.reshape(1, H), freq.reshape(1, H), w2, b2.reshape(1, H), w3)


def _dft_pair_kernel(ma_ref, mb_ref, x_ref, or_ref, oi_ref):
    xh, xl = _split(x_ref[...])
    or_ref[...] = _dot3(ma_ref[0], ma_ref[1], xh, xl)
    oi_ref[...] = _dot3(mb_ref[0], mb_ref[1], xh, xl)


def _dft_pair(ma, mb, x, tc=2048):
    R = ma.shape[1]
    K, C = x.shape
    return pl.pallas_call(
        _dft_pair_kernel,
        grid=(C // tc,),
        in_specs=[pl.BlockSpec(ma.shape, lambda j: (0, 0, 0)),
                  pl.BlockSpec(mb.shape, lambda j: (0, 0, 0)),
                  pl.BlockSpec((K, tc), lambda j: (0, j))],
        out_specs=[pl.BlockSpec((R, tc), lambda j: (0, j)), pl.BlockSpec((R, tc), lambda j: (0, j))],
        out_shape=[jax.ShapeDtypeStruct((R, C), F32), jax.ShapeDtypeStruct((R, C), F32)],
        compiler_params=_params("parallel"),
        name="dft_outer",
    )(ma, mb, x)


def _spec_filter_kernel(ar_ref, ai_ref, wf_ref, g_ref):
    ah, al = _split(jnp.concatenate([ar_ref[0], ai_ref[0]], axis=0))
    g_ref[0] = _dot3(wf_ref[0, 0], wf_ref[0, 1], ah, al)


def _spec_filter(ar, ai, wf):
    n1, n2, C = ar.shape
    return pl.pallas_call(
        _spec_filter_kernel,
        grid=(n1,),
        in_specs=[pl.BlockSpec((1, n2, C), lambda k: (k, 0, 0)),
                  pl.BlockSpec((1, n2, C), lambda k: (k, 0, 0)),
                  pl.BlockSpec((1, 2, 2 * n2, 2 * n2), lambda k: (k, 0, 0, 0))],
        out_specs=pl.BlockSpec((1, 2 * n2, C), lambda k: (k, 0, 0)),
        out_shape=jax.ShapeDtypeStruct((n1, 2 * n2, C), F32),
        compiler_params=_params("parallel"),
        name="filter_spectrum",
    )(ar, ai, wf)


def _spec_conv_kernel(ar_ref, ai_ref, wf_ref, wi_ref, g_ref, br_ref, bi_ref):
    n2 = ar_ref.shape[1]
    ah, al = _split(jnp.concatenate([ar_ref[0], ai_ref[0]], axis=0))
    x = _dot3(wf_ref[0, 0], wf_ref[0, 1], ah, al)
    xr, xi = x[:n2], x[n2:]
    gr, gi = g_ref[0, :n2], g_ref[0, n2:]
    y = jnp.concatenate([xr * gr - xi * gi, xr * gi + xi * gr], axis=0)
    yh, yl = _split(y)
    b = _dot3(wi_ref[0, 0], wi_ref[0, 1], yh, yl)
    br_ref[0] = b[:n2]
    bi_ref[0] = b[n2:]


def _spec_conv(ar, ai, wf, wi, gspec):
    n1, n2, C = ar.shape
    blk = pl.BlockSpec((1, n2, C), lambda k: (k, 0, 0))
    wblk = pl.BlockSpec((1, 2, 2 * n2, 2 * n2), lambda k: (k, 0, 0, 0))
    return pl.pallas_call(
        _spec_conv_kernel,
        grid=(n1,),
        in_specs=[blk, blk, wblk, wblk, pl.BlockSpec((1, 2 * n2, C), lambda k: (k, 0, 0))],
        out_specs=[blk, blk],
        out_shape=[jax.ShapeDtypeStruct((n1, n2, C), F32), jax.ShapeDtypeStruct((n1, n2, C), F32)],
        compiler_params=_params("parallel"),
        name="spectrum_conv",
    )(ar, ai, wf, wi, gspec)


def _idft_out_kernel(ma_ref, mb_ref, br_ref, bi_ref, z_ref, x0_ref, bias_ref, o_ref):
    brh, brl = _split(br_ref[...])
    bih, bil = _split(bi_ref[...])
    y = _dot3(ma_ref[0], ma_ref[1], brh, brl) + _dot3(mb_ref[0], mb_ref[1], bih, bil)
    o_ref[...] = ((y + z_ref[...] * bias_ref[...]) * x0_ref[...]).astype(o_ref.dtype)


def _idft_out(ma, mb, br, bi, z2, x02, bias_t, tc=2048):
    R = ma.shape[1]
    K, C = br.shape
    col = lambda rws: pl.BlockSpec((rws, tc), lambda j: (0, j))
    return pl.pallas_call(
        _idft_out_kernel,
        grid=(C // tc,),
        in_specs=[pl.BlockSpec(ma.shape, lambda j: (0, 0, 0)), pl.BlockSpec(mb.shape, lambda j: (0, 0, 0)),
                  col(K), col(K), col(R), col(R), pl.BlockSpec((1, tc), lambda j: (0, 0))],
        out_specs=col(R),
        out_shape=jax.ShapeDtypeStruct((R, C), BF16),
        compiler_params=_params("parallel"),
        name="idft_outer",
    )(ma, mb, br, bi, z2, x02, bias_t)


def _split_const(m):
    m32 = jnp.asarray(np.asarray(m, np.float32))
    hi, lo = _split(m32)
    return jnp.stack([hi, lo])


def _fft_constants(L):
    N = 2 * L
    n2 = FFT_N2
    n1 = N // n2
    h = n1 // 2
    k1 = np.arange(n1)[:, None].astype(np.float64)
    ang = 2.0 * np.pi * k1 * np.arange(n1)[None, :] / n1
    c, s = np.cos(ang), np.sin(ang)
    za = np.concatenate([c[:, :h], s[:, :h]], axis=1)
    zb = np.concatenate([-s[:, :h], c[:, :h]], axis=1)
    ga, gb = c, -s
    ya = np.concatenate([c[:h], s[:h]], axis=0)
    yb = np.concatenate([-s[:h], c[:h]], axis=0)
    kk = np.arange(n2)[:, None].astype(np.float64)
    nn = np.arange(n2)[None, :].astype(np.float64)
    base = 2.0 * np.pi * kk * nn / n2
    tw = 2.0 * np.pi * np.arange(n1)[:, None, None] * nn[None] / N
    f = lambda m: jnp.asarray(np.asarray(m, np.float32))
    cb, sb, ct, st = f(np.cos(base))[None], f(np.sin(base))[None], f(np.cos(tw)), f(np.sin(tw))
    pr = cb * ct - sb * st
    pi = -(sb * ct + cb * st)
    wf = jnp.concatenate([jnp.concatenate([pr, -pi], axis=2), jnp.concatenate([pi, pr], axis=2)], axis=1)
    qr = jnp.transpose(pr, (0, 2, 1)) * (1.0 / N)
    qi = jnp.transpose(-pi, (0, 2, 1)) * (1.0 / N)
    wi = jnp.concatenate([jnp.concatenate([qr, -qi], axis=2), jnp.concatenate([qi, qr], axis=2)], axis=1)
    sp = lambda m: _split_const(m)
    wfs = jnp.stack([*_split(wf)], axis=1)
    wis = jnp.stack([*_split(wi)], axis=1)
    return dict(za=sp(za), zb=sp(zb), ga=sp(ga), gb=sp(gb), ya=sp(ya), yb=sp(yb), wf=wfs, wi=wis, n1=n1)


def _hy_pos_tables(L):
    t = jnp.linspace(0.0, 1.0, L, dtype=F32)[:, None]
    w = 2.0 * math.pi * jnp.arange(L, dtype=F32)[:, None] / L
    f = jnp.linspace(1e-4, HY_POS_BANDS - 1, HY_POS_BANDS, dtype=F32)[None, :]
    z = jnp.concatenate([t, jnp.cos(f * w), -jnp.sin(f * w)], axis=-1)
    max_decay = math.log(HY_DECAY_TARGET) / HY_FAST_PCT
    min_decay = math.log(HY_DECAY_TARGET) / HY_SLOW_PCT
    deltas = jnp.linspace(min_decay, max_decay, HY_WIDTH, dtype=F32)
    decay = jnp.exp(-t * jnp.abs(deltas)[None, :])
    idx = jnp.concatenate([jnp.arange(L), jnp.array([0]), jnp.arange(L - 1, 0, -1)])
    zp = jnp.pad(z, ((0, 0), (0, HY_FFN_HIDDEN - HY_POS_DIM)))
    return zp[idx], decay[idx]


_TRI_ROWS = 16 + 8 * 7 + 8


def _tri_tables(tm):
    flat = [b for b in range(16)]
    for a in range(1, 8):
        flat += [16 * a + b for b in range(8)]
    flat += [16 * a for a in range(8, 16)]
    return jnp.asarray(np.tile(np.asarray(flat, np.float32)[:, None], (1, tm)))


def _topk_rows(s, k, exact):
    n, tm = s.shape
    rowf = lax.broadcasted_iota(jnp.int32, (n, tm), 0).astype(F32)
    rowk = lax.broadcasted_iota(jnp.int32, (k, tm), 0)
    vals = jnp.zeros((k, tm), F32)
    rank = jnp.full((n, tm), float(k), F32)
    for r in range(k):
        m = jnp.max(s, axis=0, keepdims=True)
        sel = s == m
        if exact:
            sel = rowf == jnp.min(jnp.where(sel, rowf, float(n)), axis=0, keepdims=True)
        s = jnp.where(sel, -jnp.inf, s)
        rank = jnp.where(sel, float(r), rank)
        vals = jnp.where(rowk == r, m, vals)
    return vals, rank


def _stair_cells(t1, t2, op):
    pieces = [op(t1[0:1], t2)]
    pieces += [op(t1[a:a + 1], t2[0:8]) for a in range(1, 8)]
    pieces += [op(t1[8:16], t2[0:1])]
    return jnp.concatenate(pieces, axis=0)


def _peer_tables(scores, flat, exact):
    K = PEER_TOPK
    tm = scores[0].shape[1]
    (sv1, rank1), (sv2, rank2) = [_topk_rows(s, K, exact) for s in scores]
    cand = _stair_cells(sv1, sv2, lambda x, y: x + y)
    if exact:
        row16 = lax.broadcasted_iota(jnp.int32, (K, tm), 0).astype(F32)
        length = jnp.zeros((K, tm), F32)
        for r in range(K):
            m = jnp.max(cand, axis=0, keepdims=True)
            f = jnp.min(jnp.where(cand == m, flat, 1e9), axis=0, keepdims=True)
            cand = jnp.where(flat == f, -jnp.inf, cand)
            length = jnp.where(row16 == jnp.floor(f * (1.0 / K)), length + 1.0, length)
        picked = cand == -jnp.inf
    else:
        for r in range(K):
            cand = jnp.where(cand == jnp.max(cand, axis=0, keepdims=True), -jnp.inf, cand)
        picked = cand == -jnp.inf
        cnt = jnp.where(picked, 1.0, 0.0)
        rows = [jnp.sum(cnt[0:16], axis=0, keepdims=True)]
        rows += [jnp.sum(cnt[8 + 8 * a:16 + 8 * a], axis=0, keepdims=True) for a in range(1, 8)]
        length = jnp.concatenate(rows + [cnt[72:80]], axis=0)
    count = lambda rk: jnp.sum(jnp.where(rk < float(K), 1.0, 0.0), axis=0, keepdims=True)
    ok = (count(rank1) == float(K)) & (count(rank2) == float(K)) & \
         (jnp.sum(length, axis=0, keepdims=True) == float(K))
    es1 = jnp.exp(sv1 - sv1[0:1])
    es2 = jnp.exp(sv2 - sv2[0:1])
    ecand = _stair_cells(es1, es2, lambda x, y: x * y)
    z = jnp.sum(jnp.where(picked, ecand, 0.0), axis=0, keepdims=True)
    lfull = jnp.zeros_like(rank1)
    for r in range(K):
        lfull = jnp.where(rank1 == float(r), length[r:r + 1], lfull)
    e1 = jnp.exp(scores[0] - sv1[0:1]) / z
    e2 = jnp.exp(scores[1] - sv2[0:1])
    return (lfull, e1, rank2, e2), jnp.where(ok, 1.0, 0.0)


def _peer_topk_kernel(q_ref, sk_ref, flat_ref, l_ref, e1_ref, rb_ref, e2_ref):
    q = q_ref[...]
    scores = []
    for p in range(2):
        qh, ql = _split(q[:, p * PEER_HALF:(p + 1) * PEER_HALF])
        kh, kl = _split(sk_ref[0, p])
        nt = (((1,), (1,)), ((), ()))
        dg = lambda a, b: lax.dot_general(a, b, nt, preferred_element_type=F32)
        scores.append(dg(kh, qh) + dg(kh, ql) + dg(kl, qh))

    def write(tabs):
        l_ref[0], e1_ref[0] = tabs[0], tabs[1]
        rb_ref[0], e2_ref[0] = tabs[2].astype(rb_ref.dtype), tabs[3].astype(e2_ref.dtype)

    tabs, ok = _peer_tables(scores, flat_ref[...], exact=False)
    write(tabs)

    @pl.when(jnp.min(ok) < 0.5)
    def _():
        write(_peer_tables(scores, flat_ref[...], exact=True)[0])


def _peer_topk(q, subkeys, tm=256):
    T = q.shape[0]
    H = PEER_HEADS
    tm = min(tm, T)
    out = jax.ShapeDtypeStruct((H, PEER_NKEYS, T), F32)
    outb = jax.ShapeDtypeStruct((H, PEER_NKEYS, T), BF16)
    oblk = pl.BlockSpec((1, PEER_NKEYS, tm), lambda i, h: (h, 0, i))
    return pl.pallas_call(
        _peer_topk_kernel,
        grid=(T // tm, H),
        in_specs=[pl.BlockSpec((tm, 2 * PEER_HALF), lambda i, h: (i, h)),
                  pl.BlockSpec((1, 2, PEER_NKEYS, PEER_HALF), lambda i, h: (h, 0, 0, 0)),
                  pl.BlockSpec((_TRI_ROWS, tm), lambda i, h: (0, 0))],
        out_specs=[oblk, oblk, oblk, oblk],
        out_shape=[out, out, outb, outb],
        compiler_params=_params("parallel", "parallel"),
        name="peer_topk",
    )(q, subkeys, _tri_tables(tm))


def _gelu_tanh(x):
    return 0.5 * x * (1.0 + jnp.tanh(math.sqrt(2.0 / math.pi) * (x + 0.044715 * (x * x * x))))


def _peer_dense_kernel(ni, hnT_ref, u_ref, vT_ref, l_ref, e1_ref, rb_ref, e2_ref, x_ref, o_ref,
                       acc_ref, act_ref):
    j = pl.program_id(1)
    nk = PEER_NKEYS

    @pl.when(j == 0)
    def _():
        acc_ref[...] = jnp.zeros_like(acc_ref)

    aT = _dot(u_ref[...], hnT_ref[...])
    tm = aT.shape[1]
    i0 = pl.multiple_of(j * ni, ni)
    zero = jnp.zeros((), BF16)
    for ii in range(ni):
        w = None
        for h in range(PEER_HEADS):
            lrow = jnp.broadcast_to(l_ref[h, pl.ds(i0 + ii, 1), :], (BF16_ROWS, tm)).astype(BF16)
            erow = jnp.broadcast_to(e1_ref[h, pl.ds(i0 + ii, 1), :], (BF16_ROWS, tm)).astype(BF16)
            wh = jnp.where(rb_ref[h] < lrow[None], e2_ref[h], zero) * erow[None]
            w = wh if w is None else w + wh
        a = aT[ii * nk:(ii + 1) * nk, :]
        act_ref[ii * nk:(ii + 1) * nk, :] = _gelu_tanh(a).astype(BF16) * w.reshape(nk, tm)
    acc_ref[...] += _dot(vT_ref[...], act_ref[...])

    @pl.when(j == pl.num_programs(1) - 1)
    def _():
        o_ref[...] = x_ref[...] + acc_ref[...].T


def _peer_dense(hnT, u_bf, vT_bf, tabs, x, tm=512, ni=8):
    D, T = hnT.shape
    NE = u_bf.shape[0]
    te = ni * PEER_NKEYS
    tm = min(tm, T)
    tab = pl.BlockSpec((PEER_HEADS, PEER_NKEYS, tm), lambda i, j: (0, 0, i))
    grp = PEER_NKEYS // BF16_ROWS
    tabb = pl.BlockSpec((PEER_HEADS, grp, BF16_ROWS, tm), lambda i, j: (0, 0, 0, i))
    lt, e1, rb, e2 = tabs
    rb, e2 = [t.reshape(PEER_HEADS, grp, BF16_ROWS, T) for t in (rb, e2)]
    return pl.pallas_call(
        functools.partial(_peer_dense_kernel, ni),
        grid=(T // tm, NE // te),
        in_specs=[pl.BlockSpec((D, tm), lambda i, j: (0, i)),
                  pl.BlockSpec((te, D), lambda i, j: (j, 0)),
                  pl.BlockSpec((D, te), lambda i, j: (0, j)),
                  tab, tab, tabb, tabb,
                  pl.BlockSpec((tm, D), lambda i, j: (i, 0))],
        out_specs=pl.BlockSpec((tm, D), lambda i, j: (i, 0)),
        out_shape=jax.ShapeDtypeStruct((T, D), F32),
        scratch_shapes=[pltpu.VMEM((D, tm), F32), pltpu.VMEM((te, tm), BF16)],
        compiler_params=_params("parallel", "arbitrary"),
        name="peer_experts",
    )(hnT, u_bf, vT_bf, lt, e1, rb, e2, x)


def kernel(x, norm1_g, w_in, gate_b, na_q_g, na_k_g, na_rpb, hy_conv_w, hy_conv_b, hy_w1, hy_b1, hy_freq,
           hy_w2, hy_b2, hy_w3, hy_bias, w_up_na, w_up_hy, w_out, norm2_g, peer_wq, peer_subkeys, peer_u,
           peer_v):
    B, S, D = x.shape
    assert B == 2, "the long convolution packs the two batches as one complex sequence"
    depth = w_in.shape[0]
    T = B * S
    xt = x.reshape(T, D)

    fc = _fft_constants(S)
    n1 = fc["n1"]
    pos_ext, dec_ext = _hy_pos_tables(S)
    head_sum = jnp.asarray(np.kron(np.eye(NA_HEADS), np.ones((NA_HEAD_DIM, NA_HEAD_DIM))), BF16)
    o_qk, o_v, o_hy = 2 * NA_WIDTH, 3 * NA_WIDTH, 3 * NA_WIDTH + 3 * HY_WIDTH
    tc = 2048
    ncol = FFT_N2 * HY_WIDTH

    for l in range(depth):
        hn, _ = _rmsnorm(xt, norm1_g[l])
        w = w_in[l].astype(BF16)
        qk_gain = jnp.concatenate([jnp.tile(na_q_g[l], NA_HEADS) * (NA_HEAD_DIM ** -0.5),
                                   jnp.tile(na_k_g[l], NA_HEADS)]).reshape(1, o_qk).astype(F32)
        qk = _mm(hn, w[:, :o_qk], out_dtype=BF16, epi=_epi_head_rmsnorm,
                 extras=[(qk_gain, "col"), (head_sum, "full")], name="proj_qk")
        v = _mm(hn, w[:, o_qk:o_v], out_dtype=BF16, name="proj_v")
        hy = _mm(hn, w[:, o_v:o_hy], out_dtype=F32, name="proj_hy")
        gates = _mm(hn, w[:, o_hy:], out_dtype=BF16, epi=_epi_gate,
                    extras=[(gate_b[l].reshape(1, 2 * D), "col")], name="proj_gates")

        a_out = _na(qk, v, _na_bias_table(na_rpb[l]), B, S)

        z, x0 = _hy_prep(hy, hy_conv_w[l], hy_conv_b[l], S)
        w1p = jnp.pad(hy_w1[l], ((0, HY_FFN_HIDDEN - HY_POS_DIM), (0, 0)))
        g = _hy_filter(S, pos_ext, dec_ext, w1p, hy_b1[l], hy_freq[l], hy_w2[l], hy_b2[l], hy_w3[l])
        gr, gi = _dft_pair(fc["ga"], fc["gb"], g.reshape(n1, ncol), tc)
        gspec = _spec_filter(gr.reshape(n1, FFT_N2, HY_WIDTH), gi.reshape(n1, FFT_N2, HY_WIDTH), fc["wf"])
        z2 = z.reshape(n1, ncol)
        ar, ai = _dft_pair(fc["za"], fc["zb"], z2, tc)
        br, bi = _spec_conv(ar.reshape(n1, FFT_N2, HY_WIDTH), ai.reshape(n1, FFT_N2, HY_WIDTH),
                            fc["wf"], fc["wi"], gspec)
        bias_t = jnp.tile(hy_bias[l], tc // HY_WIDTH).reshape(1, tc)
        b_out = _idft_out(fc["ya"], fc["yb"], br.reshape(n1, ncol), bi.reshape(n1, ncol),
                          z2, x0.reshape(n1, ncol), bias_t, tc).reshape(T, HY_WIDTH)

        merged = _merge(a_out, b_out, w_up_na[l].astype(BF16), w_up_hy[l].astype(BF16), gates)
        xt = _mm(merged, w_out[l].astype(BF16), out_dtype=F32, epi=_epi_residual,
                 extras=[(xt, "tile")], name="proj_out")

        hn2, hn2T = _rmsnorm(xt, norm2_g[l])
        q = _mm(hn2, peer_wq[l].astype(BF16), out_dtype=F32, name="peer_query")
        tabs = _peer_topk(q, peer_subkeys[l])
        xt = _peer_dense(hn2T, peer_u[l].astype(BF16), peer_v[l].T.astype(BF16), tabs, xt)
    return xt.reshape(B, S, D)
```

```python
import functools
import math

import numpy as np
import jax
import jax.numpy as jnp
from jax import lax
from jax.experimental import pallas as pl
from jax.experimental.pallas import tpu as pltpu

F32 = jnp.float32
BF16 = jnp.bfloat16

GRID_W = 64
NA_HEADS = 8
NA_HEAD_DIM = 64
NA_WIDTH = NA_HEADS * NA_HEAD_DIM
NA_KR = 8
NA_KW = 16
HY_WIDTH = 512
HY_POS_BANDS = 16
HY_POS_DIM = 1 + 2 * HY_POS_BANDS
HY_FFN_HIDDEN = 64
HY_DECAY_TARGET = 1e-2
HY_FAST_PCT = 0.3
HY_SLOW_PCT = 1.5
PEER_HEADS = 8
PEER_NKEYS = 128
PEER_TOPK = 16
PEER_HALF = 128
EPS = 1e-6
NEG = -1e30

LANES = 128
VMEM_LIMIT_BYTES = 56 * 1024 * 1024
FFT_N2 = 128
BF16_ROWS = 16
PEER_SPLIT = 1


def _params(*sem):
    return pltpu.CompilerParams(dimension_semantics=sem, vmem_limit_bytes=VMEM_LIMIT_BYTES)


def _split(x):
    hi = x.astype(BF16)
    lo = (x - hi.astype(F32)).astype(BF16)
    return hi, lo


def _dot(a, b):
    return jnp.dot(a, b, preferred_element_type=F32)


def _dot3(ah, al, bh, bl):
    return _dot(ah, bh) + _dot(ah, bl) + _dot(al, bh)


def _rmsnorm_kernel(x_ref, g_ref, o_ref, ot_ref):
    x = x_ref[...]
    y = x * lax.rsqrt(jnp.mean(x * x, axis=-1, keepdims=True) + EPS)
    y = y * g_ref[...]
    o_ref[...] = y.astype(BF16)
    ot_ref[...] = y.T.astype(BF16)


def _rmsnorm(x, g, tm=512):
    T, D = x.shape
    return pl.pallas_call(
        _rmsnorm_kernel,
        grid=(T // tm,),
        in_specs=[pl.BlockSpec((tm, D), lambda i: (i, 0)),
                  pl.BlockSpec((1, D), lambda i: (0, 0))],
        out_specs=[pl.BlockSpec((tm, D), lambda i: (i, 0)),
                   pl.BlockSpec((D, tm), lambda i: (0, i))],
        out_shape=[jax.ShapeDtypeStruct((T, D), BF16), jax.ShapeDtypeStruct((D, T), BF16)],
        compiler_params=_params("parallel"),
        name="rmsnorm",
    )(x, g.reshape(1, D))


def _mm_kernel(epi, n_extra, a_ref, w_ref, *rest):
    o_ref = rest[n_extra]
    acc = _dot(a_ref[...], w_ref[...])
    o_ref[...] = epi(acc, *[r[...] for r in rest[:n_extra]]).astype(o_ref.dtype)


def _mm(a, w, *, out_dtype, epi=lambda acc: acc, extras=(), tm=1024, tn=512, split_cols=False, name="mm"):
    M, K = a.shape
    N = w.shape[1]
    tm, tn = min(tm, M), min(tn, N)
    if split_cols:
        out_spec = pl.BlockSpec((None, tm, tn), lambda i, j: (j, i, 0))
        out_shape = jax.ShapeDtypeStruct((N // tn, M, tn), out_dtype)
    else:
        out_spec = pl.BlockSpec((tm, tn), lambda i, j: (i, j))
        out_shape = jax.ShapeDtypeStruct((M, N), out_dtype)
    in_specs = [pl.BlockSpec((tm, K), lambda i, j: (i, 0)),
                pl.BlockSpec((K, tn), lambda i, j: (0, j))]
    args = [a, w]
    for arr, kind in extras:
        if kind == "col":
            in_specs.append(pl.BlockSpec((1, tn), lambda i, j: (0, j)))
        elif kind == "tile":
            in_specs.append(pl.BlockSpec((tm, tn), lambda i, j: (i, j)))
        else:
            in_specs.append(pl.BlockSpec(arr.shape, lambda i, j, nd=arr.ndim: (0,) * nd))
        args.append(arr)
    return pl.pallas_call(
        functools.partial(_mm_kernel, epi, len(extras)),
        grid=(M // tm, N // tn),
        in_specs=in_specs,
        out_specs=out_spec,
        out_shape=out_shape,
        compiler_params=_params("parallel", "parallel"),
        name=name,
    )(*args)


def _epi_head_rmsnorm(acc, gain, head_sum):
    hi, lo = _split(acc * acc)
    ms = (_dot(hi, head_sum) + _dot(lo, head_sum)) * (1.0 / NA_HEAD_DIM)
    return acc * lax.rsqrt(ms + EPS) * gain


def _epi_gate(acc, bias):
    return jax.nn.sigmoid(acc + bias)


def _epi_residual(acc, res):
    return res + acc


def _merge_kernel(a_ref, b_ref, wa_ref, wb_ref, g_ref, o_ref):
    d = o_ref.shape[1]
    ya = _dot(a_ref[...], wa_ref[...])
    yb = _dot(b_ref[...], wb_ref[...])
    g = g_ref[...].astype(F32)
    o_ref[...] = (g[:, :d] * ya + g[:, d:] * yb).astype(o_ref.dtype)


def _merge(a_out, b_out, w_na, w_hy, gates, tm=512):
    T, wa = a_out.shape
    D = w_na.shape[1]
    return pl.pallas_call(
        _merge_kernel,
        grid=(T // tm,),
        in_specs=[pl.BlockSpec((tm, wa), lambda i: (i, 0)),
                  pl.BlockSpec((tm, b_out.shape[1]), lambda i: (i, 0)),
                  pl.BlockSpec(w_na.shape, lambda i: (0, 0)),
                  pl.BlockSpec(w_hy.shape, lambda i: (0, 0)),
                  pl.BlockSpec((tm, 2 * D), lambda i: (i, 0))],
        out_specs=pl.BlockSpec((tm, D), lambda i: (i, 0)),
        out_shape=jax.ShapeDtypeStruct((T, D), BF16),
        compiler_params=_params("parallel"),
        name="merge",
    )(a_out, b_out, w_na, w_hy, gates)


def _na_kernel(rows, rb, q_ref, k_ref, v_ref, bias_ref, o_ref, s_ref, p_ref):
    blk = pl.program_id(2)
    lane = lax.broadcasted_iota(jnp.int32, (GRID_W, LANES), 1)
    nk = NA_KR * GRID_W
    starts = []
    for i in range(rb):
        r = blk * rb + i
        r0 = jnp.clip(r - NA_KR // 2, 0, rows - NA_KR)
        dr0 = r0 - r + (NA_KR - 1)
        start = pl.multiple_of(r0 * GRID_W, GRID_W)
        starts.append(start)
        qr = q_ref[i * GRID_W:(i + 1) * GRID_W, :]
        kw = k_ref[pl.ds(start, nk), :]
        for hh in range(2):
            in_head = (lane >= NA_HEAD_DIM) if hh else (lane < NA_HEAD_DIM)
            qm = jnp.where(in_head, qr, jnp.zeros_like(qr))
            s = lax.dot_general(qm, kw, (((1,), (1,)), ((), ())), preferred_element_type=F32)
            s_ref[(2 * i + hh) * GRID_W:(2 * i + hh + 1) * GRID_W, :] = s + bias_ref[hh, dr0]
    s = s_ref[...]
    p = jnp.exp(s - jnp.max(s, axis=-1, keepdims=True))
    p_ref[...] = (p * (1.0 / jnp.sum(p, axis=-1, keepdims=True))).astype(BF16)
    for i in range(rb):
        vw = v_ref[pl.ds(starts[i], nk), :]
        o0 = _dot(p_ref[(2 * i) * GRID_W:(2 * i + 1) * GRID_W, :], vw)
        o1 = _dot(p_ref[(2 * i + 1) * GRID_W:(2 * i + 2) * GRID_W, :], vw)
        o_ref[i * GRID_W:(i + 1) * GRID_W, :] = jnp.where(lane < NA_HEAD_DIM, o0, o1).astype(o_ref.dtype)


def _na_bias_table(rpb):
    c = np.arange(GRID_W)
    c0 = np.clip(c - NA_KW // 2, 0, GRID_W - NA_KW)
    col_in = (c[None, :] >= c0[:, None]) & (c[None, :] < c0[:, None] + NA_KW)
    dc_idx = np.clip(c[None, :] - c[:, None] + (NA_KW - 1), 0, 2 * NA_KW - 2)
    onehot = jnp.asarray((dc_idx[:, :, None] == np.arange(2 * NA_KW - 1)).astype(np.float32))
    rows = jnp.stack([rpb.astype(F32)[:, d:d + NA_KR] for d in range(NA_KR)], axis=1)
    b = jnp.einsum("hdjc,qkc->hdqjk", rows, onehot, precision=lax.Precision.HIGHEST)
    b = jnp.where(jnp.asarray(col_in)[None, None, :, None, :], b, NEG)
    return b.reshape(NA_HEADS, NA_KR, GRID_W, NA_KR * GRID_W)


def _na(qk, v, bias, B, S, rb=8):
    rows = S // GRID_W
    assert rows >= NA_KR and rows % rb == 0
    nblk = rows // rb
    tq = rb * GRID_W
    kofs = NA_WIDTH // LANES
    return pl.pallas_call(
        functools.partial(_na_kernel, rows, rb),
        grid=(B, NA_HEADS // 2, nblk),
        in_specs=[pl.BlockSpec((tq, LANES), lambda b, p, r: (b * nblk + r, p)),
                  pl.BlockSpec((S, LANES), lambda b, p, r: (b, kofs + p)),
                  pl.BlockSpec((S, LANES), lambda b, p, r: (b, p)),
                  pl.BlockSpec((2, NA_KR, GRID_W, NA_KR * GRID_W), lambda b, p, r: (p, 0, 0, 0))],
        out_specs=pl.BlockSpec((tq, LANES), lambda b, p, r: (b * nblk + r, p)),
        out_shape=jax.ShapeDtypeStruct((B * S, NA_WIDTH), BF16),
        scratch_shapes=[pltpu.VMEM((2 * tq, NA_KR * GRID_W), F32), pltpu.VMEM((2 * tq, NA_KR * GRID_W), BF16)],
        compiler_params=_params("parallel", "parallel", "parallel"),
        name="na_attention",
    )(qk, qk, v, bias)


def _hy_prep_kernel(S, hy_ref, prev_ref, next_ref, w_ref, b_ref, z_ref, x0_ref):
    tm, C = hy_ref.shape
    tok0 = pl.program_id(0) * tm
    h = hy_ref[...]
    row = lax.broadcasted_iota(jnp.int32, (tm, C), 0)
    prev_row = jnp.where(tok0 % S == 0, 0.0, prev_ref[7:8, :])
    next_row = jnp.where((tok0 + tm) % S == 0, 0.0, next_ref[0:1, :])
    up = jnp.where(row == 0, prev_row, pltpu.roll(h, 1, axis=0))
    dn = jnp.where(row == tm - 1, next_row, pltpu.roll(h, tm - 1, axis=0))
    u = up * w_ref[0:1, :] + h * w_ref[1:2, :] + dn * w_ref[2:3, :] + b_ref[...]
    c = C // 3
    x0_ref[...] = u[:, :c]
    z_ref[...] = u[:, 2 * c:] * u[:, c:2 * c]


def _hy_prep(hy, conv_w, conv_b, S, tm=512):
    T, C = hy.shape
    nb = tm // 8
    last = T // 8 - 1
    return pl.pallas_call(
        functools.partial(_hy_prep_kernel, S),
        grid=(T // tm,),
        in_specs=[pl.BlockSpec((tm, C), lambda i: (i, 0)),
                  pl.BlockSpec((8, C), lambda i: (jnp.maximum(i * nb - 1, 0), 0)),
                  pl.BlockSpec((8, C), lambda i: (jnp.minimum((i + 1) * nb, last), 0)),
                  pl.BlockSpec((3, C), lambda i: (0, 0)),
                  pl.BlockSpec((1, C), lambda i: (0, 0))],
        out_specs=[pl.BlockSpec((tm, C // 3), lambda i: (i, 0)),
                   pl.BlockSpec((tm, C // 3), lambda i: (i, 0))],
        out_shape=[jax.ShapeDtypeStruct((T, C // 3), F32), jax.ShapeDtypeStruct((T, C // 3), F32)],
        compiler_params=_params("parallel"),
        name="hyena_prep",
    )(hy, hy, hy, conv_w, conv_b.reshape(1, C))


def _hy_filter_kernel(L, pos_ref, dec_ref, w1_ref, b1_ref, fr_ref, w2_ref, b2_ref, w3_ref, g_ref):
    tr, C = g_ref.shape
    fr = fr_ref[...]
    w1h, w1l = _split(w1_ref[...])
    w2h, w2l = _split(w2_ref[...])
    w3h, w3l = _split(w3_ref[...])
    ph, plo = _split(pos_ref[...])
    h = jnp.sin(fr * (_dot3(ph, plo, w1h, w1l) + _dot(plo, w1l) + b1_ref[...]))
    hh, hl = _split(h)
    h = jnp.sin(fr * (_dot3(hh, hl, w2h, w2l) + _dot(hl, w2l) + b2_ref[...]))
    hh, hl = _split(h)
    h = _dot3(hh, hl, w3h, w3l) + _dot(hl, w3l)
    dec = dec_ref[...]
    hf = h[:, :C] * dec
    hb = h[:, C:] * dec
    n = pl.program_id(0) * tr + lax.broadcasted_iota(jnp.int32, (tr, C), 0)
    g = jnp.where(n < L, hf, hb)
    g = jnp.where(n == L, 0.0, g)
    g_ref[...] = jnp.where(n == 0, hf + hb, g)


def _hy_filter(L, pos_ext, dec_ext, w1p, b1, freq, w2, b2, w3, tr=1024):
    N = 2 * L
    tr = min(tr, N)
    H = HY_FFN_HIDDEN
    full = lambda shape: pl.BlockSpec(shape, lambda i: (0, 0))
    return pl.pallas_call(
        functools.partial(_hy_filter_kernel, L),
        grid=(N // tr,),
        in_specs=[pl.BlockSpec((tr, pos_ext.shape[1]), lambda i: (i, 0)),
                  pl.BlockSpec((tr, HY_WIDTH), lambda i: (i, 0)),
                  full(w1p.shape), full((1, H)), full((1, H)), full((H, H)), full((1, H)),
                  full((H, 2 * HY_WIDTH))],
        out_specs=pl.BlockSpec((tr, HY_WIDTH), lambda i: (i, 0)),
        out_shape=jax.ShapeDtypeStruct((N, HY_WIDTH), F32),
        compiler_params=_params("parallel"),
        name="hyena_filter",
    )(pos_ext, dec_ext, w1p, b1.reshape(1, H), freq.reshape(1, H), w2, b2.reshape(1, H), w3)


def _dft_pair_kernel(ma_ref, mb_ref, x_ref, or_ref, oi_ref):
    xh, xl = _split(x_ref[...])
    or_ref[...] = _dot3(ma_ref[0], ma_ref[1], xh, xl)
    oi_ref[...] = _dot3(mb_ref[0], mb_ref[1], xh, xl)


def _dft_pair(ma, mb, x, tc=2048):
    R = ma.shape[1]
    K, C = x.shape
    return pl.pallas_call(
        _dft_pair_kernel,
        grid=(C // tc,),
        in_specs=[pl.BlockSpec(ma.shape, lambda j: (0, 0, 0)),
                  pl.BlockSpec(mb.shape, lambda j: (0, 0, 0)),
                  pl.BlockSpec((K, tc), lambda j: (0, j))],
        out_specs=[pl.BlockSpec((R, tc), lambda j: (0, j)), pl.BlockSpec((R, tc), lambda j: (0, j))],
        out_shape=[jax.ShapeDtypeStruct((R, C), F32), jax.ShapeDtypeStruct((R, C), F32)],
        compiler_params=_params("parallel"),
        name="dft_outer",
    )(ma, mb, x)


def _spec_filter_kernel(ar_ref, ai_ref, wf_ref, g_ref):
    ah, al = _split(jnp.concatenate([ar_ref[0], ai_ref[0]], axis=0))
    g_ref[0] = _dot3(wf_ref[0, 0], wf_ref[0, 1], ah, al)


def _spec_filter(ar, ai, wf):
    n1, n2, C = ar.shape
    return pl.pallas_call(
        _spec_filter_kernel,
        grid=(n1,),
        in_specs=[pl.BlockSpec((1, n2, C), lambda k: (k, 0, 0)),
                  pl.BlockSpec((1, n2, C), lambda k: (k, 0, 0)),
                  pl.BlockSpec((1, 2, 2 * n2, 2 * n2), lambda k: (k, 0, 0, 0))],
        out_specs=pl.BlockSpec((1, 2 * n2, C), lambda k: (k, 0, 0)),
        out_shape=jax.ShapeDtypeStruct((n1, 2 * n2, C), F32),
        compiler_params=_params("parallel"),
        name="filter_spectrum",
    )(ar, ai, wf)


def _spec_conv_kernel(ar_ref, ai_ref, wf_ref, wi_ref, g_ref, br_ref, bi_ref):
    n2 = ar_ref.shape[1]
    ah, al = _split(jnp.concatenate([ar_ref[0], ai_ref[0]], axis=0))
    x = _dot3(wf_ref[0, 0], wf_ref[0, 1], ah, al)
    xr, xi = x[:n2], x[n2:]
    gr, gi = g_ref[0, :n2], g_ref[0, n2:]
    y = jnp.concatenate([xr * gr - xi * gi, xr * gi + xi * gr], axis=0)
    yh, yl = _split(y)
    b = _dot3(wi_ref[0, 0], wi_ref[0, 1], yh, yl)
    br_ref[0] = b[:n2]
    bi_ref[0] = b[n2:]


def _spec_conv(ar, ai, wf, wi, gspec):
    n1, n2, C = ar.shape
    blk = pl.BlockSpec((1, n2, C), lambda k: (k, 0, 0))
    wblk = pl.BlockSpec((1, 2, 2 * n2, 2 * n2), lambda k: (k, 0, 0, 0))
    return pl.pallas_call(
        _spec_conv_kernel,
        grid=(n1,),
        in_specs=[blk, blk, wblk, wblk, pl.BlockSpec((1, 2 * n2, C), lambda k: (k, 0, 0))],
        out_specs=[blk, blk],
        out_shape=[jax.ShapeDtypeStruct((n1, n2, C), F32), jax.ShapeDtypeStruct((n1, n2, C), F32)],
        compiler_params=_params("parallel"),
        name="spectrum_conv",
    )(ar, ai, wf, wi, gspec)


def _idft_out_kernel(ma_ref, mb_ref, br_ref, bi_ref, z_ref, x0_ref, bias_ref, o_ref):
    brh, brl = _split(br_ref[...])
    bih, bil = _split(bi_ref[...])
    y = _dot3(ma_ref[0], ma_ref[1], brh, brl) + _dot3(mb_ref[0], mb_ref[1], bih, bil)
    o_ref[...] = ((y + z_ref[...] * bias_ref[...]) * x0_ref[...]).astype(o_ref.dtype)


def _idft_out(ma, mb, br, bi, z2, x02, bias_t, tc=2048):
    R = ma.shape[1]
    K, C = br.shape
    col = lambda rws: pl.BlockSpec((rws, tc), lambda j: (0, j))
    return pl.pallas_call(
        _idft_out_kernel,
        grid=(C // tc,),
        in_specs=[pl.BlockSpec(ma.shape, lambda j: (0, 0, 0)), pl.BlockSpec(mb.shape, lambda j: (0, 0, 0)),
                  col(K), col(K), col(R), col(R), pl.BlockSpec((1, tc), lambda j: (0, 0))],
        out_specs=col(R),
        out_shape=jax.ShapeDtypeStruct((R, C), BF16),
        compiler_params=_params("parallel"),
        name="idft_outer",
    )(ma, mb, br, bi, z2, x02, bias_t)


def _split_const(m):
    m32 = jnp.asarray(np.asarray(m, np.float32))
    hi, lo = _split(m32)
    return jnp.stack([hi, lo])


def _fft_constants(L):
    N = 2 * L
    n2 = FFT_N2
    n1 = N // n2
    h = n1 // 2
    k1 = np.arange(n1)[:, None].astype(np.float64)
    ang = 2.0 * np.pi * k1 * np.arange(n1)[None, :] / n1
    c, s = np.cos(ang), np.sin(ang)
    za = np.concatenate([c[:, :h], s[:, :h]], axis=1)
    zb = np.concatenate([-s[:, :h], c[:, :h]], axis=1)
    ga, gb = c, -s
    ya = np.concatenate([c[:h], s[:h]], axis=0)
    yb = np.concatenate([-s[:h], c[:h]], axis=0)
    kk = np.arange(n2)[:, None].astype(np.float64)
    nn = np.arange(n2)[None, :].astype(np.float64)
    base = 2.0 * np.pi * kk * nn / n2
    tw = 2.0 * np.pi * np.arange(n1)[:, None, None] * nn[None] / N
    f = lambda m: jnp.asarray(np.asarray(m, np.float32))
    cb, sb, ct, st = f(np.cos(base))[None], f(np.sin(base))[None], f(np.cos(tw)), f(np.sin(tw))
    pr = cb * ct - sb * st
    pi = -(sb * ct + cb * st)
    wf = jnp.concatenate([jnp.concatenate([pr, -pi], axis=2), jnp.concatenate([pi, pr], axis=2)], axis=1)
    qr = jnp.transpose(pr, (0, 2, 1)) * (1.0 / N)
    qi = jnp.transpose(-pi, (0, 2, 1)) * (1.0 / N)
    wi = jnp.concatenate([jnp.concatenate([qr, -qi], axis=2), jnp.concatenate([qi, qr], axis=2)], axis=1)
    sp = lambda m: _split_const(m)
    wfs = jnp.stack([*_split(wf)], axis=1)
    wis = jnp.stack([*_split(wi)], axis=1)
    return dict(za=sp(za), zb=sp(zb), ga=sp(ga), gb=sp(gb), ya=sp(ya), yb=sp(yb), wf=wfs, wi=wis, n1=n1)


def _hy_pos_tables(L):
    t = jnp.linspace(0.0, 1.0, L, dtype=F32)[:, None]
    w = 2.0 * math.pi * jnp.arange(L, dtype=F32)[:, None] / L
    f = jnp.linspace(1e-4, HY_POS_BANDS - 1, HY_POS_BANDS, dtype=F32)[None, :]
    z = jnp.concatenate([t, jnp.cos(f * w), -jnp.sin(f * w)], axis=-1)
    max_decay = math.log(HY_DECAY_TARGET) / HY_FAST_PCT
    min_decay = math.log(HY_DECAY_TARGET) / HY_SLOW_PCT
    deltas = jnp.linspace(min_decay, max_decay, HY_WIDTH, dtype=F32)
    decay = jnp.exp(-t * jnp.abs(deltas)[None, :])
    zp = jnp.pad(z, ((0, 0), (0, HY_FFN_HIDDEN - HY_POS_DIM)))
    ext = lambda a: jnp.concatenate([a, a[0:1], jnp.flip(a[1:], axis=0)], axis=0)
    return ext(zp), ext(decay)


_TRI_ROWS = 16 + 8 * 7 + 8


def _tri_tables(tm):
    flat = [b for b in range(16)]
    for a in range(1, 8):
        flat += [16 * a + b for b in range(8)]
    flat += [16 * a for a in range(8, 16)]
    return jnp.asarray(np.tile(np.asarray(flat, np.float32)[:, None], (1, tm)))


def _topk_rows(scores, k, exact):
    n, tm = scores[0].shape
    rowf = lax.broadcasted_iota(jnp.int32, (n, tm), 0).astype(F32)
    rowk = lax.broadcasted_iota(jnp.int32, (k, tm), 0)
    state = [(s, jnp.zeros((k, tm), F32), jnp.full((n, tm), float(k), F32)) for s in scores]
    for r in range(k):
        nxt = []
        for s, vals, rank in state:
            m = jnp.max(s, axis=0, keepdims=True)
            sel = s == m
            if exact:
                sel = rowf == jnp.min(jnp.where(sel, rowf, float(n)), axis=0, keepdims=True)
            nxt.append((jnp.where(sel, -jnp.inf, s), jnp.where(rowk == r, m, vals),
                        jnp.where(sel, float(r), rank)))
        state = nxt
    return [(vals, rank) for _, vals, rank in state]


def _stair_cells(t1, t2, op):
    pieces = [op(t1[0:1], t2)]
    pieces += [op(t1[a:a + 1], t2[0:8]) for a in range(1, 8)]
    pieces += [op(t1[8:16], t2[0:1])]
    return jnp.concatenate(pieces, axis=0)


def _peer_tables(scores, flat, exact):
    K = PEER_TOPK
    tm = scores[0].shape[1]
    (sv1, rank1), (sv2, rank2) = _topk_rows(scores, K, exact)
    cand = _stair_cells(sv1, sv2, lambda x, y: x + y)
    if exact:
        row16 = lax.broadcasted_iota(jnp.int32, (K, tm), 0).astype(F32)
        length = jnp.zeros((K, tm), F32)
        for r in range(K):
            m = jnp.max(cand, axis=0, keepdims=True)
            f = jnp.min(jnp.where(cand == m, flat, 1e9), axis=0, keepdims=True)
            cand = jnp.where(flat == f, -jnp.inf, cand)
            length = jnp.where(row16 == jnp.floor(f * (1.0 / K)), length + 1.0, length)
        picked = cand == -jnp.inf
    else:
        for r in range(K):
            cand = jnp.where(cand == jnp.max(cand, axis=0, keepdims=True), -jnp.inf, cand)
        picked = cand == -jnp.inf
        cnt = jnp.where(picked, 1.0, 0.0)
        rows = [jnp.sum(cnt[0:16], axis=0, keepdims=True)]
        rows += [jnp.sum(cnt[8 + 8 * a:16 + 8 * a], axis=0, keepdims=True) for a in range(1, 8)]
        length = jnp.concatenate(rows + [cnt[72:80]], axis=0)
    count = lambda rk: jnp.sum(jnp.where(rk < float(K), 1.0, 0.0), axis=0, keepdims=True)
    ok = (count(rank1) == float(K)) & (count(rank2) == float(K)) & \
         (jnp.sum(length, axis=0, keepdims=True) == float(K))
    es1 = jnp.exp(sv1 - sv1[0:1])
    es2 = jnp.exp(sv2 - sv2[0:1])
    ecand = _stair_cells(es1, es2, lambda x, y: x * y)
    z = jnp.sum(jnp.where(picked, ecand, 0.0), axis=0, keepdims=True)
    lfull = jnp.zeros_like(rank1)
    for r in range(K):
        lfull = jnp.where(rank1 == float(r), length[r:r + 1], lfull)
    e1 = jnp.exp(scores[0] - sv1[0:1]) / z
    e2 = jnp.exp(scores[1] - sv2[0:1])
    return (lfull, e1, rank2, e2), jnp.where(ok, 1.0, 0.0)


def _peer_topk_kernel(q_ref, sk_ref, flat_ref, l_ref, e1_ref, rb_ref, e2_ref):
    flat = flat_ref[...]

    def head(h, carry):
        q = q_ref[h]
        scores = []
        for p in range(2):
            qh, ql = _split(q[:, p * PEER_HALF:(p + 1) * PEER_HALF])
            kh, kl = _split(sk_ref[h, p])
            nt = (((1,), (1,)), ((), ()))
            dg = lambda a, b: lax.dot_general(a, b, nt, preferred_element_type=F32)
            scores.append(dg(kh, qh) + dg(kh, ql) + dg(kl, qh))

        def write(tabs):
            l_ref[h], e1_ref[h] = tabs[0], tabs[1]
            rb_ref[h], e2_ref[h] = tabs[2].astype(rb_ref.dtype), tabs[3].astype(e2_ref.dtype)

        tabs, ok = _peer_tables(scores, flat, exact=False)
        write(tabs)

        @pl.when(jnp.min(ok) < 0.5)
        def _():
            write(_peer_tables(scores, flat, exact=True)[0])
        return carry

    lax.fori_loop(0, PEER_HEADS, head, 0)


def _peer_topk(q, subkeys, tm=256):
    H, T, _ = q.shape
    tm = min(tm, T)
    out = jax.ShapeDtypeStruct((H, PEER_NKEYS, T), F32)
    outb = jax.ShapeDtypeStruct((H, PEER_NKEYS, T), BF16)
    oblk = pl.BlockSpec((H, PEER_NKEYS, tm), lambda i: (0, 0, i))
    return pl.pallas_call(
        _peer_topk_kernel,
        grid=(T // tm,),
        in_specs=[pl.BlockSpec((H, tm, 2 * PEER_HALF), lambda i: (0, i, 0)),
                  pl.BlockSpec((H, 2, PEER_NKEYS, PEER_HALF), lambda i: (0, 0, 0, 0)),
                  pl.BlockSpec((_TRI_ROWS, tm), lambda i: (0, 0))],
        out_specs=[oblk, oblk, oblk, oblk],
        out_shape=[out, out, outb, outb],
        compiler_params=_params("parallel"),
        name="peer_topk",
    )(q, subkeys, _tri_tables(tm))


def _gelu_tanh(x):
    c = math.sqrt(2.0 / math.pi)
    h = 0.5 * x
    return h + h * jnp.tanh(x * (c + (c * 0.044715) * (x * x)))


def _peer_dense_kernel(ni, hnT_ref, u_ref, vT_ref, l_ref, e1_ref, rb_ref, e2_ref, x_ref, o_ref,
                       acc_ref, act_ref, a_ref):
    j = pl.program_id(1)
    nk = PEER_NKEYS

    @pl.when(j == 0)
    def _():
        acc_ref[...] = jnp.zeros_like(acc_ref)

    tm = hnT_ref.shape[1]
    i0 = pl.multiple_of(j * ni, ni)
    zero = jnp.zeros((), BF16)
    for half in range(PEER_SPLIT):
        wd = tm // PEER_SPLIT
        hs = slice(half * wd, (half + 1) * wd)
        a_ref[half] = _dot(u_ref[...], hnT_ref[:, hs])
        for ii in range(ni):
            w = None
            for h in range(PEER_HEADS):
                lrow = jnp.broadcast_to(l_ref[h, pl.ds(i0 + ii, 1), :][:, hs], (BF16_ROWS, wd)).astype(BF16)
                erow = jnp.broadcast_to(e1_ref[h, pl.ds(i0 + ii, 1), :][:, hs], (BF16_ROWS, wd)).astype(BF16)
                wh = jnp.where(rb_ref[h, :, :, hs] < lrow[None], e2_ref[h, :, :, hs], zero) * erow[None]
                w = wh if w is None else w + wh
            a = a_ref[half, ii * nk:(ii + 1) * nk, :]
            act_ref[half, ii * nk:(ii + 1) * nk, :] = _gelu_tanh(a).astype(BF16) * w.reshape(nk, wd)
        acc_ref[half] += _dot(vT_ref[...], act_ref[half])

    @pl.when(j == pl.num_programs(1) - 1)
    def _():
        wd = tm // PEER_SPLIT
        for half in range(PEER_SPLIT):
            rs = slice(half * wd, (half + 1) * wd)
            o_ref[rs, :] = x_ref[rs, :] + acc_ref[half].T


def _peer_dense(hnT, u_bf, vT_bf, tabs, x, tm=512, ni=8):
    D, T = hnT.shape
    NE = u_bf.shape[0]
    te = ni * PEER_NKEYS
    tm = min(tm, T)
    tab = pl.BlockSpec((PEER_HEADS, PEER_NKEYS, tm), lambda i, j: (0, 0, i))
    grp = PEER_NKEYS // BF16_ROWS
    tabb = pl.BlockSpec((PEER_HEADS, grp, BF16_ROWS, tm), lambda i, j: (0, 0, 0, i))
    lt, e1, rb, e2 = tabs
    rb, e2 = [t.reshape(PEER_HEADS, grp, BF16_ROWS, T) for t in (rb, e2)]
    return pl.pallas_call(
        functools.partial(_peer_dense_kernel, ni),
        grid=(T // tm, NE // te),
        in_specs=[pl.BlockSpec((D, tm), lambda i, j: (0, i)),
                  pl.BlockSpec((te, D), lambda i, j: (j, 0)),
                  pl.BlockSpec((D, te), lambda i, j: (0, j)),
                  tab, tab, tabb, tabb,
                  pl.BlockSpec((tm, D), lambda i, j: (i, 0))],
        out_specs=pl.BlockSpec((tm, D), lambda i, j: (i, 0)),
        out_shape=jax.ShapeDtypeStruct((T, D), F32),
        scratch_shapes=[pltpu.VMEM((PEER_SPLIT, D, tm // PEER_SPLIT), F32),
                        pltpu.VMEM((PEER_SPLIT, te, tm // PEER_SPLIT), BF16),
                        pltpu.VMEM((PEER_SPLIT, te, tm // PEER_SPLIT), F32)],
        compiler_params=_params("parallel", "arbitrary"),
        name="peer_experts",
    )(hnT, u_bf, vT_bf, lt, e1, rb, e2, x)


def kernel(x, norm1_g, w_in, gate_b, na_q_g, na_k_g, na_rpb, hy_conv_w, hy_conv_b, hy_w1, hy_b1, hy_freq,
           hy_w2, hy_b2, hy_w3, hy_bias, w_up_na, w_up_hy, w_out, norm2_g, peer_wq, peer_subkeys, peer_u,
           peer_v):
    B, S, D = x.shape
    assert B == 2, "the long convolution packs the two batches as one complex sequence"
    depth = w_in.shape[0]
    T = B * S
    xt = x.reshape(T, D)

    fc = _fft_constants(S)
    n1 = fc["n1"]
    pos_ext, dec_ext = _hy_pos_tables(S)
    head_sum = jnp.asarray(np.kron(np.eye(NA_HEADS), np.ones((NA_HEAD_DIM, NA_HEAD_DIM))), BF16)
    o_qk, o_v, o_hy = 2 * NA_WIDTH, 3 * NA_WIDTH, 3 * NA_WIDTH + 3 * HY_WIDTH
    tc = 2048
    ncol = FFT_N2 * HY_WIDTH

    for l in range(depth):
        hn, _ = _rmsnorm(xt, norm1_g[l])
        w = w_in[l].astype(BF16)
        qk_gain = jnp.concatenate([jnp.tile(na_q_g[l], NA_HEADS) * (NA_HEAD_DIM ** -0.5),
                                   jnp.tile(na_k_g[l], NA_HEADS)]).reshape(1, o_qk).astype(F32)
        qk = _mm(hn, w[:, :o_qk], out_dtype=BF16, epi=_epi_head_rmsnorm,
                 extras=[(qk_gain, "col"), (head_sum, "full")], name="proj_qk")
        v = _mm(hn, w[:, o_qk:o_v], out_dtype=BF16, name="proj_v")
        hy = _mm(hn, w[:, o_v:o_hy], out_dtype=F32, name="proj_hy")
        gates = _mm(hn, w[:, o_hy:], out_dtype=BF16, epi=_epi_gate,
                    extras=[(gate_b[l].reshape(1, 2 * D), "col")], name="proj_gates")

        a_out = _na(qk, v, _na_bias_table(na_rpb[l]), B, S)

        z, x0 = _hy_prep(hy, hy_conv_w[l], hy_conv_b[l], S)
        w1p = jnp.pad(hy_w1[l], ((0, HY_FFN_HIDDEN - HY_POS_DIM), (0, 0)))
        g = _hy_filter(S, pos_ext, dec_ext, w1p, hy_b1[l], hy_freq[l], hy_w2[l], hy_b2[l], hy_w3[l])
        gr, gi = _dft_pair(fc["ga"], fc["gb"], g.reshape(n1, ncol), tc)
        gspec = _spec_filter(gr.reshape(n1, FFT_N2, HY_WIDTH), gi.reshape(n1, FFT_N2, HY_WIDTH), fc["wf"])
        z2 = z.reshape(n1, ncol)
        ar, ai = _dft_pair(fc["za"], fc["zb"], z2, tc)
        br, bi = _spec_conv(ar.reshape(n1, FFT_N2, HY_WIDTH), ai.reshape(n1, FFT_N2, HY_WIDTH),
                            fc["wf"], fc["wi"], gspec)
        bias_t = jnp.tile(hy_bias[l], tc // HY_WIDTH).reshape(1, tc)
        b_out = _idft_out(fc["ya"], fc["yb"], br.reshape(n1, ncol), bi.reshape(n1, ncol),
                          z2, x0.reshape(n1, ncol), bias_t, tc).reshape(T, HY_WIDTH)

        merged = _merge(a_out, b_out, w_up_na[l].astype(BF16), w_up_hy[l].astype(BF16), gates)
        xt = _mm(merged, w_out[l].astype(BF16), out_dtype=F32, epi=_epi_residual,
                 extras=[(xt, "tile")], name="proj_out")

        hn2, hn2T = _rmsnorm(xt, norm2_g[l])
        q = _mm(hn2, peer_wq[l].astype(BF16), out_dtype=F32, tn=2 * PEER_HALF, split_cols=True,
                name="peer_query")
        tabs = _peer_topk(q, peer_subkeys[l])
        xt = _peer_dense(hn2T, peer_u[l].astype(BF16), peer_v[l].T.astype(BF16), tabs, xt)
    return xt.reshape(B, S, D)
```

```python
import functools
import math

import numpy as np
import jax
import jax.numpy as jnp
from jax import lax
from jax.experimental import pallas as pl
from jax.experimental.pallas import tpu as pltpu

F32 = jnp.float32
BF16 = jnp.bfloat16

GRID_W = 64
NA_HEADS = 8
NA_HEAD_DIM = 64
NA_WIDTH = NA_HEADS * NA_HEAD_DIM
NA_KR = 8
NA_KW = 16
HY_WIDTH = 512
HY_POS_BANDS = 16
HY_POS_DIM = 1 + 2 * HY_POS_BANDS
HY_FFN_HIDDEN = 64
HY_DECAY_TARGET = 1e-2
HY_FAST_PCT = 0.3
HY_SLOW_PCT = 1.5
PEER_HEADS = 8
PEER_NKEYS = 128
PEER_TOPK = 16
PEER_HALF = 128
EPS = 1e-6
NEG = -1e30

LANES = 128
VMEM_LIMIT_BYTES = 56 * 1024 * 1024
FFT_N2 = 128
BF16_ROWS = 16


def _params(*sem):
    return pltpu.CompilerParams(dimension_semantics=sem, vmem_limit_bytes=VMEM_LIMIT_BYTES)


def _split(x):
    hi = x.astype(BF16)
    lo = (x - hi.astype(F32)).astype(BF16)
    return hi, lo


def _dot(a, b):
    return jnp.dot(a, b, preferred_element_type=F32)


def _dot3(ah, al, bh, bl):
    return _dot(ah, bh) + _dot(ah, bl) + _dot(al, bh)


def _rmsnorm_kernel(x_ref, g_ref, o_ref, *maybe_ot_ref):
    x = x_ref[...]
    y = x * lax.rsqrt(jnp.mean(x * x, axis=-1, keepdims=True) + EPS)
    y = y * g_ref[...]
    o_ref[...] = y.astype(BF16)
    for ot_ref in maybe_ot_ref:
        ot_ref[...] = y.T.astype(BF16)


def _rmsnorm(x, g, transposed=False, tm=512):
    T, D = x.shape
    out_specs = [pl.BlockSpec((tm, D), lambda i: (i, 0))]
    out_shape = [jax.ShapeDtypeStruct((T, D), BF16)]
    if transposed:
        out_specs.append(pl.BlockSpec((D, tm), lambda i: (0, i)))
        out_shape.append(jax.ShapeDtypeStruct((D, T), BF16))
    return pl.pallas_call(
        _rmsnorm_kernel,
        grid=(T // tm,),
        in_specs=[pl.BlockSpec((tm, D), lambda i: (i, 0)),
                  pl.BlockSpec((1, D), lambda i: (0, 0))],
        out_specs=out_specs,
        out_shape=out_shape,
        compiler_params=_params("parallel"),
        name="rmsnorm",
    )(x, g.reshape(1, D))


def _mm_kernel(epi, n_extra, a_ref, w_ref, *rest):
    o_ref = rest[n_extra]
    acc = _dot(a_ref[...], w_ref[...])
    o_ref[...] = epi(acc, *[r[...] for r in rest[:n_extra]]).astype(o_ref.dtype)


def _mm(a, w, *, out_dtype, epi=lambda acc: acc, extras=(), tm=1024, tn=512, split_cols=False,
        cols=None, name="mm"):
    M, K = a.shape
    c0, c1 = cols if cols is not None else (0, w.shape[1])
    N = c1 - c0
    tm, tn = min(tm, M), min(tn, N)
    assert c0 % tn == 0 and N % tn == 0
    j0 = c0 // tn
    if split_cols:
        out_spec = pl.BlockSpec((None, tm, tn), lambda i, j: (j, i, 0))
        out_shape = jax.ShapeDtypeStruct((N // tn, M, tn), out_dtype)
    else:
        out_spec = pl.BlockSpec((tm, tn), lambda i, j: (i, j))
        out_shape = jax.ShapeDtypeStruct((M, N), out_dtype)
    in_specs = [pl.BlockSpec((tm, K), lambda i, j: (i, 0)),
                pl.BlockSpec((K, tn), lambda i, j: (0, j0 + j))]
    args = [a, w]
    for arr, kind in extras:
        if kind == "col":
            in_specs.append(pl.BlockSpec((1, tn), lambda i, j: (0, j)))
        elif kind == "tile":
            in_specs.append(pl.BlockSpec((tm, tn), lambda i, j: (i, j)))
        else:
            in_specs.append(pl.BlockSpec(arr.shape, lambda i, j, nd=arr.ndim: (0,) * nd))
        args.append(arr)
    return pl.pallas_call(
        functools.partial(_mm_kernel, epi, len(extras)),
        grid=(M // tm, N // tn),
        in_specs=in_specs,
        out_specs=out_spec,
        out_shape=out_shape,
        compiler_params=_params("parallel", "parallel"),
        name=name,
    )(*args)


def _epi_head_rmsnorm(acc, gain, head_sum):
    hi, lo = _split(acc * acc)
    ms = (_dot(hi, head_sum) + _dot(lo, head_sum)) * (1.0 / NA_HEAD_DIM)
    return acc * lax.rsqrt(ms + EPS) * gain


def _epi_gate(acc, bias):
    return jax.nn.sigmoid(acc + bias)


def _epi_residual(acc, res):
    return res + acc


def _merge_kernel(a_ref, b_ref, wa_ref, wb_ref, g_ref, o_ref):
    d = o_ref.shape[1]
    ya = _dot(a_ref[...], wa_ref[...])
    yb = _dot(b_ref[...], wb_ref[...])
    g = g_ref[...].astype(F32)
    o_ref[...] = (g[:, :d] * ya + g[:, d:] * yb).astype(o_ref.dtype)


def _merge(a_out, b_out, w_na, w_hy, gates, tm=512):
    T, wa = a_out.shape
    D = w_na.shape[1]
    return pl.pallas_call(
        _merge_kernel,
        grid=(T // tm,),
        in_specs=[pl.BlockSpec((tm, wa), lambda i: (i, 0)),
                  pl.BlockSpec((tm, b_out.shape[1]), lambda i: (i, 0)),
                  pl.BlockSpec(w_na.shape, lambda i: (0, 0)),
                  pl.BlockSpec(w_hy.shape, lambda i: (0, 0)),
                  pl.BlockSpec((tm, 2 * D), lambda i: (i, 0))],
        out_specs=pl.BlockSpec((tm, D), lambda i: (i, 0)),
        out_shape=jax.ShapeDtypeStruct((T, D), BF16),
        compiler_params=_params("parallel"),
        name="merge",
    )(a_out, b_out, w_na, w_hy, gates)


def _na_kernel(rows, rb, q_ref, k_ref, v_ref, bias_ref, o_ref, s_ref, p_ref):
    blk = pl.program_id(2)
    lane = lax.broadcasted_iota(jnp.int32, (GRID_W, LANES), 1)
    nk = NA_KR * GRID_W
    starts = []
    for i in range(rb):
        r = blk * rb + i
        r0 = jnp.clip(r - NA_KR // 2, 0, rows - NA_KR)
        dr0 = r0 - r + (NA_KR - 1)
        start = pl.multiple_of(r0 * GRID_W, GRID_W)
        starts.append(start)
        qr = q_ref[i * GRID_W:(i + 1) * GRID_W, :]
        kw = k_ref[pl.ds(start, nk), :]
        for hh in range(2):
            in_head = (lane >= NA_HEAD_DIM) if hh else (lane < NA_HEAD_DIM)
            qm = jnp.where(in_head, qr, jnp.zeros_like(qr))
            s = lax.dot_general(qm, kw, (((1,), (1,)), ((), ())), preferred_element_type=F32)
            s_ref[(2 * i + hh) * GRID_W:(2 * i + hh + 1) * GRID_W, :] = s + bias_ref[hh, dr0]
    s = s_ref[...]
    p = jnp.exp(s - jnp.max(s, axis=-1, keepdims=True))
    p_ref[...] = (p * (1.0 / jnp.sum(p, axis=-1, keepdims=True))).astype(BF16)
    for i in range(rb):
        vw = v_ref[pl.ds(starts[i], nk), :]
        o0 = _dot(p_ref[(2 * i) * GRID_W:(2 * i + 1) * GRID_W, :], vw)
        o1 = _dot(p_ref[(2 * i + 1) * GRID_W:(2 * i + 2) * GRID_W, :], vw)
        o_ref[i * GRID_W:(i + 1) * GRID_W, :] = jnp.where(lane < NA_HEAD_DIM, o0, o1).astype(o_ref.dtype)


def _na_bias_table(rpb):
    c = np.arange(GRID_W)
    c0 = np.clip(c - NA_KW // 2, 0, GRID_W - NA_KW)
    col_in = (c[None, :] >= c0[:, None]) & (c[None, :] < c0[:, None] + NA_KW)
    dc_idx = np.clip(c[None, :] - c[:, None] + (NA_KW - 1), 0, 2 * NA_KW - 2)
    onehot = jnp.asarray((dc_idx[:, :, None] == np.arange(2 * NA_KW - 1)).astype(np.float32))
    rows = jnp.stack([rpb.astype(F32)[:, d:d + NA_KR] for d in range(NA_KR)], axis=1)
    b = jnp.einsum("hdjc,qkc->hdqjk", rows, onehot, precision=lax.Precision.HIGHEST)
    b = jnp.where(jnp.asarray(col_in)[None, None, :, None, :], b, NEG)
    return b.reshape(NA_HEADS, NA_KR, GRID_W, NA_KR * GRID_W)


def _na(qk, v, bias, B, S, rb=8):
    rows = S // GRID_W
    assert rows >= NA_KR and rows % rb == 0
    nblk = rows // rb
    tq = rb * GRID_W
    kofs = NA_WIDTH // LANES
    return pl.pallas_call(
        functools.partial(_na_kernel, rows, rb),
        grid=(B, NA_HEADS // 2, nblk),
        in_specs=[pl.BlockSpec((tq, LANES), lambda b, p, r: (b * nblk + r, p)),
                  pl.BlockSpec((S, LANES), lambda b, p, r: (b, kofs + p)),
                  pl.BlockSpec((S, LANES), lambda b, p, r: (b, p)),
                  pl.BlockSpec((2, NA_KR, GRID_W, NA_KR * GRID_W), lambda b, p, r: (p, 0, 0, 0))],
        out_specs=pl.BlockSpec((tq, LANES), lambda b, p, r: (b * nblk + r, p)),
        out_shape=jax.ShapeDtypeStruct((B * S, NA_WIDTH), BF16),
        scratch_shapes=[pltpu.VMEM((2 * tq, NA_KR * GRID_W), F32), pltpu.VMEM((2 * tq, NA_KR * GRID_W), BF16)],
        compiler_params=_params("parallel", "parallel", "parallel"),
        name="na_attention",
    )(qk, qk, v, bias)


def _hy_prep_kernel(S, hy_ref, prev_ref, next_ref, w_ref, b_ref, z_ref, x0_ref):
    tm, C = hy_ref.shape
    tok0 = pl.program_id(0) * tm
    h = hy_ref[...]
    row = lax.broadcasted_iota(jnp.int32, (tm, C), 0)
    prev_row = jnp.where(tok0 % S == 0, 0.0, prev_ref[7:8, :])
    next_row = jnp.where((tok0 + tm) % S == 0, 0.0, next_ref[0:1, :])
    up = jnp.where(row == 0, prev_row, pltpu.roll(h, 1, axis=0))
    dn = jnp.where(row == tm - 1, next_row, pltpu.roll(h, tm - 1, axis=0))
    u = up * w_ref[0:1, :] + h * w_ref[1:2, :] + dn * w_ref[2:3, :] + b_ref[...]
    c = C // 3
    x0_ref[...] = u[:, :c]
    z_ref[...] = u[:, 2 * c:] * u[:, c:2 * c]


def _hy_prep(hy, conv_w, conv_b, S, tm=512):
    T, C = hy.shape
    nb = tm // 8
    last = T // 8 - 1
    return pl.pallas_call(
        functools.partial(_hy_prep_kernel, S),
        grid=(T // tm,),
        in_specs=[pl.BlockSpec((tm, C), lambda i: (i, 0)),
                  pl.BlockSpec((8, C), lambda i: (jnp.maximum(i * nb - 1, 0), 0)),
                  pl.BlockSpec((8, C), lambda i: (jnp.minimum((i + 1) * nb, last), 0)),
                  pl.BlockSpec((3, C), lambda i: (0, 0)),
                  pl.BlockSpec((1, C), lambda i: (0, 0))],
        out_specs=[pl.BlockSpec((tm, C // 3), lambda i: (i, 0)),
                   pl.BlockSpec((tm, C // 3), lambda i: (i, 0))],
        out_shape=[jax.ShapeDtypeStruct((T, C // 3), F32), jax.ShapeDtypeStruct((T, C // 3), F32)],
        compiler_params=_params("parallel"),
        name="hyena_prep",
    )(hy, hy, hy, conv_w, conv_b.reshape(1, C))


def _hy_filter_kernel(L, pos_ref, dec_ref, w1_ref, b1_ref, fr_ref, w2_ref, b2_ref, w3_ref, g_ref):
    tr, C = g_ref.shape
    fr = fr_ref[...]
    w1h, w1l = _split(w1_ref[...])
    w2h, w2l = _split(w2_ref[...])
    w3h, w3l = _split(w3_ref[...])
    ph, plo = _split(pos_ref[...])
    h = jnp.sin(fr * (_dot3(ph, plo, w1h, w1l) + _dot(plo, w1l) + b1_ref[...]))
    hh, hl = _split(h)
    h = jnp.sin(fr * (_dot3(hh, hl, w2h, w2l) + _dot(hl, w2l) + b2_ref[...]))
    hh, hl = _split(h)
    h = _dot3(hh, hl, w3h, w3l) + _dot(hl, w3l)
    dec = jnp.exp(-(pos_ref[:, 0:1] * dec_ref[...]))
    hf = h[:, :C] * dec
    hb = h[:, C:] * dec
    n = pl.program_id(0) * tr + lax.broadcasted_iota(jnp.int32, (tr, C), 0)
    g = jnp.where(n < L, hf, hb)
    g = jnp.where(n == L, 0.0, g)
    g_ref[...] = jnp.where(n == 0, hf + hb, g)


def _hy_filter(L, pos_ext, dec_ext, w1p, b1, freq, w2, b2, w3, tr=1024):
    N = 2 * L
    tr = min(tr, N)
    H = HY_FFN_HIDDEN
    full = lambda shape: pl.BlockSpec(shape, lambda i: (0, 0))
    return pl.pallas_call(
        functools.partial(_hy_filter_kernel, L),
        grid=(N // tr,),
        in_specs=[pl.BlockSpec((tr, pos_ext.shape[1]), lambda i: (i, 0)),
                  full((1, HY_WIDTH)),
                  full(w1p.shape), full((1, H)), full((1, H)), full((H, H)), full((1, H)),
                  full((H, 2 * HY_WIDTH))],
        out_specs=pl.BlockSpec((tr, HY_WIDTH), lambda i: (i, 0)),
        out_shape=jax.ShapeDtypeStruct((N, HY_WIDTH), F32),
        compiler_params=_params("parallel"),
        name="hyena_filter",
    )(pos_ext, dec_ext, w1p, b1.reshape(1, H), freq.reshape(1, H), w2, b2.reshape(1, H), w3)


def _dft_pair_kernel(ma_ref, mb_ref, x_ref, or_ref, oi_ref):
    xh, xl = _split(x_ref[...])
    or_ref[...] = _dot3(ma_ref[0], ma_ref[1], xh, xl)
    oi_ref[...] = _dot3(mb_ref[0], mb_ref[1], xh, xl)


def _dft_pair(ma, mb, x, tc=2048):
    R = ma.shape[1]
    K, C = x.shape
    return pl.pallas_call(
        _dft_pair_kernel,
        grid=(C // tc,),
        in_specs=[pl.BlockSpec(ma.shape, lambda j: (0, 0, 0)),
                  pl.BlockSpec(mb.shape, lambda j: (0, 0, 0)),
                  pl.BlockSpec((K, tc), lambda j: (0, j))],
        out_specs=[pl.BlockSpec((R, tc), lambda j: (0, j)), pl.BlockSpec((R, tc), lambda j: (0, j))],
        out_shape=[jax.ShapeDtypeStruct((R, C), F32), jax.ShapeDtypeStruct((R, C), F32)],
        compiler_params=_params("parallel"),
        name="dft_outer",
    )(ma, mb, x)


def _spec_filter_kernel(ar_ref, ai_ref, wf_ref, g_ref):
    ah, al = _split(jnp.concatenate([ar_ref[0], ai_ref[0]], axis=0))
    g_ref[0] = _dot3(wf_ref[0, 0], wf_ref[0, 1], ah, al)


def _spec_filter(ar, ai, wf):
    n1, n2, C = ar.shape
    return pl.pallas_call(
        _spec_filter_kernel,
        grid=(n1,),
        in_specs=[pl.BlockSpec((1, n2, C), lambda k: (k, 0, 0)),
                  pl.BlockSpec((1, n2, C), lambda k: (k, 0, 0)),
                  pl.BlockSpec((1, 2, 2 * n2, 2 * n2), lambda k: (k, 0, 0, 0))],
        out_specs=pl.BlockSpec((1, 2 * n2, C), lambda k: (k, 0, 0)),
        out_shape=jax.ShapeDtypeStruct((n1, 2 * n2, C), F32),
        compiler_params=_params("parallel"),
        name="filter_spectrum",
    )(ar, ai, wf)


def _spec_conv_kernel(ar_ref, ai_ref, wf_ref, wi_ref, g_ref, br_ref, bi_ref):
    n2 = ar_ref.shape[1]
    ah, al = _split(jnp.concatenate([ar_ref[0], ai_ref[0]], axis=0))
    x = _dot3(wf_ref[0, 0], wf_ref[0, 1], ah, al)
    xr, xi = x[:n2], x[n2:]
    gr, gi = g_ref[0, :n2], g_ref[0, n2:]
    y = jnp.concatenate([xr * gr - xi * gi, xr * gi + xi * gr], axis=0)
    yh, yl = _split(y)
    b = _dot3(wi_ref[0, 0], wi_ref[0, 1], yh, yl)
    br_ref[0] = b[:n2]
    bi_ref[0] = b[n2:]


def _spec_conv(ar, ai, wf, wi, gspec):
    n1, n2, C = ar.shape
    blk = pl.BlockSpec((1, n2, C), lambda k: (k, 0, 0))
    wblk = pl.BlockSpec((1, 2, 2 * n2, 2 * n2), lambda k: (k, 0, 0, 0))
    return pl.pallas_call(
        _spec_conv_kernel,
        grid=(n1,),
        in_specs=[blk, blk, wblk, wblk, pl.BlockSpec((1, 2 * n2, C), lambda k: (k, 0, 0))],
        out_specs=[blk, blk],
        out_shape=[jax.ShapeDtypeStruct((n1, n2, C), F32), jax.ShapeDtypeStruct((n1, n2, C), F32)],
        compiler_params=_params("parallel"),
        name="spectrum_conv",
    )(ar, ai, wf, wi, gspec)


def _idft_out_kernel(ma_ref, mb_ref, br_ref, bi_ref, z_ref, x0_ref, bias_ref, o_ref):
    brh, brl = _split(br_ref[...])
    bih, bil = _split(bi_ref[...])
    y = _dot3(ma_ref[0], ma_ref[1], brh, brl) + _dot3(mb_ref[0], mb_ref[1], bih, bil)
    o_ref[...] = ((y + z_ref[...] * bias_ref[...]) * x0_ref[...]).astype(o_ref.dtype)


def _idft_out(ma, mb, br, bi, z2, x02, bias_t, tc=2048):
    R = ma.shape[1]
    K, C = br.shape
    col = lambda rws: pl.BlockSpec((rws, tc), lambda j: (0, j))
    return pl.pallas_call(
        _idft_out_kernel,
        grid=(C // tc,),
        in_specs=[pl.BlockSpec(ma.shape, lambda j: (0, 0, 0)), pl.BlockSpec(mb.shape, lambda j: (0, 0, 0)),
                  col(K), col(K), col(R), col(R), pl.BlockSpec((1, tc), lambda j: (0, 0))],
        out_specs=col(R),
        out_shape=jax.ShapeDtypeStruct((R, C), BF16),
        compiler_params=_params("parallel"),
        name="idft_outer",
    )(ma, mb, br, bi, z2, x02, bias_t)


def _split_const(m):
    m32 = jnp.asarray(np.asarray(m, np.float32))
    hi, lo = _split(m32)
    return jnp.stack([hi, lo])


def _fft_constants(L):
    N = 2 * L
    n2 = FFT_N2
    n1 = N // n2
    h = n1 // 2
    k1 = np.arange(n1)[:, None].astype(np.float64)
    ang = 2.0 * np.pi * k1 * np.arange(n1)[None, :] / n1
    c, s = np.cos(ang), np.sin(ang)
    za = np.concatenate([c[:, :h], s[:, :h]], axis=1)
    zb = np.concatenate([-s[:, :h], c[:, :h]], axis=1)
    ga, gb = c, -s
    ya = np.concatenate([c[:h], s[:h]], axis=0)
    yb = np.concatenate([-s[:h], c[:h]], axis=0)
    kk = np.arange(n2)[:, None].astype(np.float64)
    nn = np.arange(n2)[None, :].astype(np.float64)
    base = 2.0 * np.pi * kk * nn / n2
    tw = 2.0 * np.pi * np.arange(n1)[:, None, None] * nn[None] / N
    f = lambda m: jnp.asarray(np.asarray(m, np.float32))
    cb, sb, ct, st = f(np.cos(base))[None], f(np.sin(base))[None], f(np.cos(tw)), f(np.sin(tw))
    pr = cb * ct - sb * st
    pi = -(sb * ct + cb * st)
    wf = jnp.concatenate([jnp.concatenate([pr, -pi], axis=2), jnp.concatenate([pi, pr], axis=2)], axis=1)
    qr = jnp.transpose(pr, (0, 2, 1)) * (1.0 / N)
    qi = jnp.transpose(-pi, (0, 2, 1)) * (1.0 / N)
    wi = jnp.concatenate([jnp.concatenate([qr, -qi], axis=2), jnp.concatenate([qi, qr], axis=2)], axis=1)
    sp = lambda m: _split_const(m)
    wfs = jnp.stack([*_split(wf)], axis=1)
    wis = jnp.stack([*_split(wi)], axis=1)
    return dict(za=sp(za), zb=sp(zb), ga=sp(ga), gb=sp(gb), ya=sp(ya), yb=sp(yb), wf=wfs, wi=wis, n1=n1)


def _hy_pos_tables(L):
    t = jnp.linspace(0.0, 1.0, L, dtype=F32)[:, None]
    w = 2.0 * math.pi * jnp.arange(L, dtype=F32)[:, None] / L
    f = jnp.linspace(1e-4, HY_POS_BANDS - 1, HY_POS_BANDS, dtype=F32)[None, :]
    z = jnp.concatenate([t, jnp.cos(f * w), -jnp.sin(f * w)], axis=-1)
    max_decay = math.log(HY_DECAY_TARGET) / HY_FAST_PCT
    min_decay = math.log(HY_DECAY_TARGET) / HY_SLOW_PCT
    deltas = jnp.linspace(min_decay, max_decay, HY_WIDTH, dtype=F32)
    zp = jnp.pad(z, ((0, 0), (0, HY_FFN_HIDDEN - HY_POS_DIM)))
    ext = jnp.concatenate([zp, zp[0:1], jnp.flip(zp[1:], axis=0)], axis=0)
    return ext, jnp.abs(deltas)[None, :]


_TRI_ROWS = 16 + 8 * 7 + 8


def _tri_tables(tm):
    flat = [b for b in range(16)]
    for a in range(1, 8):
        flat += [16 * a + b for b in range(8)]
    flat += [16 * a for a in range(8, 16)]
    return jnp.asarray(np.tile(np.asarray(flat, np.float32)[:, None], (1, tm)))


def _topk_rows(scores, k, exact):
    n, tm = scores[0].shape
    rowk = lax.broadcasted_iota(jnp.int32, (k, tm), 0)
    if not exact:
        state = [(s, jnp.zeros((k, tm), F32)) for s in scores]
        for r in range(k):
            nxt = []
            for s, vals in state:
                m = jnp.max(s, axis=0, keepdims=True)
                nxt.append((jnp.where(s == m, -jnp.inf, s), jnp.where(rowk == r, m, vals)))
            state = nxt
        return [(vals, None) for _, vals in state]
    rowf = lax.broadcasted_iota(jnp.int32, (n, tm), 0).astype(F32)
    state = [(s, jnp.zeros((k, tm), F32), jnp.full((n, tm), float(k), F32)) for s in scores]
    for r in range(k):
        nxt = []
        for s, vals, rank in state:
            m = jnp.max(s, axis=0, keepdims=True)
            sel = rowf == jnp.min(jnp.where(s == m, rowf, float(n)), axis=0, keepdims=True)
            nxt.append((jnp.where(sel, -jnp.inf, s), jnp.where(rowk == r, m, vals),
                        jnp.where(sel, float(r), rank)))
        state = nxt
    return [(vals, rank) for _, vals, rank in state]


def _stair_cells(t1, t2, op):
    pieces = [op(t1[0:1], t2)]
    pieces += [op(t1[a:a + 1], t2[0:8]) for a in range(1, 8)]
    pieces += [op(t1[8:16], t2[0:1])]
    return jnp.concatenate(pieces, axis=0)


def _peer_tables(scores, flat, exact):
    K = PEER_TOPK
    tm = scores[0].shape[1]
    (sv1, rank1), (sv2, rank2) = _topk_rows(scores, K, exact)
    cand = _stair_cells(sv1, sv2, lambda x, y: x + y)
    if exact:
        row16 = lax.broadcasted_iota(jnp.int32, (K, tm), 0).astype(F32)
        length = jnp.zeros((K, tm), F32)
        for r in range(K):
            m = jnp.max(cand, axis=0, keepdims=True)
            f = jnp.min(jnp.where(cand == m, flat, 1e9), axis=0, keepdims=True)
            cand = jnp.where(flat == f, -jnp.inf, cand)
            length = jnp.where(row16 == jnp.floor(f * (1.0 / K)), length + 1.0, length)
        picked = cand == -jnp.inf
    else:
        for r in range(K):
            cand = jnp.where(cand == jnp.max(cand, axis=0, keepdims=True), -jnp.inf, cand)
        picked = cand == -jnp.inf
        cnt = jnp.where(picked, 1.0, 0.0)
        rows = [jnp.sum(cnt[0:16], axis=0, keepdims=True)]
        rows += [jnp.sum(cnt[8 + 8 * a:16 + 8 * a], axis=0, keepdims=True) for a in range(1, 8)]
        length = jnp.concatenate(rows + [cnt[72:80]], axis=0)
    s1, s2 = scores
    lfull = jnp.zeros_like(s1)
    if exact:
        ok = jnp.ones((1, tm), jnp.bool_)
        for r in range(K):
            lfull = jnp.where(rank1 == float(r), length[r:r + 1], lfull)
    else:
        rank2 = jnp.zeros_like(s2)
        for r in range(K):
            lfull = jnp.where(s1 == sv1[r:r + 1], length[r:r + 1], lfull)
            rank2 = rank2 + jnp.where(sv2[r:r + 1] > s2, 1.0, 0.0)
        reach = lambda s, sv: jnp.sum(jnp.where(s >= sv[K - 1:K], 1.0, 0.0), axis=0, keepdims=True)
        ok = (reach(s1, sv1) == float(K)) & (reach(s2, sv2) == float(K)) & \
             (jnp.sum(length, axis=0, keepdims=True) == float(K))
    es1 = jnp.exp(sv1 - sv1[0:1])
    es2 = jnp.exp(sv2 - sv2[0:1])
    ecand = _stair_cells(es1, es2, lambda x, y: x * y)
    z = jnp.sum(jnp.where(picked, ecand, 0.0), axis=0, keepdims=True)
    e1 = jnp.exp(scores[0] - sv1[0:1]) / z
    e2 = jnp.exp(scores[1] - sv2[0:1])
    return (lfull, e1, rank2, e2), jnp.where(ok, 1.0, 0.0)


def _peer_topk_kernel(q_ref, sk_ref, flat_ref, l_ref, e1_ref, rb_ref, e2_ref):
    flat = flat_ref[...]

    def head(h, carry):
        q = q_ref[h]
        scores = []
        for p in range(2):
            qh, ql = _split(q[:, p * PEER_HALF:(p + 1) * PEER_HALF])
            kh, kl = _split(sk_ref[h, p])
            nt = (((1,), (1,)), ((), ()))
            dg = lambda a, b: lax.dot_general(a, b, nt, preferred_element_type=F32)
            scores.append(dg(kh, qh) + dg(kh, ql) + dg(kl, qh))

        def write(tabs):
            l_ref[h], e1_ref[h] = tabs[0], tabs[1]
            rb_ref[h], e2_ref[h] = tabs[2].astype(rb_ref.dtype), tabs[3].astype(e2_ref.dtype)

        tabs, ok = _peer_tables(scores, flat, exact=False)
        write(tabs)

        @pl.when(jnp.min(ok) < 0.5)
        def _():
            write(_peer_tables(scores, flat, exact=True)[0])
        return carry

    lax.fori_loop(0, PEER_HEADS, head, 0)


def _peer_topk(q, subkeys, tm=256):
    H, T, _ = q.shape
    tm = min(tm, T)
    out = jax.ShapeDtypeStruct((H, PEER_NKEYS, T), F32)
    outb = jax.ShapeDtypeStruct((H, PEER_NKEYS, T), BF16)
    oblk = pl.BlockSpec((H, PEER_NKEYS, tm), lambda i: (0, 0, i))
    return pl.pallas_call(
        _peer_topk_kernel,
        grid=(T // tm,),
        in_specs=[pl.BlockSpec((H, tm, 2 * PEER_HALF), lambda i: (0, i, 0)),
                  pl.BlockSpec((H, 2, PEER_NKEYS, PEER_HALF), lambda i: (0, 0, 0, 0)),
                  pl.BlockSpec((_TRI_ROWS, tm), lambda i: (0, 0))],
        out_specs=[oblk, oblk, oblk, oblk],
        out_shape=[out, out, outb, outb],
        compiler_params=_params("parallel"),
        name="peer_topk",
    )(q, subkeys, _tri_tables(tm))


def _gelu_tanh(x):
    c = math.sqrt(2.0 / math.pi)
    h = 0.5 * x
    return h + h * jnp.tanh(x * (c + (c * 0.044715) * (x * x)))


def _peer_dense_kernel(ni, hnT_ref, u_ref, vT_ref, l_ref, e1_ref, rb_ref, e2_ref, x_ref, o_ref,
                       acc_ref, act_ref, a_ref):
    j = pl.program_id(1)
    nk = PEER_NKEYS

    @pl.when(j == 0)
    def _():
        acc_ref[...] = jnp.zeros_like(acc_ref)

    tm = hnT_ref.shape[1]
    i0 = pl.multiple_of(j * ni, ni)
    zero = jnp.zeros((), BF16)
    a_ref[...] = _dot(u_ref[...], hnT_ref[...])
    for ii in range(ni):
        w = None
        for h in range(PEER_HEADS):
            lrow = jnp.broadcast_to(l_ref[h, pl.ds(i0 + ii, 1), :], (BF16_ROWS, tm)).astype(BF16)
            erow = jnp.broadcast_to(e1_ref[h, pl.ds(i0 + ii, 1), :], (BF16_ROWS, tm)).astype(BF16)
            wh = jnp.where(rb_ref[h] < lrow[None], e2_ref[h], zero) * erow[None]
            w = wh if w is None else w + wh
        a = a_ref[ii * nk:(ii + 1) * nk, :]
        act_ref[ii * nk:(ii + 1) * nk, :] = _gelu_tanh(a).astype(BF16) * w.reshape(nk, tm)
    acc_ref[...] += _dot(vT_ref[...], act_ref[...])

    @pl.when(j == pl.num_programs(1) - 1)
    def _():
        o_ref[...] = x_ref[...] + acc_ref[...].T


def _peer_dense(hnT, u_bf, vT_bf, tabs, x, tm=512, ni=8):
    D, T = hnT.shape
    NE = u_bf.shape[0]
    te = ni * PEER_NKEYS
    tm = min(tm, T)
    tab = pl.BlockSpec((PEER_HEADS, PEER_NKEYS, tm), lambda i, j: (0, 0, i))
    grp = PEER_NKEYS // BF16_ROWS
    tabb = pl.BlockSpec((PEER_HEADS, grp, BF16_ROWS, tm), lambda i, j: (0, 0, 0, i))
    lt, e1, rb, e2 = tabs
    rb, e2 = [t.reshape(PEER_HEADS, grp, BF16_ROWS, T) for t in (rb, e2)]
    return pl.pallas_call(
        functools.partial(_peer_dense_kernel, ni),
        grid=(T // tm, NE // te),
        in_specs=[pl.BlockSpec((D, tm), lambda i, j: (0, i)),
                  pl.BlockSpec((te, D), lambda i, j: (j, 0)),
                  pl.BlockSpec((D, te), lambda i, j: (0, j)),
                  tab, tab, tabb, tabb,
                  pl.BlockSpec((tm, D), lambda i, j: (i, 0))],
        out_specs=pl.BlockSpec((tm, D), lambda i, j: (i, 0)),
        out_shape=jax.ShapeDtypeStruct((T, D), F32),
        scratch_shapes=[pltpu.VMEM((D, tm), F32), pltpu.VMEM((te, tm), BF16), pltpu.VMEM((te, tm), F32)],
        compiler_params=_params("parallel", "arbitrary"),
        name="peer_experts",
    )(hnT, u_bf, vT_bf, lt, e1, rb, e2, x)


def kernel(x, norm1_g, w_in, gate_b, na_q_g, na_k_g, na_rpb, hy_conv_w, hy_conv_b, hy_w1, hy_b1, hy_freq,
           hy_w2, hy_b2, hy_w3, hy_bias, w_up_na, w_up_hy, w_out, norm2_g, peer_wq, peer_subkeys, peer_u,
           peer_v):
    B, S, D = x.shape
    assert B == 2, "the long convolution packs the two batches as one complex sequence"
    depth = w_in.shape[0]
    T = B * S
    xt = x.reshape(T, D)

    fc = _fft_constants(S)
    n1 = fc["n1"]
    pos_ext, dec_ext = _hy_pos_tables(S)
    head_sum = jnp.asarray(np.kron(np.eye(NA_HEADS), np.ones((NA_HEAD_DIM, NA_HEAD_DIM))), BF16)
    o_qk, o_v, o_hy = 2 * NA_WIDTH, 3 * NA_WIDTH, 3 * NA_WIDTH + 3 * HY_WIDTH
    tc = 2048
    ncol = FFT_N2 * HY_WIDTH

    for l in range(depth):
        (hn,) = _rmsnorm(xt, norm1_g[l])
        w = w_in[l].astype(BF16)
        qk_gain = jnp.concatenate([jnp.tile(na_q_g[l], NA_HEADS) * (NA_HEAD_DIM ** -0.5),
                                   jnp.tile(na_k_g[l], NA_HEADS)]).reshape(1, o_qk).astype(F32)
        qk = _mm(hn, w, cols=(0, o_qk), out_dtype=BF16, epi=_epi_head_rmsnorm,
                 extras=[(qk_gain, "col"), (head_sum, "full")], name="proj_qk")
        v = _mm(hn, w, cols=(o_qk, o_v), out_dtype=BF16, name="proj_v")
        hy = _mm(hn, w, cols=(o_v, o_hy), out_dtype=F32, name="proj_hy")
        gates = _mm(hn, w, cols=(o_hy, w.shape[1]), out_dtype=BF16, epi=_epi_gate,
                    extras=[(gate_b[l].reshape(1, 2 * D), "col")], name="proj_gates")

        a_out = _na(qk, v, _na_bias_table(na_rpb[l]), B, S)

        z, x0 = _hy_prep(hy, hy_conv_w[l], hy_conv_b[l], S)
        w1p = jnp.pad(hy_w1[l], ((0, HY_FFN_HIDDEN - HY_POS_DIM), (0, 0)))
        g = _hy_filter(S, pos_ext, dec_ext, w1p, hy_b1[l], hy_freq[l], hy_w2[l], hy_b2[l], hy_w3[l])
        gr, gi = _dft_pair(fc["ga"], fc["gb"], g.reshape(n1, ncol), tc)
        gspec = _spec_filter(gr.reshape(n1, FFT_N2, HY_WIDTH), gi.reshape(n1, FFT_N2, HY_WIDTH), fc["wf"])
        z2 = z.reshape(n1, ncol)
        ar, ai = _dft_pair(fc["za"], fc["zb"], z2, tc)
        br, bi = _spec_conv(ar.reshape(n1, FFT_N2, HY_WIDTH), ai.reshape(n1, FFT_N2, HY_WIDTH),
                            fc["wf"], fc["wi"], gspec)
        bias_t = jnp.tile(hy_bias[l], tc // HY_WIDTH).reshape(1, tc)
        b_out = _idft_out(fc["ya"], fc["yb"], br.reshape(n1, ncol), bi.reshape(n1, ncol),
                          z2, x0.reshape(n1, ncol), bias_t, tc).reshape(T, HY_WIDTH)

        merged = _merge(a_out, b_out, w_up_na[l].astype(BF16), w_up_hy[l].astype(BF16), gates)
        xt = _mm(merged, w_out[l].astype(BF16), out_dtype=F32, epi=_epi_residual,
                 extras=[(xt, "tile")], name="proj_out")

        hn2, hn2T = _rmsnorm(xt, norm2_g[l], transposed=True)
        q = _mm(hn2, peer_wq[l].astype(BF16), out_dtype=F32, tm=2048, tn=2 * PEER_HALF, split_cols=True,
                name="peer_query")
        tabs = _peer_topk(q, peer_subkeys[l])
        xt = _peer_dense(hn2T, peer_u[l].astype(BF16), peer_v[l].T.astype(BF16), tabs, xt)
    return xt.reshape(B, S, D)
```

```python
import functools
import math

import numpy as np
import jax
import jax.numpy as jnp
from jax import lax
from jax.experimental import pallas as pl
from jax.experimental.pallas import tpu as pltpu

F32 = jnp.float32
BF16 = jnp.bfloat16

GRID_W = 64
NA_HEADS = 8
NA_HEAD_DIM = 64
NA_WIDTH = NA_HEADS * NA_HEAD_DIM
NA_KR = 8
NA_KW = 16
HY_WIDTH = 512
HY_POS_BANDS = 16
HY_POS_DIM = 1 + 2 * HY_POS_BANDS
HY_FFN_HIDDEN = 64
HY_DECAY_TARGET = 1e-2
HY_FAST_PCT = 0.3
HY_SLOW_PCT = 1.5
PEER_HEADS = 8
PEER_NKEYS = 128
PEER_TOPK = 16
PEER_HALF = 128
EPS = 1e-6
NEG = -1e30

LANES = 128
VMEM_LIMIT_BYTES = 56 * 1024 * 1024
FFT_N2 = 128
BF16_ROWS = 16


def _params(*sem):
    return pltpu.CompilerParams(dimension_semantics=sem, vmem_limit_bytes=VMEM_LIMIT_BYTES)


def _split(x):
    hi = x.astype(BF16)
    lo = (x - hi.astype(F32)).astype(BF16)
    return hi, lo


def _dot(a, b):
    return jnp.dot(a, b, preferred_element_type=F32)


def _dot3(ah, al, bh, bl):
    return _dot(ah, bh) + _dot(ah, bl) + _dot(al, bh)


def _rmsnorm_kernel(x_ref, g_ref, o_ref, *maybe_ot_ref):
    x = x_ref[...]
    y = x * lax.rsqrt(jnp.mean(x * x, axis=-1, keepdims=True) + EPS)
    y = y * g_ref[...]
    o_ref[...] = y.astype(BF16)
    for ot_ref in maybe_ot_ref:
        ot_ref[...] = y.T.astype(BF16)


def _rmsnorm(x, g, transposed=False, tm=512):
    T, D = x.shape
    out_specs = [pl.BlockSpec((tm, D), lambda i: (i, 0))]
    out_shape = [jax.ShapeDtypeStruct((T, D), BF16)]
    if transposed:
        out_specs.append(pl.BlockSpec((D, tm), lambda i: (0, i)))
        out_shape.append(jax.ShapeDtypeStruct((D, T), BF16))
    return pl.pallas_call(
        _rmsnorm_kernel,
        grid=(T // tm,),
        in_specs=[pl.BlockSpec((tm, D), lambda i: (i, 0)),
                  pl.BlockSpec((1, D), lambda i: (0, 0))],
        out_specs=out_specs,
        out_shape=out_shape,
        compiler_params=_params("parallel"),
        name="rmsnorm",
    )(x, g.reshape(1, D))


def _mm_kernel(epi, n_extra, a_ref, w_ref, *rest):
    o_ref = rest[n_extra]
    acc = _dot(a_ref[...], w_ref[...])
    o_ref[...] = epi(acc, *[r[...] for r in rest[:n_extra]]).astype(o_ref.dtype)


def _mm(a, w, *, out_dtype, epi=lambda acc: acc, extras=(), tm=1024, tn=512, split_cols=False,
        cols=None, name="mm"):
    M, K = a.shape
    c0, c1 = cols if cols is not None else (0, w.shape[1])
    N = c1 - c0
    tm, tn = min(tm, M), min(tn, N)
    assert c0 % tn == 0 and N % tn == 0
    j0 = c0 // tn
    if split_cols:
        out_spec = pl.BlockSpec((None, tm, tn), lambda i, j: (j, i, 0))
        out_shape = jax.ShapeDtypeStruct((N // tn, M, tn), out_dtype)
    else:
        out_spec = pl.BlockSpec((tm, tn), lambda i, j: (i, j))
        out_shape = jax.ShapeDtypeStruct((M, N), out_dtype)
    in_specs = [pl.BlockSpec((tm, K), lambda i, j: (i, 0)),
                pl.BlockSpec((K, tn), lambda i, j: (0, j0 + j))]
    args = [a, w]
    for arr, kind in extras:
        if kind == "col":
            in_specs.append(pl.BlockSpec((1, tn), lambda i, j: (0, j)))
        elif kind == "tile":
            in_specs.append(pl.BlockSpec((tm, tn), lambda i, j: (i, j)))
        else:
            in_specs.append(pl.BlockSpec(arr.shape, lambda i, j, nd=arr.ndim: (0,) * nd))
        args.append(arr)
    return pl.pallas_call(
        functools.partial(_mm_kernel, epi, len(extras)),
        grid=(M // tm, N // tn),
        in_specs=in_specs,
        out_specs=out_spec,
        out_shape=out_shape,
        compiler_params=_params("parallel", "parallel"),
        name=name,
    )(*args)


def _epi_head_rmsnorm(acc, gain, head_sum):
    hi, lo = _split(acc * acc)
    ms = (_dot(hi, head_sum) + _dot(lo, head_sum)) * (1.0 / NA_HEAD_DIM)
    return acc * lax.rsqrt(ms + EPS) * gain


def _epi_gate(acc, bias):
    return jax.nn.sigmoid(acc + bias)


def _epi_residual(acc, res):
    return res + acc


def _merge_kernel(a_ref, b_ref, wa_ref, wb_ref, g_ref, o_ref):
    d = o_ref.shape[1]
    ya = _dot(a_ref[...], wa_ref[...])
    yb = _dot(b_ref[...], wb_ref[...])
    g = g_ref[...].astype(F32)
    o_ref[...] = (g[:, :d] * ya + g[:, d:] * yb).astype(o_ref.dtype)


def _merge(a_out, b_out, w_na, w_hy, gates, tm=512):
    T, wa = a_out.shape
    D = w_na.shape[1]
    return pl.pallas_call(
        _merge_kernel,
        grid=(T // tm,),
        in_specs=[pl.BlockSpec((tm, wa), lambda i: (i, 0)),
                  pl.BlockSpec((tm, b_out.shape[1]), lambda i: (i, 0)),
                  pl.BlockSpec(w_na.shape, lambda i: (0, 0)),
                  pl.BlockSpec(w_hy.shape, lambda i: (0, 0)),
                  pl.BlockSpec((tm, 2 * D), lambda i: (i, 0))],
        out_specs=pl.BlockSpec((tm, D), lambda i: (i, 0)),
        out_shape=jax.ShapeDtypeStruct((T, D), BF16),
        compiler_params=_params("parallel"),
        name="merge",
    )(a_out, b_out, w_na, w_hy, gates)


def _na_kernel(rows, rb, q_ref, k_ref, v_ref, bias_ref, o_ref, s_ref, p_ref):
    blk = pl.program_id(2)
    lane = lax.broadcasted_iota(jnp.int32, (GRID_W, LANES), 1)
    nk = NA_KR * GRID_W
    starts = []
    for i in range(rb):
        r = blk * rb + i
        r0 = jnp.clip(r - NA_KR // 2, 0, rows - NA_KR)
        dr0 = r0 - r + (NA_KR - 1)
        start = pl.multiple_of(r0 * GRID_W, GRID_W)
        starts.append(start)
        qr = q_ref[i * GRID_W:(i + 1) * GRID_W, :]
        kw = k_ref[pl.ds(start, nk), :]
        for hh in range(2):
            in_head = (lane >= NA_HEAD_DIM) if hh else (lane < NA_HEAD_DIM)
            qm = jnp.where(in_head, qr, jnp.zeros_like(qr))
            s = lax.dot_general(qm, kw, (((1,), (1,)), ((), ())), preferred_element_type=F32)
            s_ref[(2 * i + hh) * GRID_W:(2 * i + hh + 1) * GRID_W, :] = s + bias_ref[hh, dr0]
    s = s_ref[...]
    p = jnp.exp(s - jnp.max(s, axis=-1, keepdims=True))
    p_ref[...] = (p * (1.0 / jnp.sum(p, axis=-1, keepdims=True))).astype(BF16)
    for i in range(rb):
        vw = v_ref[pl.ds(starts[i], nk), :]
        o0 = _dot(p_ref[(2 * i) * GRID_W:(2 * i + 1) * GRID_W, :], vw)
        o1 = _dot(p_ref[(2 * i + 1) * GRID_W:(2 * i + 2) * GRID_W, :], vw)
        o_ref[i * GRID_W:(i + 1) * GRID_W, :] = jnp.where(lane < NA_HEAD_DIM, o0, o1).astype(o_ref.dtype)


def _na_bias_table(rpb):
    c = np.arange(GRID_W)
    c0 = np.clip(c - NA_KW // 2, 0, GRID_W - NA_KW)
    col_in = (c[None, :] >= c0[:, None]) & (c[None, :] < c0[:, None] + NA_KW)
    dc_idx = np.clip(c[None, :] - c[:, None] + (NA_KW - 1), 0, 2 * NA_KW - 2)
    onehot = jnp.asarray((dc_idx[:, :, None] == np.arange(2 * NA_KW - 1)).astype(np.float32))
    rows = jnp.stack([rpb.astype(F32)[:, d:d + NA_KR] for d in range(NA_KR)], axis=1)
    b = jnp.einsum("hdjc,qkc->hdqjk", rows, onehot, precision=lax.Precision.HIGHEST)
    b = jnp.where(jnp.asarray(col_in)[None, None, :, None, :], b, NEG)
    return b.reshape(NA_HEADS, NA_KR, GRID_W, NA_KR * GRID_W)


def _na(qk, v, bias, B, S, rb=8):
    rows = S // GRID_W
    assert rows >= NA_KR and rows % rb == 0
    nblk = rows // rb
    tq = rb * GRID_W
    kofs = NA_WIDTH // LANES
    return pl.pallas_call(
        functools.partial(_na_kernel, rows, rb),
        grid=(B, NA_HEADS // 2, nblk),
        in_specs=[pl.BlockSpec((tq, LANES), lambda b, p, r: (b * nblk + r, p)),
                  pl.BlockSpec((S, LANES), lambda b, p, r: (b, kofs + p)),
                  pl.BlockSpec((S, LANES), lambda b, p, r: (b, p)),
                  pl.BlockSpec((2, NA_KR, GRID_W, NA_KR * GRID_W), lambda b, p, r: (p, 0, 0, 0))],
        out_specs=pl.BlockSpec((tq, LANES), lambda b, p, r: (b * nblk + r, p)),
        out_shape=jax.ShapeDtypeStruct((B * S, NA_WIDTH), BF16),
        scratch_shapes=[pltpu.VMEM((2 * tq, NA_KR * GRID_W), F32), pltpu.VMEM((2 * tq, NA_KR * GRID_W), BF16)],
        compiler_params=_params("parallel", "parallel", "parallel"),
        name="na_attention",
    )(qk, qk, v, bias)


def _hy_prep_kernel(S, hy_ref, prev_ref, next_ref, w_ref, b_ref, z_ref, x0_ref):
    tm, C = hy_ref.shape
    tok0 = pl.program_id(0) * tm
    h = hy_ref[...]
    row = lax.broadcasted_iota(jnp.int32, (tm, C), 0)
    prev_row = jnp.where(tok0 % S == 0, 0.0, prev_ref[7:8, :])
    next_row = jnp.where((tok0 + tm) % S == 0, 0.0, next_ref[0:1, :])
    up = jnp.where(row == 0, prev_row, pltpu.roll(h, 1, axis=0))
    dn = jnp.where(row == tm - 1, next_row, pltpu.roll(h, tm - 1, axis=0))
    u = up * w_ref[0:1, :] + h * w_ref[1:2, :] + dn * w_ref[2:3, :] + b_ref[...]
    c = C // 3
    x0_ref[...] = u[:, :c]
    z_ref[...] = u[:, 2 * c:] * u[:, c:2 * c]


def _hy_prep(hy, conv_w, conv_b, S, tm=512):
    T, C = hy.shape
    nb = tm // 8
    last = T // 8 - 1
    return pl.pallas_call(
        functools.partial(_hy_prep_kernel, S),
        grid=(T // tm,),
        in_specs=[pl.BlockSpec((tm, C), lambda i: (i, 0)),
                  pl.BlockSpec((8, C), lambda i: (jnp.maximum(i * nb - 1, 0), 0)),
                  pl.BlockSpec((8, C), lambda i: (jnp.minimum((i + 1) * nb, last), 0)),
                  pl.BlockSpec((3, C), lambda i: (0, 0)),
                  pl.BlockSpec((1, C), lambda i: (0, 0))],
        out_specs=[pl.BlockSpec((tm, C // 3), lambda i: (i, 0)),
                   pl.BlockSpec((tm, C // 3), lambda i: (i, 0))],
        out_shape=[jax.ShapeDtypeStruct((T, C // 3), F32), jax.ShapeDtypeStruct((T, C // 3), F32)],
        compiler_params=_params("parallel"),
        name="hyena_prep",
    )(hy, hy, hy, conv_w, conv_b.reshape(1, C))


def _hy_filter_kernel(L, pos_ref, dec_ref, w1_ref, b1_ref, fr_ref, w2_ref, b2_ref, w3_ref, g_ref):
    tr, C = g_ref.shape
    fr = fr_ref[...]
    w1h, w1l = _split(w1_ref[...])
    w2h, w2l = _split(w2_ref[...])
    w3h, w3l = _split(w3_ref[...])
    ph, plo = _split(pos_ref[...])
    h = jnp.sin(fr * (_dot3(ph, plo, w1h, w1l) + _dot(plo, w1l) + b1_ref[...]))
    hh, hl = _split(h)
    h = jnp.sin(fr * (_dot3(hh, hl, w2h, w2l) + _dot(hl, w2l) + b2_ref[...]))
    hh, hl = _split(h)
    h = _dot3(hh, hl, w3h, w3l) + _dot(hl, w3l)
    dec = jnp.exp(-(pos_ref[:, 0:1] * dec_ref[...]))
    hf = h[:, :C] * dec
    hb = h[:, C:] * dec
    n = pl.program_id(0) * tr + lax.broadcasted_iota(jnp.int32, (tr, C), 0)
    g = jnp.where(n < L, hf, hb)
    g = jnp.where(n == L, 0.0, g)
    g_ref[...] = jnp.where(n == 0, hf + hb, g)


def _hy_filter(L, pos_ext, dec_ext, w1p, b1, freq, w2, b2, w3, tr=1024):
    N = 2 * L
    tr = min(tr, N)
    H = HY_FFN_HIDDEN
    full = lambda shape: pl.BlockSpec(shape, lambda i: (0, 0))
    return pl.pallas_call(
        functools.partial(_hy_filter_kernel, L),
        grid=(N // tr,),
        in_specs=[pl.BlockSpec((tr, pos_ext.shape[1]), lambda i: (i, 0)),
                  full((1, HY_WIDTH)),
                  full(w1p.shape), full((1, H)), full((1, H)), full((H, H)), full((1, H)),
                  full((H, 2 * HY_WIDTH))],
        out_specs=pl.BlockSpec((tr, HY_WIDTH), lambda i: (i, 0)),
        out_shape=jax.ShapeDtypeStruct((N, HY_WIDTH), F32),
        compiler_params=_params("parallel"),
        name="hyena_filter",
    )(pos_ext, dec_ext, w1p, b1.reshape(1, H), freq.reshape(1, H), w2, b2.reshape(1, H), w3)


def _dft_pair_kernel(ma_ref, mb_ref, x_ref, or_ref, oi_ref):
    xh, xl = _split(x_ref[...])
    or_ref[...] = _dot3(ma_ref[0], ma_ref[1], xh, xl)
    oi_ref[...] = _dot3(mb_ref[0], mb_ref[1], xh, xl)


def _dft_pair(ma, mb, x, tc=2048):
    R = ma.shape[1]
    K, C = x.shape
    return pl.pallas_call(
        _dft_pair_kernel,
        grid=(C // tc,),
        in_specs=[pl.BlockSpec(ma.shape, lambda j: (0, 0, 0)),
                  pl.BlockSpec(mb.shape, lambda j: (0, 0, 0)),
                  pl.BlockSpec((K, tc), lambda j: (0, j))],
        out_specs=[pl.BlockSpec((R, tc), lambda j: (0, j)), pl.BlockSpec((R, tc), lambda j: (0, j))],
        out_shape=[jax.ShapeDtypeStruct((R, C), F32), jax.ShapeDtypeStruct((R, C), F32)],
        compiler_params=_params("parallel"),
        name="dft_outer",
    )(ma, mb, x)


def _spec_conv_kernel(ar_ref, ai_ref, fr_ref, fi_ref, wf_ref, wi_ref, br_ref, bi_ref):
    kb, n2, _ = ar_ref.shape
    for k in range(kb):
        ah, al = _split(jnp.concatenate([ar_ref[k], ai_ref[k]], axis=0))
        x = _dot3(wf_ref[k, 0], wf_ref[k, 1], ah, al)
        fh, fl = _split(jnp.concatenate([fr_ref[k], fi_ref[k]], axis=0))
        g = _dot3(wf_ref[k, 0], wf_ref[k, 1], fh, fl)
        xr, xi, gr, gi = x[:n2], x[n2:], g[:n2], g[n2:]
        yh, yl = _split(jnp.concatenate([xr * gr - xi * gi, xr * gi + xi * gr], axis=0))
        b = _dot3(wi_ref[k, 0], wi_ref[k, 1], yh, yl)
        br_ref[k] = b[:n2]
        bi_ref[k] = b[n2:]


def _spec_conv(ar, ai, fr, fi, wf, wi, kb=4):
    n1, n2, C = ar.shape
    kb = min(kb, n1)
    blk = pl.BlockSpec((kb, n2, C), lambda k: (k, 0, 0))
    wblk = pl.BlockSpec((kb, 2, 2 * n2, 2 * n2), lambda k: (k, 0, 0, 0))
    return pl.pallas_call(
        _spec_conv_kernel,
        grid=(n1 // kb,),
        in_specs=[blk, blk, blk, blk, wblk, wblk],
        out_specs=[blk, blk],
        out_shape=[jax.ShapeDtypeStruct((n1, n2, C), F32), jax.ShapeDtypeStruct((n1, n2, C), F32)],
        compiler_params=_params("parallel"),
        name="spectrum_conv",
    )(ar, ai, fr, fi, wf, wi)


def _idft_out_kernel(ma_ref, mb_ref, br_ref, bi_ref, z_ref, x0_ref, bias_ref, o_ref):
    brh, brl = _split(br_ref[...])
    bih, bil = _split(bi_ref[...])
    y = _dot3(ma_ref[0], ma_ref[1], brh, brl) + _dot3(mb_ref[0], mb_ref[1], bih, bil)
    o_ref[...] = ((y + z_ref[...] * bias_ref[...]) * x0_ref[...]).astype(o_ref.dtype)


def _idft_out(ma, mb, br, bi, z2, x02, bias_t, tc=2048):
    R = ma.shape[1]
    K, C = br.shape
    col = lambda rws: pl.BlockSpec((rws, tc), lambda j: (0, j))
    return pl.pallas_call(
        _idft_out_kernel,
        grid=(C // tc,),
        in_specs=[pl.BlockSpec(ma.shape, lambda j: (0, 0, 0)), pl.BlockSpec(mb.shape, lambda j: (0, 0, 0)),
                  col(K), col(K), col(R), col(R), pl.BlockSpec((1, tc), lambda j: (0, 0))],
        out_specs=col(R),
        out_shape=jax.ShapeDtypeStruct((R, C), BF16),
        compiler_params=_params("parallel"),
        name="idft_outer",
    )(ma, mb, br, bi, z2, x02, bias_t)


def _split_const(m):
    m32 = jnp.asarray(np.asarray(m, np.float32))
    hi, lo = _split(m32)
    return jnp.stack([hi, lo])


def _fft_constants(L):
    N = 2 * L
    n2 = FFT_N2
    n1 = N // n2
    h = n1 // 2
    k1 = np.arange(n1)[:, None].astype(np.float64)
    ang = 2.0 * np.pi * k1 * np.arange(n1)[None, :] / n1
    c, s = np.cos(ang), np.sin(ang)
    za = np.concatenate([c[:, :h], s[:, :h]], axis=1)
    zb = np.concatenate([-s[:, :h], c[:, :h]], axis=1)
    ga, gb = c, -s
    ya = np.concatenate([c[:h], s[:h]], axis=0)
    yb = np.concatenate([-s[:h], c[:h]], axis=0)
    kk = np.arange(n2)[:, None].astype(np.float64)
    nn = np.arange(n2)[None, :].astype(np.float64)
    base = 2.0 * np.pi * kk * nn / n2
    tw = 2.0 * np.pi * np.arange(n1)[:, None, None] * nn[None] / N
    f = lambda m: jnp.asarray(np.asarray(m, np.float32))
    cb, sb, ct, st = f(np.cos(base))[None], f(np.sin(base))[None], f(np.cos(tw)), f(np.sin(tw))
    pr = cb * ct - sb * st
    pi = -(sb * ct + cb * st)
    wf = jnp.concatenate([jnp.concatenate([pr, -pi], axis=2), jnp.concatenate([pi, pr], axis=2)], axis=1)
    qr = jnp.transpose(pr, (0, 2, 1)) * (1.0 / N)
    qi = jnp.transpose(-pi, (0, 2, 1)) * (1.0 / N)
    wi = jnp.concatenate([jnp.concatenate([qr, -qi], axis=2), jnp.concatenate([qi, qr], axis=2)], axis=1)
    sp = lambda m: _split_const(m)
    wfs = jnp.stack([*_split(wf)], axis=1)
    wis = jnp.stack([*_split(wi)], axis=1)
    return dict(za=sp(za), zb=sp(zb), ga=sp(ga), gb=sp(gb), ya=sp(ya), yb=sp(yb), wf=wfs, wi=wis, n1=n1)


def _hy_pos_tables(L):
    t = jnp.linspace(0.0, 1.0, L, dtype=F32)[:, None]
    w = 2.0 * math.pi * jnp.arange(L, dtype=F32)[:, None] / L
    f = jnp.linspace(1e-4, HY_POS_BANDS - 1, HY_POS_BANDS, dtype=F32)[None, :]
    z = jnp.concatenate([t, jnp.cos(f * w), -jnp.sin(f * w)], axis=-1)
    max_decay = math.log(HY_DECAY_TARGET) / HY_FAST_PCT
    min_decay = math.log(HY_DECAY_TARGET) / HY_SLOW_PCT
    deltas = jnp.linspace(min_decay, max_decay, HY_WIDTH, dtype=F32)
    zp = jnp.pad(z, ((0, 0), (0, HY_FFN_HIDDEN - HY_POS_DIM)))
    ext = jnp.concatenate([zp, zp[0:1], jnp.flip(zp[1:], axis=0)], axis=0)
    return ext, jnp.abs(deltas)[None, :]


_TRI_ROWS = 16 + 8 * 7 + 8


def _tri_tables(tm):
    flat = [b for b in range(16)]
    for a in range(1, 8):
        flat += [16 * a + b for b in range(8)]
    flat += [16 * a for a in range(8, 16)]
    return jnp.asarray(np.tile(np.asarray(flat, np.float32)[:, None], (1, tm)))


def _topk_rows(scores, k, exact):
    n, tm = scores[0].shape
    rowk = lax.broadcasted_iota(jnp.int32, (k, tm), 0)
    if not exact:
        state = [(s, jnp.zeros((k, tm), F32)) for s in scores]
        for r in range(k):
            nxt = []
            for s, vals in state:
                m = jnp.max(s, axis=0, keepdims=True)
                nxt.append((jnp.where(s == m, -jnp.inf, s), jnp.where(rowk == r, m, vals)))
            state = nxt
        return [(vals, None) for _, vals in state]
    rowf = lax.broadcasted_iota(jnp.int32, (n, tm), 0).astype(F32)
    state = [(s, jnp.zeros((k, tm), F32), jnp.full((n, tm), float(k), F32)) for s in scores]
    for r in range(k):
        nxt = []
        for s, vals, rank in state:
            m = jnp.max(s, axis=0, keepdims=True)
            sel = rowf == jnp.min(jnp.where(s == m, rowf, float(n)), axis=0, keepdims=True)
            nxt.append((jnp.where(sel, -jnp.inf, s), jnp.where(rowk == r, m, vals),
                        jnp.where(sel, float(r), rank)))
        state = nxt
    return [(vals, rank) for _, vals, rank in state]


def _stair_cells(t1, t2, op):
    pieces = [op(t1[0:1], t2)]
    pieces += [op(t1[a:a + 1], t2[0:8]) for a in range(1, 8)]
    pieces += [op(t1[8:16], t2[0:1])]
    return jnp.concatenate(pieces, axis=0)


def _peer_tables(scores, flat, exact):
    K = PEER_TOPK
    tm = scores[0].shape[1]
    (sv1, rank1), (sv2, rank2) = _topk_rows(scores, K, exact)
    cand = _stair_cells(sv1, sv2, lambda x, y: x + y)
    if exact:
        row16 = lax.broadcasted_iota(jnp.int32, (K, tm), 0).astype(F32)
        length = jnp.zeros((K, tm), F32)
        for r in range(K):
            m = jnp.max(cand, axis=0, keepdims=True)
            f = jnp.min(jnp.where(cand == m, flat, 1e9), axis=0, keepdims=True)
            cand = jnp.where(flat == f, -jnp.inf, cand)
            length = jnp.where(row16 == jnp.floor(f * (1.0 / K)), length + 1.0, length)
        picked = cand == -jnp.inf
    else:
        for r in range(K):
            cand = jnp.where(cand == jnp.max(cand, axis=0, keepdims=True), -jnp.inf, cand)
        picked = cand == -jnp.inf
        cnt = jnp.where(picked, 1.0, 0.0)
        rows = [jnp.sum(cnt[0:16], axis=0, keepdims=True)]
        rows += [jnp.sum(cnt[8 + 8 * a:16 + 8 * a], axis=0, keepdims=True) for a in range(1, 8)]
        length = jnp.concatenate(rows + [cnt[72:80]], axis=0)
    s1, s2 = scores
    lfull = jnp.zeros_like(s1)
    if exact:
        ok = jnp.ones((1, tm), jnp.bool_)
        for r in range(K):
            lfull = jnp.where(rank1 == float(r), length[r:r + 1], lfull)
    else:
        rank2 = jnp.zeros_like(s2)
        for r in range(K):
            lfull = jnp.where(s1 == sv1[r:r + 1], length[r:r + 1], lfull)
            rank2 = rank2 + jnp.where(sv2[r:r + 1] > s2, 1.0, 0.0)
        reach = lambda s, sv: jnp.sum(jnp.where(s >= sv[K - 1:K], 1.0, 0.0), axis=0, keepdims=True)
        ok = (reach(s1, sv1) == float(K)) & (reach(s2, sv2) == float(K)) & \
             (jnp.sum(length, axis=0, keepdims=True) == float(K))
    es1 = jnp.exp(sv1 - sv1[0:1])
    es2 = jnp.exp(sv2 - sv2[0:1])
    ecand = _stair_cells(es1, es2, lambda x, y: x * y)
    z = jnp.sum(jnp.where(picked, ecand, 0.0), axis=0, keepdims=True)
    e1 = jnp.exp(scores[0] - sv1[0:1]) / z
    e2 = jnp.exp(scores[1] - sv2[0:1])
    return (lfull, e1, rank2, e2), jnp.where(ok, 1.0, 0.0)


def _peer_topk_kernel(q_ref, sk_ref, flat_ref, l_ref, e1_ref, rb_ref, e2_ref):
    flat = flat_ref[...]

    def head(h, carry):
        q = q_ref[h]
        scores = []
        for p in range(2):
            qh, ql = _split(q[:, p * PEER_HALF:(p + 1) * PEER_HALF])
            kh, kl = _split(sk_ref[h, p])
            nt = (((1,), (1,)), ((), ()))
            dg = lambda a, b: lax.dot_general(a, b, nt, preferred_element_type=F32)
            scores.append(dg(kh, qh) + dg(kh, ql) + dg(kl, qh))

        def write(tabs, cs):
            l_ref[h, :, cs], e1_ref[h, :, cs] = tabs[0], tabs[1]
            rb_ref[h, :, cs], e2_ref[h, :, cs] = tabs[2].astype(rb_ref.dtype), tabs[3].astype(e2_ref.dtype)

        wd = flat.shape[1]
        for c in range(q.shape[0] // wd):
            cs = slice(c * wd, (c + 1) * wd)
            sc = [s[:, cs] for s in scores]
            tabs, ok = _peer_tables(sc, flat, exact=False)
            write(tabs, cs)

            @pl.when(jnp.min(ok) < 0.5)
            def _():
                write(_peer_tables(sc, flat, exact=True)[0], cs)
        return carry

    lax.fori_loop(0, PEER_HEADS, head, 0)


def _peer_topk(q, subkeys, tm=512, wd=256):
    H, T, _ = q.shape
    tm = min(tm, T)
    wd = min(wd, tm)
    out = jax.ShapeDtypeStruct((H, PEER_NKEYS, T), F32)
    outb = jax.ShapeDtypeStruct((H, PEER_NKEYS, T), BF16)
    oblk = pl.BlockSpec((H, PEER_NKEYS, tm), lambda i: (0, 0, i))
    return pl.pallas_call(
        _peer_topk_kernel,
        grid=(T // tm,),
        in_specs=[pl.BlockSpec((H, tm, 2 * PEER_HALF), lambda i: (0, i, 0)),
                  pl.BlockSpec((H, 2, PEER_NKEYS, PEER_HALF), lambda i: (0, 0, 0, 0)),
                  pl.BlockSpec((_TRI_ROWS, wd), lambda i: (0, 0))],
        out_specs=[oblk, oblk, oblk, oblk],
        out_shape=[out, out, outb, outb],
        compiler_params=_params("parallel"),
        name="peer_topk",
    )(q, subkeys, _tri_tables(wd))


def _gelu_tanh(x):
    c = math.sqrt(2.0 / math.pi)
    h = 0.5 * x
    return h + h * jnp.tanh(x * (c + (c * 0.044715) * (x * x)))


def _peer_dense_kernel(ni, hnT_ref, u_ref, vT_ref, l_ref, e1_ref, rb_ref, e2_ref, x_ref, o_ref,
                       acc_ref, act_ref, a_ref):
    j = pl.program_id(1)
    nk = PEER_NKEYS

    @pl.when(j == 0)
    def _():
        acc_ref[...] = jnp.zeros_like(acc_ref)

    tm = hnT_ref.shape[1]
    i0 = pl.multiple_of(j * ni, ni)
    zero = jnp.zeros((), BF16)
    a_ref[...] = _dot(u_ref[...], hnT_ref[...])
    for ii in range(ni):
        w = None
        for h in range(PEER_HEADS):
            lrow = jnp.broadcast_to(l_ref[h, pl.ds(i0 + ii, 1), :], (BF16_ROWS, tm)).astype(BF16)
            erow = jnp.broadcast_to(e1_ref[h, pl.ds(i0 + ii, 1), :], (BF16_ROWS, tm)).astype(BF16)
            wh = jnp.where(rb_ref[h] < lrow[None], e2_ref[h], zero) * erow[None]
            w = wh if w is None else w + wh
        a = a_ref[ii * nk:(ii + 1) * nk, :]
        act_ref[ii * nk:(ii + 1) * nk, :] = _gelu_tanh(a).astype(BF16) * w.reshape(nk, tm)
    acc_ref[...] += _dot(vT_ref[...], act_ref[...])

    @pl.when(j == pl.num_programs(1) - 1)
    def _():
        o_ref[...] = x_ref[...] + acc_ref[...].T


def _peer_dense(hnT, u_bf, vT_bf, tabs, x, tm=512, ni=16):
    D, T = hnT.shape
    NE = u_bf.shape[0]
    te = ni * PEER_NKEYS
    tm = min(tm, T)
    tab = pl.BlockSpec((PEER_HEADS, PEER_NKEYS, tm), lambda i, j: (0, 0, i))
    grp = PEER_NKEYS // BF16_ROWS
    tabb = pl.BlockSpec((PEER_HEADS, grp, BF16_ROWS, tm), lambda i, j: (0, 0, 0, i))
    lt, e1, rb, e2 = tabs
    rb, e2 = [t.reshape(PEER_HEADS, grp, BF16_ROWS, T) for t in (rb, e2)]
    return pl.pallas_call(
        functools.partial(_peer_dense_kernel, ni),
        grid=(T // tm, NE // te),
        in_specs=[pl.BlockSpec((D, tm), lambda i, j: (0, i)),
                  pl.BlockSpec((te, D), lambda i, j: (j, 0)),
                  pl.BlockSpec((D, te), lambda i, j: (0, j)),
                  tab, tab, tabb, tabb,
                  pl.BlockSpec((tm, D), lambda i, j: (i, 0))],
        out_specs=pl.BlockSpec((tm, D), lambda i, j: (i, 0)),
        out_shape=jax.ShapeDtypeStruct((T, D), F32),
        scratch_shapes=[pltpu.VMEM((D, tm), F32), pltpu.VMEM((te, tm), BF16), pltpu.VMEM((te, tm), F32)],
        compiler_params=_params("parallel", "arbitrary"),
        name="peer_experts",
    )(hnT, u_bf, vT_bf, lt, e1, rb, e2, x)


def kernel(x, norm1_g, w_in, gate_b, na_q_g, na_k_g, na_rpb, hy_conv_w, hy_conv_b, hy_w1, hy_b1, hy_freq,
           hy_w2, hy_b2, hy_w3, hy_bias, w_up_na, w_up_hy, w_out, norm2_g, peer_wq, peer_subkeys, peer_u,
           peer_v):
    B, S, D = x.shape
    assert B == 2, "the long convolution packs the two batches as one complex sequence"
    depth = w_in.shape[0]
    T = B * S
    xt = x.reshape(T, D)

    fc = _fft_constants(S)
    n1 = fc["n1"]
    pos_ext, dec_ext = _hy_pos_tables(S)
    head_sum = jnp.asarray(np.kron(np.eye(NA_HEADS), np.ones((NA_HEAD_DIM, NA_HEAD_DIM))), BF16)
    o_qk, o_v, o_hy = 2 * NA_WIDTH, 3 * NA_WIDTH, 3 * NA_WIDTH + 3 * HY_WIDTH
    tc = 2048
    ncol = FFT_N2 * HY_WIDTH

    for l in range(depth):
        (hn,) = _rmsnorm(xt, norm1_g[l])
        w = w_in[l].astype(BF16)
        qk_gain = jnp.concatenate([jnp.tile(na_q_g[l], NA_HEADS) * (NA_HEAD_DIM ** -0.5),
                                   jnp.tile(na_k_g[l], NA_HEADS)]).reshape(1, o_qk).astype(F32)
        qk = _mm(hn, w, cols=(0, o_qk), out_dtype=BF16, epi=_epi_head_rmsnorm,
                 extras=[(qk_gain, "col"), (head_sum, "full")], name="proj_qk")
        v = _mm(hn, w, cols=(o_qk, o_v), out_dtype=BF16, name="proj_v")
        hy = _mm(hn, w, cols=(o_v, o_hy), out_dtype=F32, name="proj_hy")
        gates = _mm(hn, w, cols=(o_hy, w.shape[1]), out_dtype=BF16, epi=_epi_gate,
                    extras=[(gate_b[l].reshape(1, 2 * D), "col")], name="proj_gates")

        a_out = _na(qk, v, _na_bias_table(na_rpb[l]), B, S)

        z, x0 = _hy_prep(hy, hy_conv_w[l], hy_conv_b[l], S)
        w1p = jnp.pad(hy_w1[l], ((0, HY_FFN_HIDDEN - HY_POS_DIM), (0, 0)))
        g = _hy_filter(S, pos_ext, dec_ext, w1p, hy_b1[l], hy_freq[l], hy_w2[l], hy_b2[l], hy_w3[l])
        gr, gi = _dft_pair(fc["ga"], fc["gb"], g.reshape(n1, ncol), tc)
        z2 = z.reshape(n1, ncol)
        ar, ai = _dft_pair(fc["za"], fc["zb"], z2, tc)
        cube = lambda t: t.reshape(n1, FFT_N2, HY_WIDTH)
        br, bi = _spec_conv(cube(ar), cube(ai), cube(gr), cube(gi), fc["wf"], fc["wi"])
        bias_t = jnp.tile(hy_bias[l], tc // HY_WIDTH).reshape(1, tc)
        b_out = _idft_out(fc["ya"], fc["yb"], br.reshape(n1, ncol), bi.reshape(n1, ncol),
                          z2, x0.reshape(n1, ncol), bias_t, tc).reshape(T, HY_WIDTH)

        merged = _merge(a_out, b_out, w_up_na[l].astype(BF16), w_up_hy[l].astype(BF16), gates)
        xt = _mm(merged, w_out[l].astype(BF16), out_dtype=F32, epi=_epi_residual,
                 extras=[(xt, "tile")], name="proj_out")

        hn2, hn2T = _rmsnorm(xt, norm2_g[l], transposed=True)
        q = _mm(hn2, peer_wq[l].astype(BF16), out_dtype=F32, tm=2048, tn=2 * PEER_HALF, split_cols=True,
                name="peer_query")
        tabs = _peer_topk(q, peer_subkeys[l])
        xt = _peer_dense(hn2T, peer_u[l].astype(BF16), peer_v[l].T.astype(BF16), tabs, xt)
    return xt.reshape(B, S, D)
```

```python
import functools
import math

import numpy as np
import jax
import jax.numpy as jnp
from jax import lax
from jax.experimental import pallas as pl
from jax.experimental.pallas import tpu as pltpu

F32 = jnp.float32
BF16 = jnp.bfloat16

GRID_W = 64
NA_HEADS = 8
NA_HEAD_DIM = 64
NA_WIDTH = NA_HEADS * NA_HEAD_DIM
NA_KR = 8
NA_KW = 16
HY_WIDTH = 512
HY_POS_BANDS = 16
HY_POS_DIM = 1 + 2 * HY_POS_BANDS
HY_FFN_HIDDEN = 64
HY_DECAY_TARGET = 1e-2
HY_FAST_PCT = 0.3
HY_SLOW_PCT = 1.5
PEER_HEADS = 8
PEER_NKEYS = 128
PEER_TOPK = 16
PEER_HALF = 128
EPS = 1e-6
NEG = -1e30

LANES = 128
VMEM_LIMIT_BYTES = 56 * 1024 * 1024
FFT_N2 = 128
BF16_ROWS = 16


def _params(*sem):
    return pltpu.CompilerParams(dimension_semantics=sem, vmem_limit_bytes=VMEM_LIMIT_BYTES)


def _split(x):
    hi = x.astype(BF16)
    lo = (x - hi.astype(F32)).astype(BF16)
    return hi, lo


def _dot(a, b):
    return jnp.dot(a, b, preferred_element_type=F32)


def _dot3(ah, al, bh, bl):
    return _dot(ah, bh) + _dot(ah, bl) + _dot(al, bh)


def _rmsnorm_kernel(x_ref, g_ref, o_ref, *maybe_ot_ref):
    x = x_ref[...]
    y = x * lax.rsqrt(jnp.mean(x * x, axis=-1, keepdims=True) + EPS)
    y = y * g_ref[...]
    o_ref[...] = y.astype(BF16)
    for ot_ref in maybe_ot_ref:
        ot_ref[...] = y.T.astype(BF16)


def _rmsnorm(x, g, transposed=False, tm=512):
    T, D = x.shape
    out_specs = [pl.BlockSpec((tm, D), lambda i: (i, 0))]
    out_shape = [jax.ShapeDtypeStruct((T, D), BF16)]
    if transposed:
        out_specs.append(pl.BlockSpec((D, tm), lambda i: (0, i)))
        out_shape.append(jax.ShapeDtypeStruct((D, T), BF16))
    return pl.pallas_call(
        _rmsnorm_kernel,
        grid=(T // tm,),
        in_specs=[pl.BlockSpec((tm, D), lambda i: (i, 0)),
                  pl.BlockSpec((1, D), lambda i: (0, 0))],
        out_specs=out_specs,
        out_shape=out_shape,
        compiler_params=_params("parallel"),
        name="rmsnorm",
    )(x, g.reshape(1, D))


def _mm_kernel(epi, n_extra, a_ref, w_ref, *rest):
    o_ref = rest[n_extra]
    acc = _dot(a_ref[...], w_ref[...])
    o_ref[...] = epi(acc, *[r[...] for r in rest[:n_extra]]).astype(o_ref.dtype)


def _mm(a, w, *, out_dtype, epi=lambda acc: acc, extras=(), tm=1024, tn=512, split_cols=False,
        cols=None, name="mm"):
    M, K = a.shape
    c0, c1 = cols if cols is not None else (0, w.shape[1])
    N = c1 - c0
    tm, tn = min(tm, M), min(tn, N)
    assert c0 % tn == 0 and N % tn == 0
    j0 = c0 // tn
    if split_cols:
        out_spec = pl.BlockSpec((None, tm, tn), lambda i, j: (j, i, 0))
        out_shape = jax.ShapeDtypeStruct((N // tn, M, tn), out_dtype)
    else:
        out_spec = pl.BlockSpec((tm, tn), lambda i, j: (i, j))
        out_shape = jax.ShapeDtypeStruct((M, N), out_dtype)
    in_specs = [pl.BlockSpec((tm, K), lambda i, j: (i, 0)),
                pl.BlockSpec((K, tn), lambda i, j: (0, j0 + j))]
    args = [a, w]
    for arr, kind in extras:
        if kind == "col":
            in_specs.append(pl.BlockSpec((1, tn), lambda i, j: (0, j)))
        elif kind == "tile":
            in_specs.append(pl.BlockSpec((tm, tn), lambda i, j: (i, j)))
        else:
            in_specs.append(pl.BlockSpec(arr.shape, lambda i, j, nd=arr.ndim: (0,) * nd))
        args.append(arr)
    return pl.pallas_call(
        functools.partial(_mm_kernel, epi, len(extras)),
        grid=(M // tm, N // tn),
        in_specs=in_specs,
        out_specs=out_spec,
        out_shape=out_shape,
        compiler_params=_params("parallel", "parallel"),
        name=name,
    )(*args)


def _epi_head_rmsnorm(acc, gain, head_sum):
    hi, lo = _split(acc * acc)
    ms = (_dot(hi, head_sum) + _dot(lo, head_sum)) * (1.0 / NA_HEAD_DIM)
    return acc * lax.rsqrt(ms + EPS) * gain


def _epi_gate(acc, bias):
    return jax.nn.sigmoid(acc + bias)


def _epi_residual(acc, res):
    return res + acc


def _merge_kernel(a_ref, b_ref, wa_ref, wb_ref, g_ref, o_ref):
    d = o_ref.shape[1]
    ya = _dot(a_ref[...], wa_ref[...])
    yb = _dot(b_ref[...], wb_ref[...])
    g = g_ref[...].astype(F32)
    o_ref[...] = (g[:, :d] * ya + g[:, d:] * yb).astype(o_ref.dtype)


def _merge(a_out, b_out, w_na, w_hy, gates, tm=512):
    T, wa = a_out.shape
    D = w_na.shape[1]
    return pl.pallas_call(
        _merge_kernel,
        grid=(T // tm,),
        in_specs=[pl.BlockSpec((tm, wa), lambda i: (i, 0)),
                  pl.BlockSpec((tm, b_out.shape[1]), lambda i: (i, 0)),
                  pl.BlockSpec(w_na.shape, lambda i: (0, 0)),
                  pl.BlockSpec(w_hy.shape, lambda i: (0, 0)),
                  pl.BlockSpec((tm, 2 * D), lambda i: (i, 0))],
        out_specs=pl.BlockSpec((tm, D), lambda i: (i, 0)),
        out_shape=jax.ShapeDtypeStruct((T, D), BF16),
        compiler_params=_params("parallel"),
        name="merge",
    )(a_out, b_out, w_na, w_hy, gates)


def _na_kernel(rows, rb, q_ref, k_ref, v_ref, bias_ref, o_ref, s_ref, p_ref):
    blk = pl.program_id(2)
    lane = lax.broadcasted_iota(jnp.int32, (GRID_W, LANES), 1)
    nk = NA_KR * GRID_W
    starts = []
    for i in range(rb):
        r = blk * rb + i
        r0 = jnp.clip(r - NA_KR // 2, 0, rows - NA_KR)
        dr0 = r0 - r + (NA_KR - 1)
        start = pl.multiple_of(r0 * GRID_W, GRID_W)
        starts.append(start)
        qr = q_ref[i * GRID_W:(i + 1) * GRID_W, :]
        kw = k_ref[pl.ds(start, nk), :]
        for hh in range(2):
            in_head = (lane >= NA_HEAD_DIM) if hh else (lane < NA_HEAD_DIM)
            qm = jnp.where(in_head, qr, jnp.zeros_like(qr))
            s = lax.dot_general(qm, kw, (((1,), (1,)), ((), ())), preferred_element_type=F32)
            s_ref[(2 * i + hh) * GRID_W:(2 * i + hh + 1) * GRID_W, :] = s + bias_ref[hh, dr0]
    s = s_ref[...]
    p = jnp.exp(s - jnp.max(s, axis=-1, keepdims=True))
    p_ref[...] = (p * (1.0 / jnp.sum(p, axis=-1, keepdims=True))).astype(BF16)
    for i in range(rb):
        vw = v_ref[pl.ds(starts[i], nk), :]
        o0 = _dot(p_ref[(2 * i) * GRID_W:(2 * i + 1) * GRID_W, :], vw)
        o1 = _dot(p_ref[(2 * i + 1) * GRID_W:(2 * i + 2) * GRID_W, :], vw)
        o_ref[i * GRID_W:(i + 1) * GRID_W, :] = jnp.where(lane < NA_HEAD_DIM, o0, o1).astype(o_ref.dtype)


def _na_bias_table(rpb):
    c = np.arange(GRID_W)
    c0 = np.clip(c - NA_KW // 2, 0, GRID_W - NA_KW)
    col_in = (c[None, :] >= c0[:, None]) & (c[None, :] < c0[:, None] + NA_KW)
    dc_idx = np.clip(c[None, :] - c[:, None] + (NA_KW - 1), 0, 2 * NA_KW - 2)
    onehot = jnp.asarray((dc_idx[:, :, None] == np.arange(2 * NA_KW - 1)).astype(np.float32))
    rows = jnp.stack([rpb.astype(F32)[:, d:d + NA_KR] for d in range(NA_KR)], axis=1)
    b = jnp.einsum("hdjc,qkc->hdqjk", rows, onehot, precision=lax.Precision.HIGHEST)
    b = jnp.where(jnp.asarray(col_in)[None, None, :, None, :], b, NEG)
    return b.reshape(NA_HEADS, NA_KR, GRID_W, NA_KR * GRID_W)


def _na(qk, v, bias, B, S, rb=8):
    rows = S // GRID_W
    assert rows >= NA_KR and rows % rb == 0
    nblk = rows // rb
    tq = rb * GRID_W
    kofs = NA_WIDTH // LANES
    return pl.pallas_call(
        functools.partial(_na_kernel, rows, rb),
        grid=(B, NA_HEADS // 2, nblk),
        in_specs=[pl.BlockSpec((tq, LANES), lambda b, p, r: (b * nblk + r, p)),
                  pl.BlockSpec((S, LANES), lambda b, p, r: (b, kofs + p)),
                  pl.BlockSpec((S, LANES), lambda b, p, r: (b, p)),
                  pl.BlockSpec((2, NA_KR, GRID_W, NA_KR * GRID_W), lambda b, p, r: (p, 0, 0, 0))],
        out_specs=pl.BlockSpec((tq, LANES), lambda b, p, r: (b * nblk + r, p)),
        out_shape=jax.ShapeDtypeStruct((B * S, NA_WIDTH), BF16),
        scratch_shapes=[pltpu.VMEM((2 * tq, NA_KR * GRID_W), F32), pltpu.VMEM((2 * tq, NA_KR * GRID_W), BF16)],
        compiler_params=_params("parallel", "parallel", "parallel"),
        name="na_attention",
    )(qk, qk, v, bias)


def _hy_prep_kernel(S, hy_ref, prev_ref, next_ref, w_ref, b_ref, z_ref, x0_ref):
    tm, C = hy_ref.shape
    tok0 = pl.program_id(0) * tm
    h = hy_ref[...]
    row = lax.broadcasted_iota(jnp.int32, (tm, C), 0)
    prev_row = jnp.where(tok0 % S == 0, 0.0, prev_ref[7:8, :])
    next_row = jnp.where((tok0 + tm) % S == 0, 0.0, next_ref[0:1, :])
    up = jnp.where(row == 0, prev_row, pltpu.roll(h, 1, axis=0))
    dn = jnp.where(row == tm - 1, next_row, pltpu.roll(h, tm - 1, axis=0))
    u = up * w_ref[0:1, :] + h * w_ref[1:2, :] + dn * w_ref[2:3, :] + b_ref[...]
    c = C // 3
    x0_ref[...] = u[:, :c]
    z_ref[...] = u[:, 2 * c:] * u[:, c:2 * c]


def _hy_prep(hy, conv_w, conv_b, S, tm=512):
    T, C = hy.shape
    nb = tm // 8
    last = T // 8 - 1
    return pl.pallas_call(
        functools.partial(_hy_prep_kernel, S),
        grid=(T // tm,),
        in_specs=[pl.BlockSpec((tm, C), lambda i: (i, 0)),
                  pl.BlockSpec((8, C), lambda i: (jnp.maximum(i * nb - 1, 0), 0)),
                  pl.BlockSpec((8, C), lambda i: (jnp.minimum((i + 1) * nb, last), 0)),
                  pl.BlockSpec((3, C), lambda i: (0, 0)),
                  pl.BlockSpec((1, C), lambda i: (0, 0))],
        out_specs=[pl.BlockSpec((tm, C // 3), lambda i: (i, 0)),
                   pl.BlockSpec((tm, C // 3), lambda i: (i, 0))],
        out_shape=[jax.ShapeDtypeStruct((T, C // 3), F32), jax.ShapeDtypeStruct((T, C // 3), F32)],
        compiler_params=_params("parallel"),
        name="hyena_prep",
    )(hy, hy, hy, conv_w, conv_b.reshape(1, C))


def _hy_filter_kernel(L, pos_ref, dec_ref, w1_ref, b1_ref, fr_ref, w2_ref, b2_ref, w3_ref, g_ref):
    tr, C = g_ref.shape
    fr = fr_ref[...]
    w1h, w1l = _split(w1_ref[...])
    w2h, w2l = _split(w2_ref[...])
    w3h, w3l = _split(w3_ref[...])
    ph, plo = _split(pos_ref[...])
    h = jnp.sin(fr * (_dot3(ph, plo, w1h, w1l) + _dot(plo, w1l) + b1_ref[...]))
    hh, hl = _split(h)
    h = jnp.sin(fr * (_dot3(hh, hl, w2h, w2l) + _dot(hl, w2l) + b2_ref[...]))
    hh, hl = _split(h)
    h = _dot3(hh, hl, w3h, w3l) + _dot(hl, w3l)
    dec = jnp.exp(-(pos_ref[:, 0:1] * dec_ref[...]))
    hf = h[:, :C] * dec
    hb = h[:, C:] * dec
    n = pl.program_id(0) * tr + lax.broadcasted_iota(jnp.int32, (tr, C), 0)
    g = jnp.where(n < L, hf, hb)
    g = jnp.where(n == L, 0.0, g)
    g_ref[...] = jnp.where(n == 0, hf + hb, g)


def _hy_filter(L, pos_ext, dec_ext, w1p, b1, freq, w2, b2, w3, tr=1024):
    N = 2 * L
    tr = min(tr, N)
    H = HY_FFN_HIDDEN
    full = lambda shape: pl.BlockSpec(shape, lambda i: (0, 0))
    return pl.pallas_call(
        functools.partial(_hy_filter_kernel, L),
        grid=(N // tr,),
        in_specs=[pl.BlockSpec((tr, pos_ext.shape[1]), lambda i: (i, 0)),
                  full((1, HY_WIDTH)),
                  full(w1p.shape), full((1, H)), full((1, H)), full((H, H)), full((1, H)),
                  full((H, 2 * HY_WIDTH))],
        out_specs=pl.BlockSpec((tr, HY_WIDTH), lambda i: (i, 0)),
        out_shape=jax.ShapeDtypeStruct((N, HY_WIDTH), F32),
        compiler_params=_params("parallel"),
        name="hyena_filter",
    )(pos_ext, dec_ext, w1p, b1.reshape(1, H), freq.reshape(1, H), w2, b2.reshape(1, H), w3)


def _dft_pair_kernel(ma_ref, mb_ref, x_ref, or_ref, oi_ref):
    xh, xl = _split(x_ref[...])
    or_ref[...] = _dot3(ma_ref[0], ma_ref[1], xh, xl)
    oi_ref[...] = _dot3(mb_ref[0], mb_ref[1], xh, xl)


def _dft_pair(ma, mb, x, tc=2048):
    R = ma.shape[1]
    K, C = x.shape
    return pl.pallas_call(
        _dft_pair_kernel,
        grid=(C // tc,),
        in_specs=[pl.BlockSpec(ma.shape, lambda j: (0, 0, 0)),
                  pl.BlockSpec(mb.shape, lambda j: (0, 0, 0)),
                  pl.BlockSpec((K, tc), lambda j: (0, j))],
        out_specs=[pl.BlockSpec((R, tc), lambda j: (0, j)), pl.BlockSpec((R, tc), lambda j: (0, j))],
        out_shape=[jax.ShapeDtypeStruct((R, C), F32), jax.ShapeDtypeStruct((R, C), F32)],
        compiler_params=_params("parallel"),
        name="dft_outer",
    )(ma, mb, x)


def _spec_conv_kernel(ar_ref, ai_ref, fr_ref, fi_ref, wf_ref, wi_ref, br_ref, bi_ref):
    kb, n2, _ = ar_ref.shape
    for k in range(kb):
        ah, al = _split(jnp.concatenate([ar_ref[k], ai_ref[k]], axis=0))
        x = _dot3(wf_ref[k, 0], wf_ref[k, 1], ah, al)
        fh, fl = _split(jnp.concatenate([fr_ref[k], fi_ref[k]], axis=0))
        g = _dot3(wf_ref[k, 0], wf_ref[k, 1], fh, fl)
        xr, xi, gr, gi = x[:n2], x[n2:], g[:n2], g[n2:]
        yh, yl = _split(jnp.concatenate([xr * gr - xi * gi, xr * gi + xi * gr], axis=0))
        b = _dot3(wi_ref[k, 0], wi_ref[k, 1], yh, yl)
        br_ref[k] = b[:n2]
        bi_ref[k] = b[n2:]


def _spec_conv(ar, ai, fr, fi, wf, wi, kb=4):
    n1, n2, C = ar.shape
    kb = min(kb, n1)
    blk = pl.BlockSpec((kb, n2, C), lambda k: (k, 0, 0))
    wblk = pl.BlockSpec((kb, 2, 2 * n2, 2 * n2), lambda k: (k, 0, 0, 0))
    return pl.pallas_call(
        _spec_conv_kernel,
        grid=(n1 // kb,),
        in_specs=[blk, blk, blk, blk, wblk, wblk],
        out_specs=[blk, blk],
        out_shape=[jax.ShapeDtypeStruct((n1, n2, C), F32), jax.ShapeDtypeStruct((n1, n2, C), F32)],
        compiler_params=_params("parallel"),
        name="spectrum_conv",
    )(ar, ai, fr, fi, wf, wi)


def _idft_out_kernel(ma_ref, mb_ref, br_ref, bi_ref, z_ref, x0_ref, bias_ref, o_ref):
    brh, brl = _split(br_ref[...])
    bih, bil = _split(bi_ref[...])
    y = _dot3(ma_ref[0], ma_ref[1], brh, brl) + _dot3(mb_ref[0], mb_ref[1], bih, bil)
    o_ref[...] = ((y + z_ref[...] * bias_ref[...]) * x0_ref[...]).astype(o_ref.dtype)


def _idft_out(ma, mb, br, bi, z2, x02, bias_t, tc=2048):
    R = ma.shape[1]
    K, C = br.shape
    col = lambda rws: pl.BlockSpec((rws, tc), lambda j: (0, j))
    return pl.pallas_call(
        _idft_out_kernel,
        grid=(C // tc,),
        in_specs=[pl.BlockSpec(ma.shape, lambda j: (0, 0, 0)), pl.BlockSpec(mb.shape, lambda j: (0, 0, 0)),
                  col(K), col(K), col(R), col(R), pl.BlockSpec((1, tc), lambda j: (0, 0))],
        out_specs=col(R),
        out_shape=jax.ShapeDtypeStruct((R, C), BF16),
        compiler_params=_params("parallel"),
        name="idft_outer",
    )(ma, mb, br, bi, z2, x02, bias_t)


def _split_const(m):
    m32 = jnp.asarray(np.asarray(m, np.float32))
    hi, lo = _split(m32)
    return jnp.stack([hi, lo])


def _fft_constants(L):
    N = 2 * L
    n2 = FFT_N2
    n1 = N // n2
    h = n1 // 2
    k1 = np.arange(n1)[:, None].astype(np.float64)
    ang = 2.0 * np.pi * k1 * np.arange(n1)[None, :] / n1
    c, s = np.cos(ang), np.sin(ang)
    za = np.concatenate([c[:, :h], s[:, :h]], axis=1)
    zb = np.concatenate([-s[:, :h], c[:, :h]], axis=1)
    ga, gb = c, -s
    ya = np.concatenate([c[:h], s[:h]], axis=0)
    yb = np.concatenate([-s[:h], c[:h]], axis=0)
    kk = np.arange(n2)[:, None].astype(np.float64)
    nn = np.arange(n2)[None, :].astype(np.float64)
    base = 2.0 * np.pi * kk * nn / n2
    tw = 2.0 * np.pi * np.arange(n1)[:, None, None] * nn[None] / N
    f = lambda m: jnp.asarray(np.asarray(m, np.float32))
    cb, sb, ct, st = f(np.cos(base))[None], f(np.sin(base))[None], f(np.cos(tw)), f(np.sin(tw))
    pr = cb * ct - sb * st
    pi = -(sb * ct + cb * st)
    wf = jnp.concatenate([jnp.concatenate([pr, -pi], axis=2), jnp.concatenate([pi, pr], axis=2)], axis=1)
    qr = jnp.transpose(pr, (0, 2, 1)) * (1.0 / N)
    qi = jnp.transpose(-pi, (0, 2, 1)) * (1.0 / N)
    wi = jnp.concatenate([jnp.concatenate([qr, -qi], axis=2), jnp.concatenate([qi, qr], axis=2)], axis=1)
    sp = lambda m: _split_const(m)
    wfs = jnp.stack([*_split(wf)], axis=1)
    wis = jnp.stack([*_split(wi)], axis=1)
    return dict(za=sp(za), zb=sp(zb), ga=sp(ga), gb=sp(gb), ya=sp(ya), yb=sp(yb), wf=wfs, wi=wis, n1=n1)


def _hy_pos_tables(L):
    t = jnp.linspace(0.0, 1.0, L, dtype=F32)[:, None]
    w = 2.0 * math.pi * jnp.arange(L, dtype=F32)[:, None] / L
    f = jnp.linspace(1e-4, HY_POS_BANDS - 1, HY_POS_BANDS, dtype=F32)[None, :]
    z = jnp.concatenate([t, jnp.cos(f * w), -jnp.sin(f * w)], axis=-1)
    max_decay = math.log(HY_DECAY_TARGET) / HY_FAST_PCT
    min_decay = math.log(HY_DECAY_TARGET) / HY_SLOW_PCT
    deltas = jnp.linspace(min_decay, max_decay, HY_WIDTH, dtype=F32)
    zp = jnp.pad(z, ((0, 0), (0, HY_FFN_HIDDEN - HY_POS_DIM)))
    ext = jnp.concatenate([zp, zp[0:1], jnp.flip(zp[1:], axis=0)], axis=0)
    return ext, jnp.abs(deltas)[None, :]


_TRI_ROWS = 16 + 8 * 7 + 8


def _tri_tables(tm):
    flat = [b for b in range(16)]
    for a in range(1, 8):
        flat += [16 * a + b for b in range(8)]
    flat += [16 * a for a in range(8, 16)]
    return jnp.asarray(np.tile(np.asarray(flat, np.float32)[:, None], (1, tm)))


def _topk_rows(scores, k, exact):
    n, tm = scores[0].shape
    rowk = lax.broadcasted_iota(jnp.int32, (k, tm), 0)
    if not exact:
        state = [(s, jnp.zeros((k, tm), F32)) for s in scores]
        for r in range(k):
            nxt = []
            for s, vals in state:
                m = jnp.max(s, axis=0, keepdims=True)
                nxt.append((jnp.where(s == m, -jnp.inf, s), jnp.where(rowk == r, m, vals)))
            state = nxt
        return [(vals, None) for _, vals in state]
    rowf = lax.broadcasted_iota(jnp.int32, (n, tm), 0).astype(F32)
    state = [(s, jnp.zeros((k, tm), F32), jnp.full((n, tm), float(k), F32)) for s in scores]
    for r in range(k):
        nxt = []
        for s, vals, rank in state:
            m = jnp.max(s, axis=0, keepdims=True)
            sel = rowf == jnp.min(jnp.where(s == m, rowf, float(n)), axis=0, keepdims=True)
            nxt.append((jnp.where(sel, -jnp.inf, s), jnp.where(rowk == r, m, vals),
                        jnp.where(sel, float(r), rank)))
        state = nxt
    return [(vals, rank) for _, vals, rank in state]


def _stair_cells(t1, t2, op):
    pieces = [op(t1[0:1], t2)]
    pieces += [op(t1[a:a + 1], t2[0:8]) for a in range(1, 8)]
    pieces += [op(t1[8:16], t2[0:1])]
    return jnp.concatenate(pieces, axis=0)


def _peer_tables(scores, flat, exact):
    K = PEER_TOPK
    tm = scores[0].shape[1]
    (sv1, rank1), (sv2, rank2) = _topk_rows(scores, K, exact)
    cand = _stair_cells(sv1, sv2, lambda x, y: x + y)
    if exact:
        row16 = lax.broadcasted_iota(jnp.int32, (K, tm), 0).astype(F32)
        length = jnp.zeros((K, tm), F32)
        for r in range(K):
            m = jnp.max(cand, axis=0, keepdims=True)
            f = jnp.min(jnp.where(cand == m, flat, 1e9), axis=0, keepdims=True)
            cand = jnp.where(flat == f, -jnp.inf, cand)
            length = jnp.where(row16 == jnp.floor(f * (1.0 / K)), length + 1.0, length)
        picked = cand == -jnp.inf
    else:
        for r in range(K):
            cand = jnp.where(cand == jnp.max(cand, axis=0, keepdims=True), -jnp.inf, cand)
        picked = cand == -jnp.inf
        cnt = jnp.where(picked, 1.0, 0.0)
        rows = [jnp.sum(cnt[0:16], axis=0, keepdims=True)]
        rows += [jnp.sum(cnt[8 + 8 * a:16 + 8 * a], axis=0, keepdims=True) for a in range(1, 8)]
        length = jnp.concatenate(rows + [cnt[72:80]], axis=0)
    s1, s2 = scores
    lfull = jnp.zeros_like(s1)
    if exact:
        ok = jnp.ones((1, tm), jnp.bool_)
        for r in range(K):
            lfull = jnp.where(rank1 == float(r), length[r:r + 1], lfull)
    else:
        rank2 = jnp.zeros_like(s2)
        for r in range(K):
            lfull = jnp.where(s1 == sv1[r:r + 1], length[r:r + 1], lfull)
            rank2 = rank2 + jnp.where(sv2[r:r + 1] > s2, 1.0, 0.0)
        reach = lambda s, sv: jnp.sum(jnp.where(s >= sv[K - 1:K], 1.0, 0.0), axis=0, keepdims=True)
        ok = (reach(s1, sv1) == float(K)) & (reach(s2, sv2) == float(K)) & \
             (jnp.sum(length, axis=0, keepdims=True) == float(K))
    es1 = jnp.exp(sv1 - sv1[0:1])
    es2 = jnp.exp(sv2 - sv2[0:1])
    ecand = _stair_cells(es1, es2, lambda x, y: x * y)
    z = jnp.sum(jnp.where(picked, ecand, 0.0), axis=0, keepdims=True)
    e1 = jnp.exp(scores[0] - sv1[0:1]) / z
    e2 = jnp.exp(scores[1] - sv2[0:1])
    return (lfull, e1, rank2, e2), jnp.where(ok, 1.0, 0.0)


def _peer_topk_kernel(q_ref, sk_ref, flat_ref, l_ref, e1_ref, rb_ref, e2_ref):
    flat = flat_ref[...]

    def head(h, carry):
        q = q_ref[h]
        scores = []
        for p in range(2):
            qh, ql = _split(q[:, p * PEER_HALF:(p + 1) * PEER_HALF])
            kh, kl = _split(sk_ref[h, p])
            nt = (((1,), (1,)), ((), ()))
            dg = lambda a, b: lax.dot_general(a, b, nt, preferred_element_type=F32)
            scores.append(dg(kh, qh) + dg(kh, ql) + dg(kl, qh))

        def write(tabs, cs):
            l_ref[h, :, cs], e1_ref[h, :, cs] = tabs[0], tabs[1]
            rb_ref[h, :, cs], e2_ref[h, :, cs] = tabs[2].astype(rb_ref.dtype), tabs[3].astype(e2_ref.dtype)

        wd = flat.shape[1]
        for c in range(q.shape[0] // wd):
            cs = slice(c * wd, (c + 1) * wd)
            sc = [s[:, cs] for s in scores]
            tabs, ok = _peer_tables(sc, flat, exact=False)
            write(tabs, cs)

            @pl.when(jnp.min(ok) < 0.5)
            def _():
                write(_peer_tables(sc, flat, exact=True)[0], cs)
        return carry

    lax.fori_loop(0, PEER_HEADS, head, 0)


def _peer_topk(q, subkeys, tm=512, wd=256):
    H, T, _ = q.shape
    tm = min(tm, T)
    wd = min(wd, tm)
    out = jax.ShapeDtypeStruct((H, PEER_NKEYS, T), F32)
    outb = jax.ShapeDtypeStruct((H, PEER_NKEYS, T), BF16)
    oblk = pl.BlockSpec((H, PEER_NKEYS, tm), lambda i: (0, 0, i))
    return pl.pallas_call(
        _peer_topk_kernel,
        grid=(T // tm,),
        in_specs=[pl.BlockSpec((H, tm, 2 * PEER_HALF), lambda i: (0, i, 0)),
                  pl.BlockSpec((H, 2, PEER_NKEYS, PEER_HALF), lambda i: (0, 0, 0, 0)),
                  pl.BlockSpec((_TRI_ROWS, wd), lambda i: (0, 0))],
        out_specs=[oblk, oblk, oblk, oblk],
        out_shape=[out, out, outb, outb],
        compiler_params=_params("parallel"),
        name="peer_topk",
    )(q, subkeys, _tri_tables(wd))


def _gelu_tanh(x):
    c = math.sqrt(2.0 / math.pi)
    h = 0.5 * x
    return h + h * jnp.tanh(x * (c + (c * 0.044715) * (x * x)))


def _peer_dense_kernel(ni, hnT_ref, u_ref, vT_ref, l_ref, e1_ref, rb_ref, e2_ref, x_ref, o_ref,
                       acc_ref, act_ref, a_ref):
    j = pl.program_id(1)
    nk = PEER_NKEYS

    @pl.when(j == 0)
    def _():
        acc_ref[...] = jnp.zeros_like(acc_ref)

    tm = hnT_ref.shape[1]
    i0 = pl.multiple_of(j * ni, ni)
    zero = jnp.zeros((), BF16)
    a_ref[...] = _dot(u_ref[...], hnT_ref[...])
    for ii in range(ni):
        w = None
        for h in range(PEER_HEADS):
            lrow = jnp.broadcast_to(l_ref[h, pl.ds(i0 + ii, 1), :], (BF16_ROWS, tm)).astype(BF16)
            erow = jnp.broadcast_to(e1_ref[h, pl.ds(i0 + ii, 1), :], (BF16_ROWS, tm)).astype(BF16)
            wh = jnp.where(rb_ref[h] < lrow[None], e2_ref[h], zero) * erow[None]
            w = wh if w is None else w + wh
        a = a_ref[ii * nk:(ii + 1) * nk, :].astype(BF16)
        act_ref[ii * nk:(ii + 1) * nk, :] = _gelu_tanh(a) * w.reshape(nk, tm)
    acc_ref[...] += _dot(vT_ref[...], act_ref[...])

    @pl.when(j == pl.num_programs(1) - 1)
    def _():
        o_ref[...] = x_ref[...] + acc_ref[...].T


def _peer_dense(hnT, u_bf, vT_bf, tabs, x, tm=512, ni=16):
    D, T = hnT.shape
    NE = u_bf.shape[0]
    te = ni * PEER_NKEYS
    tm = min(tm, T)
    tab = pl.BlockSpec((PEER_HEADS, PEER_NKEYS, tm), lambda i, j: (0, 0, i))
    grp = PEER_NKEYS // BF16_ROWS
    tabb = pl.BlockSpec((PEER_HEADS, grp, BF16_ROWS, tm), lambda i, j: (0, 0, 0, i))
    lt, e1, rb, e2 = tabs
    rb, e2 = [t.reshape(PEER_HEADS, grp, BF16_ROWS, T) for t in (rb, e2)]
    return pl.pallas_call(
        functools.partial(_peer_dense_kernel, ni),
        grid=(T // tm, NE // te),
        in_specs=[pl.BlockSpec((D, tm), lambda i, j: (0, i)),
                  pl.BlockSpec((te, D), lambda i, j: (j, 0)),
                  pl.BlockSpec((D, te), lambda i, j: (0, j)),
                  tab, tab, tabb, tabb,
                  pl.BlockSpec((tm, D), lambda i, j: (i, 0))],
        out_specs=pl.BlockSpec((tm, D), lambda i, j: (i, 0)),
        out_shape=jax.ShapeDtypeStruct((T, D), F32),
        scratch_shapes=[pltpu.VMEM((D, tm), F32), pltpu.VMEM((te, tm), BF16), pltpu.VMEM((te, tm), F32)],
        compiler_params=_params("parallel", "arbitrary"),
        name="peer_experts",
    )(hnT, u_bf, vT_bf, lt, e1, rb, e2, x)


def kernel(x, norm1_g, w_in, gate_b, na_q_g, na_k_g, na_rpb, hy_conv_w, hy_conv_b, hy_w1, hy_b1, hy_freq,
           hy_w2, hy_b2, hy_w3, hy_bias, w_up_na, w_up_hy, w_out, norm2_g, peer_wq, peer_subkeys, peer_u,
           peer_v):
    B, S, D = x.shape
    assert B == 2, "the long convolution packs the two batches as one complex sequence"
    depth = w_in.shape[0]
    T = B * S
    xt = x.reshape(T, D)

    fc = _fft_constants(S)
    n1 = fc["n1"]
    pos_ext, dec_ext = _hy_pos_tables(S)
    head_sum = jnp.asarray(np.kron(np.eye(NA_HEADS), np.ones((NA_HEAD_DIM, NA_HEAD_DIM))), BF16)
    o_qk, o_v, o_hy = 2 * NA_WIDTH, 3 * NA_WIDTH, 3 * NA_WIDTH + 3 * HY_WIDTH
    tc = 2048
    ncol = FFT_N2 * HY_WIDTH

    for l in range(depth):
        (hn,) = _rmsnorm(xt, norm1_g[l])
        w = w_in[l].astype(BF16)
        qk_gain = jnp.concatenate([jnp.tile(na_q_g[l], NA_HEADS) * (NA_HEAD_DIM ** -0.5),
                                   jnp.tile(na_k_g[l], NA_HEADS)]).reshape(1, o_qk).astype(F32)
        qk = _mm(hn, w, cols=(0, o_qk), out_dtype=BF16, epi=_epi_head_rmsnorm,
                 extras=[(qk_gain, "col"), (head_sum, "full")], name="proj_qk")
        v = _mm(hn, w, cols=(o_qk, o_v), out_dtype=BF16, name="proj_v")
        hy = _mm(hn, w, cols=(o_v, o_hy), out_dtype=F32, name="proj_hy")
        gates = _mm(hn, w, cols=(o_hy, w.shape[1]), out_dtype=BF16, epi=_epi_gate,
                    extras=[(gate_b[l].reshape(1, 2 * D), "col")], name="proj_gates")

        a_out = _na(qk, v, _na_bias_table(na_rpb[l]), B, S)

        z, x0 = _hy_prep(hy, hy_conv_w[l], hy_conv_b[l], S)
        w1p = jnp.pad(hy_w1[l], ((0, HY_FFN_HIDDEN - HY_POS_DIM), (0, 0)))
        g = _hy_filter(S, pos_ext, dec_ext, w1p, hy_b1[l], hy_freq[l], hy_w2[l], hy_b2[l], hy_w3[l])
        gr, gi = _dft_pair(fc["ga"], fc["gb"], g.reshape(n1, ncol), tc)
        z2 = z.reshape(n1, ncol)
        ar, ai = _dft_pair(fc["za"], fc["zb"], z2, tc)
        cube = lambda t: t.reshape(n1, FFT_N2, HY_WIDTH)
        br, bi = _spec_conv(cube(ar), cube(ai), cube(gr), cube(gi), fc["wf"], fc["wi"])
        bias_t = jnp.tile(hy_bias[l], tc // HY_WIDTH).reshape(1, tc)
        b_out = _idft_out(fc["ya"], fc["yb"], br.reshape(n1, ncol), bi.reshape(n1, ncol),
                          z2, x0.reshape(n1, ncol), bias_t, tc).reshape(T, HY_WIDTH)

        merged = _merge(a_out, b_out, w_up_na[l].astype(BF16), w_up_hy[l].astype(BF16), gates)
        xt = _mm(merged, w_out[l].astype(BF16), out_dtype=F32, epi=_epi_residual,
                 extras=[(xt, "tile")], name="proj_out")

        hn2, hn2T = _rmsnorm(xt, norm2_g[l], transposed=True)
        q = _mm(hn2, peer_wq[l].astype(BF16), out_dtype=F32, tm=2048, tn=2 * PEER_HALF, split_cols=True,
                name="peer_query")
        tabs = _peer_topk(q, peer_subkeys[l])
        xt = _peer_dense(hn2T, peer_u[l].astype(BF16), peer_v[l].T.astype(BF16), tabs, xt)
    return xt.reshape(B, S, D)
```

```python
import functools
import math

import numpy as np
import jax
import jax.numpy as jnp
from jax import lax
from jax.experimental import pallas as pl
from jax.experimental.pallas import tpu as pltpu

F32 = jnp.float32
BF16 = jnp.bfloat16

GRID_W = 64
NA_HEADS = 8
NA_HEAD_DIM = 64
NA_WIDTH = NA_HEADS * NA_HEAD_DIM
NA_KR = 8
NA_KW = 16
HY_WIDTH = 512
HY_POS_BANDS = 16
HY_POS_DIM = 1 + 2 * HY_POS_BANDS
HY_FFN_HIDDEN = 64
HY_DECAY_TARGET = 1e-2
HY_FAST_PCT = 0.3
HY_SLOW_PCT = 1.5
PEER_HEADS = 8
PEER_NKEYS = 128
PEER_TOPK = 16
PEER_HALF = 128
EPS = 1e-6
NEG = -1e30

LANES = 128
VMEM_LIMIT_BYTES = 56 * 1024 * 1024
FFT_N2 = 128
BF16_ROWS = 16


def _params(*sem):
    return pltpu.CompilerParams(dimension_semantics=sem, vmem_limit_bytes=VMEM_LIMIT_BYTES)


def _split(x):
    hi = x.astype(BF16)
    lo = (x - hi.astype(F32)).astype(BF16)
    return hi, lo


def _dot(a, b):
    return jnp.dot(a, b, preferred_element_type=F32)


def _dot3(ah, al, bh, bl):
    return _dot(ah, bh) + _dot(ah, bl) + _dot(al, bh)


def _rmsnorm_kernel(x_ref, g_ref, o_ref, *maybe_ot_ref):
    x = x_ref[...]
    y = x * lax.rsqrt(jnp.mean(x * x, axis=-1, keepdims=True) + EPS)
    y = y * g_ref[...]
    o_ref[...] = y.astype(BF16)
    for ot_ref in maybe_ot_ref:
        ot_ref[...] = y.T.astype(BF16)


def _rmsnorm(x, g, transposed=False, tm=512):
    T, D = x.shape
    out_specs = [pl.BlockSpec((tm, D), lambda i: (i, 0))]
    out_shape = [jax.ShapeDtypeStruct((T, D), BF16)]
    if transposed:
        out_specs.append(pl.BlockSpec((D, tm), lambda i: (0, i)))
        out_shape.append(jax.ShapeDtypeStruct((D, T), BF16))
    return pl.pallas_call(
        _rmsnorm_kernel,
        grid=(T // tm,),
        in_specs=[pl.BlockSpec((tm, D), lambda i: (i, 0)),
                  pl.BlockSpec((1, D), lambda i: (0, 0))],
        out_specs=out_specs,
        out_shape=out_shape,
        compiler_params=_params("parallel"),
        name="rmsnorm",
    )(x, g.reshape(1, D))


def _mm_kernel(epi, n_extra, a_ref, w_ref, *rest):
    o_ref = rest[n_extra]
    acc = _dot(a_ref[...], w_ref[...])
    o_ref[...] = epi(acc, *[r[...] for r in rest[:n_extra]]).astype(o_ref.dtype)


def _mm(a, w, *, out_dtype, epi=lambda acc: acc, extras=(), tm=1024, tn=512, cols=None, name="mm"):
    M, K = a.shape
    c0, c1 = cols if cols is not None else (0, w.shape[1])
    N = c1 - c0
    tm, tn = min(tm, M), min(tn, N)
    assert c0 % tn == 0 and N % tn == 0
    j0 = c0 // tn
    out_spec = pl.BlockSpec((tm, tn), lambda i, j: (i, j))
    out_shape = jax.ShapeDtypeStruct((M, N), out_dtype)
    in_specs = [pl.BlockSpec((tm, K), lambda i, j: (i, 0)),
                pl.BlockSpec((K, tn), lambda i, j: (0, j0 + j))]
    args = [a, w]
    for arr, kind in extras:
        if kind == "col":
            in_specs.append(pl.BlockSpec((1, tn), lambda i, j: (0, j)))
        elif kind == "tile":
            in_specs.append(pl.BlockSpec((tm, tn), lambda i, j: (i, j)))
        else:
            in_specs.append(pl.BlockSpec(arr.shape, lambda i, j, nd=arr.ndim: (0,) * nd))
        args.append(arr)
    return pl.pallas_call(
        functools.partial(_mm_kernel, epi, len(extras)),
        grid=(M // tm, N // tn),
        in_specs=in_specs,
        out_specs=out_spec,
        out_shape=out_shape,
        compiler_params=_params("parallel", "parallel"),
        name=name,
    )(*args)


def _epi_head_rmsnorm(acc, gain, head_sum):
    hi, lo = _split(acc * acc)
    ms = (_dot(hi, head_sum) + _dot(lo, head_sum)) * (1.0 / NA_HEAD_DIM)
    return acc * lax.rsqrt(ms + EPS) * gain


def _epi_gate(acc, bias):
    return jax.nn.sigmoid(acc + bias)


def _epi_residual(acc, res):
    return res + acc


def _merge_kernel(a_ref, b_ref, wa_ref, wb_ref, g_ref, o_ref):
    d = o_ref.shape[1]
    ya = _dot(a_ref[...], wa_ref[...])
    yb = _dot(b_ref[...], wb_ref[...])
    g = g_ref[...].astype(F32)
    o_ref[...] = (g[:, :d] * ya + g[:, d:] * yb).astype(o_ref.dtype)


def _merge(a_out, b_out, w_na, w_hy, gates, tm=512):
    T, wa = a_out.shape
    D = w_na.shape[1]
    return pl.pallas_call(
        _merge_kernel,
        grid=(T // tm,),
        in_specs=[pl.BlockSpec((tm, wa), lambda i: (i, 0)),
                  pl.BlockSpec((tm, b_out.shape[1]), lambda i: (i, 0)),
                  pl.BlockSpec(w_na.shape, lambda i: (0, 0)),
                  pl.BlockSpec(w_hy.shape, lambda i: (0, 0)),
                  pl.BlockSpec((tm, 2 * D), lambda i: (i, 0))],
        out_specs=pl.BlockSpec((tm, D), lambda i: (i, 0)),
        out_shape=jax.ShapeDtypeStruct((T, D), BF16),
        compiler_params=_params("parallel"),
        name="merge",
    )(a_out, b_out, w_na, w_hy, gates)


def _na_kernel(rows, rb, q_ref, k_ref, v_ref, bias_ref, o_ref, s_ref, p_ref):
    blk = pl.program_id(2)
    lane = lax.broadcasted_iota(jnp.int32, (GRID_W, LANES), 1)
    nk = NA_KR * GRID_W
    starts = []
    for i in range(rb):
        r = blk * rb + i
        r0 = jnp.clip(r - NA_KR // 2, 0, rows - NA_KR)
        dr0 = r0 - r + (NA_KR - 1)
        start = pl.multiple_of(r0 * GRID_W, GRID_W)
        starts.append(start)
        qr = q_ref[i * GRID_W:(i + 1) * GRID_W, :]
        kw = k_ref[pl.ds(start, nk), :]
        for hh in range(2):
            in_head = (lane >= NA_HEAD_DIM) if hh else (lane < NA_HEAD_DIM)
            qm = jnp.where(in_head, qr, jnp.zeros_like(qr))
            s = lax.dot_general(qm, kw, (((1,), (1,)), ((), ())), preferred_element_type=F32)
            s_ref[(2 * i + hh) * GRID_W:(2 * i + hh + 1) * GRID_W, :] = s + bias_ref[hh, dr0]
    s = s_ref[...]
    p = jnp.exp(s - jnp.max(s, axis=-1, keepdims=True))
    p_ref[...] = (p * (1.0 / jnp.sum(p, axis=-1, keepdims=True))).astype(BF16)
    for i in range(rb):
        vw = v_ref[pl.ds(starts[i], nk), :]
        o0 = _dot(p_ref[(2 * i) * GRID_W:(2 * i + 1) * GRID_W, :], vw)
        o1 = _dot(p_ref[(2 * i + 1) * GRID_W:(2 * i + 2) * GRID_W, :], vw)
        o_ref[i * GRID_W:(i + 1) * GRID_W, :] = jnp.where(lane < NA_HEAD_DIM, o0, o1).astype(o_ref.dtype)


def _na_bias_table(rpb):
    c = np.arange(GRID_W)
    c0 = np.clip(c - NA_KW // 2, 0, GRID_W - NA_KW)
    col_in = (c[None, :] >= c0[:, None]) & (c[None, :] < c0[:, None] + NA_KW)
    dc_idx = np.clip(c[None, :] - c[:, None] + (NA_KW - 1), 0, 2 * NA_KW - 2)
    onehot = jnp.asarray((dc_idx[:, :, None] == np.arange(2 * NA_KW - 1)).astype(np.float32))
    rows = jnp.stack([rpb.astype(F32)[:, d:d + NA_KR] for d in range(NA_KR)], axis=1)
    b = jnp.einsum("hdjc,qkc->hdqjk", rows, onehot, precision=lax.Precision.HIGHEST)
    b = jnp.where(jnp.asarray(col_in)[None, None, :, None, :], b, NEG)
    return b.reshape(NA_HEADS, NA_KR, GRID_W, NA_KR * GRID_W)


def _na(qk, v, bias, B, S, rb=16):
    rows = S // GRID_W
    assert rows >= NA_KR and rows % rb == 0
    nblk = rows // rb
    tq = rb * GRID_W
    kofs = NA_WIDTH // LANES
    return pl.pallas_call(
        functools.partial(_na_kernel, rows, rb),
        grid=(B, NA_HEADS // 2, nblk),
        in_specs=[pl.BlockSpec((tq, LANES), lambda b, p, r: (b * nblk + r, p)),
                  pl.BlockSpec((S, LANES), lambda b, p, r: (b, kofs + p)),
                  pl.BlockSpec((S, LANES), lambda b, p, r: (b, p)),
                  pl.BlockSpec((2, NA_KR, GRID_W, NA_KR * GRID_W), lambda b, p, r: (p, 0, 0, 0))],
        out_specs=pl.BlockSpec((tq, LANES), lambda b, p, r: (b * nblk + r, p)),
        out_shape=jax.ShapeDtypeStruct((B * S, NA_WIDTH), BF16),
        scratch_shapes=[pltpu.VMEM((2 * tq, NA_KR * GRID_W), F32), pltpu.VMEM((2 * tq, NA_KR * GRID_W), BF16)],
        compiler_params=_params("parallel", "parallel", "parallel"),
        name="na_attention",
    )(qk, qk, v, bias)


def _hy_prep_kernel(S, hy_ref, prev_ref, next_ref, w_ref, b_ref, z_ref, x0_ref):
    tm, C = hy_ref.shape
    tok0 = pl.program_id(0) * tm
    h = hy_ref[...]
    row = lax.broadcasted_iota(jnp.int32, (tm, C), 0)
    prev_row = jnp.where(tok0 % S == 0, 0.0, prev_ref[7:8, :])
    next_row = jnp.where((tok0 + tm) % S == 0, 0.0, next_ref[0:1, :])
    up = jnp.where(row == 0, prev_row, pltpu.roll(h, 1, axis=0))
    dn = jnp.where(row == tm - 1, next_row, pltpu.roll(h, tm - 1, axis=0))
    u = up * w_ref[0:1, :] + h * w_ref[1:2, :] + dn * w_ref[2:3, :] + b_ref[...]
    c = C // 3
    x0_ref[...] = u[:, :c]
    z_ref[...] = u[:, 2 * c:] * u[:, c:2 * c]


def _hy_prep(hy, conv_w, conv_b, S, tm=512):
    T, C = hy.shape
    nb = tm // 8
    last = T // 8 - 1
    return pl.pallas_call(
        functools.partial(_hy_prep_kernel, S),
        grid=(T // tm,),
        in_specs=[pl.BlockSpec((tm, C), lambda i: (i, 0)),
                  pl.BlockSpec((8, C), lambda i: (jnp.maximum(i * nb - 1, 0), 0)),
                  pl.BlockSpec((8, C), lambda i: (jnp.minimum((i + 1) * nb, last), 0)),
                  pl.BlockSpec((3, C), lambda i: (0, 0)),
                  pl.BlockSpec((1, C), lambda i: (0, 0))],
        out_specs=[pl.BlockSpec((tm, C // 3), lambda i: (i, 0)),
                   pl.BlockSpec((tm, C // 3), lambda i: (i, 0))],
        out_shape=[jax.ShapeDtypeStruct((T, C // 3), F32), jax.ShapeDtypeStruct((T, C // 3), F32)],
        compiler_params=_params("parallel"),
        name="hyena_prep",
    )(hy, hy, hy, conv_w, conv_b.reshape(1, C))


def _hy_filter_kernel(L, pos_ref, dec_ref, w1_ref, b1_ref, fr_ref, w2_ref, b2_ref, w3_ref, g_ref):
    tr, C = g_ref.shape
    fr = fr_ref[...]
    w1h, w1l = _split(w1_ref[...])
    w2h, w2l = _split(w2_ref[...])
    w3h, w3l = _split(w3_ref[...])
    ph, plo = _split(pos_ref[...])
    h = jnp.sin(fr * (_dot3(ph, plo, w1h, w1l) + _dot(plo, w1l) + b1_ref[...]))
    hh, hl = _split(h)
    h = jnp.sin(fr * (_dot3(hh, hl, w2h, w2l) + _dot(hl, w2l) + b2_ref[...]))
    hh, hl = _split(h)
    h = _dot3(hh, hl, w3h, w3l) + _dot(hl, w3l)
    dec = jnp.exp(-(pos_ref[:, 0:1] * dec_ref[...]))
    hf = h[:, :C] * dec
    hb = h[:, C:] * dec
    n = pl.program_id(0) * tr + lax.broadcasted_iota(jnp.int32, (tr, C), 0)
    g = jnp.where(n < L, hf, hb)
    g = jnp.where(n == L, 0.0, g)
    g_ref[...] = jnp.where(n == 0, hf + hb, g)


def _hy_filter(L, pos_ext, dec_ext, w1p, b1, freq, w2, b2, w3, tr=1024):
    N = 2 * L
    tr = min(tr, N)
    H = HY_FFN_HIDDEN
    full = lambda shape: pl.BlockSpec(shape, lambda i: (0, 0))
    return pl.pallas_call(
        functools.partial(_hy_filter_kernel, L),
        grid=(N // tr,),
        in_specs=[pl.BlockSpec((tr, pos_ext.shape[1]), lambda i: (i, 0)),
                  full((1, HY_WIDTH)),
                  full(w1p.shape), full((1, H)), full((1, H)), full((H, H)), full((1, H)),
                  full((H, 2 * HY_WIDTH))],
        out_specs=pl.BlockSpec((tr, HY_WIDTH), lambda i: (i, 0)),
        out_shape=jax.ShapeDtypeStruct((N, HY_WIDTH), F32),
        compiler_params=_params("parallel"),
        name="hyena_filter",
    )(pos_ext, dec_ext, w1p, b1.reshape(1, H), freq.reshape(1, H), w2, b2.reshape(1, H), w3)


def _dft_pair_kernel(ma_ref, mb_ref, x_ref, or_ref, oi_ref):
    xh, xl = _split(x_ref[...])
    or_ref[...] = _dot3(ma_ref[0], ma_ref[1], xh, xl)
    oi_ref[...] = _dot3(mb_ref[0], mb_ref[1], xh, xl)


def _dft_pair(ma, mb, x, tc=2048):
    R = ma.shape[1]
    K, C = x.shape
    return pl.pallas_call(
        _dft_pair_kernel,
        grid=(C // tc,),
        in_specs=[pl.BlockSpec(ma.shape, lambda j: (0, 0, 0)),
                  pl.BlockSpec(mb.shape, lambda j: (0, 0, 0)),
                  pl.BlockSpec((K, tc), lambda j: (0, j))],
        out_specs=[pl.BlockSpec((R, tc), lambda j: (0, j)), pl.BlockSpec((R, tc), lambda j: (0, j))],
        out_shape=[jax.ShapeDtypeStruct((R, C), F32), jax.ShapeDtypeStruct((R, C), F32)],
        compiler_params=_params("parallel"),
        name="dft_outer",
    )(ma, mb, x)


def _spec_conv_kernel(ar_ref, ai_ref, fr_ref, fi_ref, wf_ref, wi_ref, br_ref, bi_ref):
    kb, n2, _ = ar_ref.shape
    for k in range(kb):
        ah, al = _split(jnp.concatenate([ar_ref[k], ai_ref[k]], axis=0))
        x = _dot3(wf_ref[k, 0], wf_ref[k, 1], ah, al)
        fh, fl = _split(jnp.concatenate([fr_ref[k], fi_ref[k]], axis=0))
        g = _dot3(wf_ref[k, 0], wf_ref[k, 1], fh, fl)
        xr, xi, gr, gi = x[:n2], x[n2:], g[:n2], g[n2:]
        yh, yl = _split(jnp.concatenate([xr * gr - xi * gi, xr * gi + xi * gr], axis=0))
        b = _dot3(wi_ref[k, 0], wi_ref[k, 1], yh, yl)
        br_ref[k] = b[:n2]
        bi_ref[k] = b[n2:]


def _spec_conv(ar, ai, fr, fi, wf, wi, kb=4):
    n1, n2, C = ar.shape
    kb = min(kb, n1)
    blk = pl.BlockSpec((kb, n2, C), lambda k: (k, 0, 0))
    wblk = pl.BlockSpec((kb, 2, 2 * n2, 2 * n2), lambda k: (k, 0, 0, 0))
    return pl.pallas_call(
        _spec_conv_kernel,
        grid=(n1 // kb,),
        in_specs=[blk, blk, blk, blk, wblk, wblk],
        out_specs=[blk, blk],
        out_shape=[jax.ShapeDtypeStruct((n1, n2, C), F32), jax.ShapeDtypeStruct((n1, n2, C), F32)],
        compiler_params=_params("parallel"),
        name="spectrum_conv",
    )(ar, ai, fr, fi, wf, wi)


def _idft_out_kernel(ma_ref, mb_ref, br_ref, bi_ref, z_ref, x0_ref, bias_ref, o_ref):
    brh, brl = _split(br_ref[...])
    bih, bil = _split(bi_ref[...])
    y = _dot3(ma_ref[0], ma_ref[1], brh, brl) + _dot3(mb_ref[0], mb_ref[1], bih, bil)
    o_ref[...] = ((y + z_ref[...] * bias_ref[...]) * x0_ref[...]).astype(o_ref.dtype)


def _idft_out(ma, mb, br, bi, z2, x02, bias_t, tc=2048):
    R = ma.shape[1]
    K, C = br.shape
    col = lambda rws: pl.BlockSpec((rws, tc), lambda j: (0, j))
    return pl.pallas_call(
        _idft_out_kernel,
        grid=(C // tc,),
        in_specs=[pl.BlockSpec(ma.shape, lambda j: (0, 0, 0)), pl.BlockSpec(mb.shape, lambda j: (0, 0, 0)),
                  col(K), col(K), col(R), col(R), pl.BlockSpec((1, tc), lambda j: (0, 0))],
        out_specs=col(R),
        out_shape=jax.ShapeDtypeStruct((R, C), BF16),
        compiler_params=_params("parallel"),
        name="idft_outer",
    )(ma, mb, br, bi, z2, x02, bias_t)


def _split_const(m):
    m32 = jnp.asarray(np.asarray(m, np.float32))
    hi, lo = _split(m32)
    return jnp.stack([hi, lo])


def _fft_constants(L):
    N = 2 * L
    n2 = FFT_N2
    n1 = N // n2
    h = n1 // 2
    k1 = np.arange(n1)[:, None].astype(np.float64)
    ang = 2.0 * np.pi * k1 * np.arange(n1)[None, :] / n1
    c, s = np.cos(ang), np.sin(ang)
    za = np.concatenate([c[:, :h], s[:, :h]], axis=1)
    zb = np.concatenate([-s[:, :h], c[:, :h]], axis=1)
    ga, gb = c, -s
    ya = np.concatenate([c[:h], s[:h]], axis=0)
    yb = np.concatenate([-s[:h], c[:h]], axis=0)
    kk = np.arange(n2)[:, None].astype(np.float64)
    nn = np.arange(n2)[None, :].astype(np.float64)
    base = 2.0 * np.pi * kk * nn / n2
    tw = 2.0 * np.pi * np.arange(n1)[:, None, None] * nn[None] / N
    f = lambda m: jnp.asarray(np.asarray(m, np.float32))
    cb, sb, ct, st = f(np.cos(base))[None], f(np.sin(base))[None], f(np.cos(tw)), f(np.sin(tw))
    pr = cb * ct - sb * st
    pi = -(sb * ct + cb * st)
    wf = jnp.concatenate([jnp.concatenate([pr, -pi], axis=2), jnp.concatenate([pi, pr], axis=2)], axis=1)
    qr = jnp.transpose(pr, (0, 2, 1)) * (1.0 / N)
    qi = jnp.transpose(-pi, (0, 2, 1)) * (1.0 / N)
    wi = jnp.concatenate([jnp.concatenate([qr, -qi], axis=2), jnp.concatenate([qi, qr], axis=2)], axis=1)
    sp = lambda m: _split_const(m)
    wfs = jnp.stack([*_split(wf)], axis=1)
    wis = jnp.stack([*_split(wi)], axis=1)
    return dict(za=sp(za), zb=sp(zb), ga=sp(ga), gb=sp(gb), ya=sp(ya), yb=sp(yb), wf=wfs, wi=wis, n1=n1)


def _hy_pos_tables(L):
    t = jnp.linspace(0.0, 1.0, L, dtype=F32)[:, None]
    w = 2.0 * math.pi * jnp.arange(L, dtype=F32)[:, None] / L
    f = jnp.linspace(1e-4, HY_POS_BANDS - 1, HY_POS_BANDS, dtype=F32)[None, :]
    z = jnp.concatenate([t, jnp.cos(f * w), -jnp.sin(f * w)], axis=-1)
    max_decay = math.log(HY_DECAY_TARGET) / HY_FAST_PCT
    min_decay = math.log(HY_DECAY_TARGET) / HY_SLOW_PCT
    deltas = jnp.linspace(min_decay, max_decay, HY_WIDTH, dtype=F32)
    zp = jnp.pad(z, ((0, 0), (0, HY_FFN_HIDDEN - HY_POS_DIM)))
    ext = jnp.concatenate([zp, zp[0:1], jnp.flip(zp[1:], axis=0)], axis=0)
    return ext, jnp.abs(deltas)[None, :]


_TRI_ROWS = 16 + 8 * 7 + 8


def _tri_tables(tm):
    flat = [b for b in range(16)]
    for a in range(1, 8):
        flat += [16 * a + b for b in range(8)]
    flat += [16 * a for a in range(8, 16)]
    return jnp.asarray(np.tile(np.asarray(flat, np.float32)[:, None], (1, tm)))


def _topk_rows(scores, k, exact):
    n, tm = scores[0].shape
    rowk = lax.broadcasted_iota(jnp.int32, (k, tm), 0)
    if not exact:
        state = [(s, jnp.zeros((k, tm), F32)) for s in scores]
        for r in range(k):
            nxt = []
            for s, vals in state:
                m = jnp.max(s, axis=0, keepdims=True)
                nxt.append((jnp.where(s == m, -jnp.inf, s), jnp.where(rowk == r, m, vals)))
            state = nxt
        return [(vals, None) for _, vals in state]
    rowf = lax.broadcasted_iota(jnp.int32, (n, tm), 0).astype(F32)
    state = [(s, jnp.zeros((k, tm), F32), jnp.full((n, tm), float(k), F32)) for s in scores]
    for r in range(k):
        nxt = []
        for s, vals, rank in state:
            m = jnp.max(s, axis=0, keepdims=True)
            sel = rowf == jnp.min(jnp.where(s == m, rowf, float(n)), axis=0, keepdims=True)
            nxt.append((jnp.where(sel, -jnp.inf, s), jnp.where(rowk == r, m, vals),
                        jnp.where(sel, float(r), rank)))
        state = nxt
    return [(vals, rank) for _, vals, rank in state]


def _stair_cells(t1, t2, op):
    pieces = [op(t1[0:1], t2)]
    pieces += [op(t1[a:a + 1], t2[0:8]) for a in range(1, 8)]
    pieces += [op(t1[8:16], t2[0:1])]
    return jnp.concatenate(pieces, axis=0)


def _peer_tables(scores, flat, exact):
    K = PEER_TOPK
    tm = scores[0].shape[1]
    (sv1, rank1), (sv2, rank2) = _topk_rows(scores, K, exact)
    cand = _stair_cells(sv1, sv2, lambda x, y: x + y)
    if exact:
        row16 = lax.broadcasted_iota(jnp.int32, (K, tm), 0).astype(F32)
        length = jnp.zeros((K, tm), F32)
        for r in range(K):
            m = jnp.max(cand, axis=0, keepdims=True)
            f = jnp.min(jnp.where(cand == m, flat, 1e9), axis=0, keepdims=True)
            cand = jnp.where(flat == f, -jnp.inf, cand)
            length = jnp.where(row16 == jnp.floor(f * (1.0 / K)), length + 1.0, length)
        picked = cand == -jnp.inf
    else:
        for r in range(K):
            cand = jnp.where(cand == jnp.max(cand, axis=0, keepdims=True), -jnp.inf, cand)
        picked = cand == -jnp.inf
        cnt = jnp.where(picked, 1.0, 0.0)
        rows = [jnp.sum(cnt[0:16], axis=0, keepdims=True)]
        rows += [jnp.sum(cnt[8 + 8 * a:16 + 8 * a], axis=0, keepdims=True) for a in range(1, 8)]
        length = jnp.concatenate(rows + [cnt[72:80]], axis=0)
    s1, s2 = scores
    lfull = jnp.zeros_like(s1)
    if exact:
        ok = jnp.ones((1, tm), jnp.bool_)
        for r in range(K):
            lfull = jnp.where(rank1 == float(r), length[r:r + 1], lfull)
    else:
        rank2 = jnp.zeros_like(s2)
        for r in range(K):
            lfull = jnp.where(s1 == sv1[r:r + 1], length[r:r + 1], lfull)
            rank2 = jnp.where(sv2[r:r + 1] > s2, float(r + 1), rank2)
        reach = lambda s, sv: jnp.sum(jnp.where(s >= sv[K - 1:K], 1.0, 0.0), axis=0, keepdims=True)
        ok = (reach(s1, sv1) == float(K)) & (reach(s2, sv2) == float(K)) & \
             (jnp.sum(length, axis=0, keepdims=True) == float(K))
    es1 = jnp.exp(sv1 - sv1[0:1])
    es2 = jnp.exp(sv2 - sv2[0:1])
    ecand = _stair_cells(es1, es2, lambda x, y: x * y)
    z = jnp.sum(jnp.where(picked, ecand, 0.0), axis=0, keepdims=True)
    e1 = jnp.exp(scores[0] - sv1[0:1]) / z
    e2 = jnp.exp(scores[1] - sv2[0:1])
    return (lfull, e1, rank2, e2), jnp.where(ok, 1.0, 0.0)


def _peer_topk_kernel(hn_ref, wq_ref, sk_ref, flat_ref, l_ref, e1_ref, rb_ref, e2_ref):
    flat = flat_ref[...]

    def head(h, carry):
        q = _dot(hn_ref[...], wq_ref[h])
        scores = []
        for p in range(2):
            qh, ql = _split(q[:, p * PEER_HALF:(p + 1) * PEER_HALF])
            kh, kl = _split(sk_ref[h, p])
            nt = (((1,), (1,)), ((), ()))
            dg = lambda a, b: lax.dot_general(a, b, nt, preferred_element_type=F32)
            scores.append(dg(kh, qh) + dg(kh, ql) + dg(kl, qh))

        def write(tabs, cs):
            l_ref[h, :, cs], e1_ref[h, :, cs] = tabs[0], tabs[1]
            rb_ref[h, :, cs], e2_ref[h, :, cs] = tabs[2].astype(rb_ref.dtype), tabs[3].astype(e2_ref.dtype)

        wd = flat.shape[1]
        for c in range(q.shape[0] // wd):
            cs = slice(c * wd, (c + 1) * wd)
            sc = [s[:, cs] for s in scores]
            tabs, ok = _peer_tables(sc, flat, exact=False)
            write(tabs, cs)

            @pl.when(jnp.min(ok) < 0.5)
            def _():
                write(_peer_tables(sc, flat, exact=True)[0], cs)
        return carry

    lax.fori_loop(0, PEER_HEADS, head, 0)


def _peer_topk(hn, wq_heads, subkeys, tm=512, wd=256):
    T, D = hn.shape
    H = wq_heads.shape[0]
    tm = min(tm, T)
    wd = min(wd, tm)
    out = jax.ShapeDtypeStruct((H, PEER_NKEYS, T), F32)
    outb = jax.ShapeDtypeStruct((H, PEER_NKEYS, T), BF16)
    oblk = pl.BlockSpec((H, PEER_NKEYS, tm), lambda i: (0, 0, i))
    return pl.pallas_call(
        _peer_topk_kernel,
        grid=(T // tm,),
        in_specs=[pl.BlockSpec((tm, D), lambda i: (i, 0)),
                  pl.BlockSpec(wq_heads.shape, lambda i: (0, 0, 0)),
                  pl.BlockSpec((H, 2, PEER_NKEYS, PEER_HALF), lambda i: (0, 0, 0, 0)),
                  pl.BlockSpec((_TRI_ROWS, wd), lambda i: (0, 0))],
        out_specs=[oblk, oblk, oblk, oblk],
        out_shape=[out, out, outb, outb],
        compiler_params=_params("parallel"),
        name="peer_topk",
    )(hn, wq_heads, subkeys, _tri_tables(wd))


def _gelu_tanh(x):
    c = math.sqrt(2.0 / math.pi)
    h = 0.5 * x
    return h + h * jnp.tanh(x * (c + (c * 0.044715) * (x * x)))


def _peer_dense_kernel(ni, hnT_ref, u_ref, vT_ref, l_ref, e1_ref, rb_ref, e2_ref, x_ref, o_ref,
                       acc_ref, act_ref, a_ref):
    j = pl.program_id(1)
    nk = PEER_NKEYS

    @pl.when(j == 0)
    def _():
        acc_ref[...] = jnp.zeros_like(acc_ref)

    tm = hnT_ref.shape[1]
    i0 = pl.multiple_of(j * ni, ni)
    zero = jnp.zeros((), BF16)
    a_ref[...] = _dot(u_ref[...], hnT_ref[...])
    for ii in range(ni):
        w = None
        for h in range(PEER_HEADS):
            lrow = jnp.broadcast_to(l_ref[h, pl.ds(i0 + ii, 1), :], (BF16_ROWS, tm)).astype(BF16)
            erow = jnp.broadcast_to(e1_ref[h, pl.ds(i0 + ii, 1), :], (BF16_ROWS, tm)).astype(BF16)
            wh = jnp.where(rb_ref[h] < lrow[None], e2_ref[h], zero) * erow[None]
            w = wh if w is None else w + wh
        a = a_ref[ii * nk:(ii + 1) * nk, :].astype(BF16)
        act_ref[ii * nk:(ii + 1) * nk, :] = _gelu_tanh(a) * w.reshape(nk, tm)
    acc_ref[...] += _dot(vT_ref[...], act_ref[...])

    @pl.when(j == pl.num_programs(1) - 1)
    def _():
        o_ref[...] = x_ref[...] + acc_ref[...].T


def _peer_dense(hnT, u_bf, vT_bf, tabs, x, tm=512, ni=16):
    D, T = hnT.shape
    NE = u_bf.shape[0]
    te = ni * PEER_NKEYS
    tm = min(tm, T)
    tab = pl.BlockSpec((PEER_HEADS, PEER_NKEYS, tm), lambda i, j: (0, 0, i))
    grp = PEER_NKEYS // BF16_ROWS
    tabb = pl.BlockSpec((PEER_HEADS, grp, BF16_ROWS, tm), lambda i, j: (0, 0, 0, i))
    lt, e1, rb, e2 = tabs
    rb, e2 = [t.reshape(PEER_HEADS, grp, BF16_ROWS, T) for t in (rb, e2)]
    return pl.pallas_call(
        functools.partial(_peer_dense_kernel, ni),
        grid=(T // tm, NE // te),
        in_specs=[pl.BlockSpec((D, tm), lambda i, j: (0, i)),
                  pl.BlockSpec((te, D), lambda i, j: (j, 0)),
                  pl.BlockSpec((D, te), lambda i, j: (0, j)),
                  tab, tab, tabb, tabb,
                  pl.BlockSpec((tm, D), lambda i, j: (i, 0))],
        out_specs=pl.BlockSpec((tm, D), lambda i, j: (i, 0)),
        out_shape=jax.ShapeDtypeStruct((T, D), F32),
        scratch_shapes=[pltpu.VMEM((D, tm), F32), pltpu.VMEM((te, tm), BF16), pltpu.VMEM((te, tm), F32)],
        compiler_params=_params("parallel", "arbitrary"),
        name="peer_experts",
    )(hnT, u_bf, vT_bf, lt, e1, rb, e2, x)


def kernel(x, norm1_g, w_in, gate_b, na_q_g, na_k_g, na_rpb, hy_conv_w, hy_conv_b, hy_w1, hy_b1, hy_freq,
           hy_w2, hy_b2, hy_w3, hy_bias, w_up_na, w_up_hy, w_out, norm2_g, peer_wq, peer_subkeys, peer_u,
           peer_v):
    B, S, D = x.shape
    assert B == 2, "the long convolution packs the two batches as one complex sequence"
    depth = w_in.shape[0]
    T = B * S
    xt = x.reshape(T, D)

    fc = _fft_constants(S)
    n1 = fc["n1"]
    pos_ext, dec_ext = _hy_pos_tables(S)
    head_sum = jnp.asarray(np.kron(np.eye(NA_HEADS), np.ones((NA_HEAD_DIM, NA_HEAD_DIM))), BF16)
    o_qk, o_v, o_hy = 2 * NA_WIDTH, 3 * NA_WIDTH, 3 * NA_WIDTH + 3 * HY_WIDTH
    tc = 2048
    ncol = FFT_N2 * HY_WIDTH

    for l in range(depth):
        (hn,) = _rmsnorm(xt, norm1_g[l])
        w = w_in[l].astype(BF16)
        qk_gain = jnp.concatenate([jnp.tile(na_q_g[l], NA_HEADS) * (NA_HEAD_DIM ** -0.5),
                                   jnp.tile(na_k_g[l], NA_HEADS)]).reshape(1, o_qk).astype(F32)
        qk = _mm(hn, w, cols=(0, o_qk), out_dtype=BF16, epi=_epi_head_rmsnorm,
                 extras=[(qk_gain, "col"), (head_sum, "full")], name="proj_qk")
        v = _mm(hn, w, cols=(o_qk, o_v), out_dtype=BF16, name="proj_v")
        hy = _mm(hn, w, cols=(o_v, o_hy), out_dtype=F32, name="proj_hy")
        gates = _mm(hn, w, cols=(o_hy, w.shape[1]), out_dtype=BF16, epi=_epi_gate,
                    extras=[(gate_b[l].reshape(1, 2 * D), "col")], tn=1024, name="proj_gates")

        a_out = _na(qk, v, _na_bias_table(na_rpb[l]), B, S)

        z, x0 = _hy_prep(hy, hy_conv_w[l], hy_conv_b[l], S)
        w1p = jnp.pad(hy_w1[l], ((0, HY_FFN_HIDDEN - HY_POS_DIM), (0, 0)))
        g = _hy_filter(S, pos_ext, dec_ext, w1p, hy_b1[l], hy_freq[l], hy_w2[l], hy_b2[l], hy_w3[l])
        gr, gi = _dft_pair(fc["ga"], fc["gb"], g.reshape(n1, ncol), tc)
        z2 = z.reshape(n1, ncol)
        ar, ai = _dft_pair(fc["za"], fc["zb"], z2, tc)
        cube = lambda t: t.reshape(n1, FFT_N2, HY_WIDTH)
        br, bi = _spec_conv(cube(ar), cube(ai), cube(gr), cube(gi), fc["wf"], fc["wi"])
        bias_t = jnp.tile(hy_bias[l], tc // HY_WIDTH).reshape(1, tc)
        b_out = _idft_out(fc["ya"], fc["yb"], br.reshape(n1, ncol), bi.reshape(n1, ncol),
                          z2, x0.reshape(n1, ncol), bias_t, tc).reshape(T, HY_WIDTH)

        merged = _merge(a_out, b_out, w_up_na[l].astype(BF16), w_up_hy[l].astype(BF16), gates)
        xt = _mm(merged, w_out[l].astype(BF16), out_dtype=F32, epi=_epi_residual,
                 extras=[(xt, "tile")], tn=1024, name="proj_out")

        hn2, hn2T = _rmsnorm(xt, norm2_g[l], transposed=True)
        wq_heads = peer_wq[l].reshape(D, PEER_HEADS, 2 * PEER_HALF).transpose(1, 0, 2).astype(BF16)
        tabs = _peer_topk(hn2, wq_heads, peer_subkeys[l])
        xt = _peer_dense(hn2T, peer_u[l].astype(BF16), peer_v[l].T.astype(BF16), tabs, xt)
    return xt.reshape(B, S, D)
```

```python
import functools
import math

import numpy as np
import jax
import jax.numpy as jnp
from jax import lax
from jax.experimental import pallas as pl
from jax.experimental.pallas import tpu as pltpu

F32 = jnp.float32
BF16 = jnp.bfloat16

GRID_W = 64
NA_HEADS = 8
NA_HEAD_DIM = 64
NA_WIDTH = NA_HEADS * NA_HEAD_DIM
NA_KR = 8
NA_KW = 16
HY_WIDTH = 512
HY_POS_BANDS = 16
HY_POS_DIM = 1 + 2 * HY_POS_BANDS
HY_FFN_HIDDEN = 64
HY_DECAY_TARGET = 1e-2
HY_FAST_PCT = 0.3
HY_SLOW_PCT = 1.5
PEER_HEADS = 8
PEER_NKEYS = 128
PEER_TOPK = 16
PEER_HALF = 128
EPS = 1e-6
NEG = -1e30

LANES = 128
VMEM_LIMIT_BYTES = 56 * 1024 * 1024
FFT_N2 = 128
BF16_ROWS = 16


def _params(*sem):
    return pltpu.CompilerParams(dimension_semantics=sem, vmem_limit_bytes=VMEM_LIMIT_BYTES)


def _split(x):
    hi = x.astype(BF16)
    lo = (x - hi.astype(F32)).astype(BF16)
    return hi, lo


def _dot(a, b):
    return jnp.dot(a, b, preferred_element_type=F32)


def _dot3(ah, al, bh, bl):
    return _dot(ah, bh) + _dot(ah, bl) + _dot(al, bh)


def _rmsnorm_kernel(x_ref, g_ref, o_ref, *maybe_ot_ref):
    x = x_ref[...]
    y = x * lax.rsqrt(jnp.mean(x * x, axis=-1, keepdims=True) + EPS)
    y = y * g_ref[...]
    o_ref[...] = y.astype(BF16)
    for ot_ref in maybe_ot_ref:
        ot_ref[...] = y.T.astype(BF16)


def _rmsnorm(x, g, transposed=False, tm=512):
    T, D = x.shape
    out_specs = [pl.BlockSpec((tm, D), lambda i: (i, 0))]
    out_shape = [jax.ShapeDtypeStruct((T, D), BF16)]
    if transposed:
        out_specs.append(pl.BlockSpec((D, tm), lambda i: (0, i)))
        out_shape.append(jax.ShapeDtypeStruct((D, T), BF16))
    return pl.pallas_call(
        _rmsnorm_kernel,
        grid=(T // tm,),
        in_specs=[pl.BlockSpec((tm, D), lambda i: (i, 0)),
                  pl.BlockSpec((1, D), lambda i: (0, 0))],
        out_specs=out_specs,
        out_shape=out_shape,
        compiler_params=_params("parallel"),
        name="rmsnorm",
    )(x, g.reshape(1, D))


def _mm_kernel(epi, n_extra, a_ref, w_ref, *rest):
    o_ref = rest[n_extra]
    acc = _dot(a_ref[...], w_ref[...])
    o_ref[...] = epi(acc, *[r[...] for r in rest[:n_extra]]).astype(o_ref.dtype)


def _mm(a, w, *, out_dtype, epi=lambda acc: acc, extras=(), tm=1024, tn=512, cols=None, name="mm"):
    M, K = a.shape
    c0, c1 = cols if cols is not None else (0, w.shape[1])
    N = c1 - c0
    tm, tn = min(tm, M), min(tn, N)
    assert c0 % tn == 0 and N % tn == 0
    j0 = c0 // tn
    out_spec = pl.BlockSpec((tm, tn), lambda i, j: (i, j))
    out_shape = jax.ShapeDtypeStruct((M, N), out_dtype)
    in_specs = [pl.BlockSpec((tm, K), lambda i, j: (i, 0)),
                pl.BlockSpec((K, tn), lambda i, j: (0, j0 + j))]
    args = [a, w]
    for arr, kind in extras:
        if kind == "col":
            in_specs.append(pl.BlockSpec((1, tn), lambda i, j: (0, j)))
        elif kind == "tile":
            in_specs.append(pl.BlockSpec((tm, tn), lambda i, j: (i, j)))
        else:
            in_specs.append(pl.BlockSpec(arr.shape, lambda i, j, nd=arr.ndim: (0,) * nd))
        args.append(arr)
    return pl.pallas_call(
        functools.partial(_mm_kernel, epi, len(extras)),
        grid=(M // tm, N // tn),
        in_specs=in_specs,
        out_specs=out_spec,
        out_shape=out_shape,
        compiler_params=_params("parallel", "parallel"),
        name=name,
    )(*args)


def _epi_head_rmsnorm(acc, gain, head_sum):
    hi, lo = _split(acc * acc)
    ms = (_dot(hi, head_sum) + _dot(lo, head_sum)) * (1.0 / NA_HEAD_DIM)
    return acc * lax.rsqrt(ms + EPS) * gain


def _epi_gate(acc, bias):
    return jax.nn.sigmoid(acc + bias)


def _epi_residual(acc, res):
    return res + acc


def _merge_kernel(a_ref, b_ref, wa_ref, wb_ref, g_ref, o_ref):
    d = o_ref.shape[1]
    ya = _dot(a_ref[...], wa_ref[...])
    yb = _dot(b_ref[...], wb_ref[...])
    g = g_ref[...].astype(F32)
    o_ref[...] = (g[:, :d] * ya + g[:, d:] * yb).astype(o_ref.dtype)


def _merge(a_out, b_out, w_na, w_hy, gates, tm=512):
    T, wa = a_out.shape
    D = w_na.shape[1]
    return pl.pallas_call(
        _merge_kernel,
        grid=(T // tm,),
        in_specs=[pl.BlockSpec((tm, wa), lambda i: (i, 0)),
                  pl.BlockSpec((tm, b_out.shape[1]), lambda i: (i, 0)),
                  pl.BlockSpec(w_na.shape, lambda i: (0, 0)),
                  pl.BlockSpec(w_hy.shape, lambda i: (0, 0)),
                  pl.BlockSpec((tm, 2 * D), lambda i: (i, 0))],
        out_specs=pl.BlockSpec((tm, D), lambda i: (i, 0)),
        out_shape=jax.ShapeDtypeStruct((T, D), BF16),
        compiler_params=_params("parallel"),
        name="merge",
    )(a_out, b_out, w_na, w_hy, gates)


def _na_kernel(rows, rb, q_ref, k_ref, v_ref, bias_ref, o_ref, s_ref, p_ref):
    blk = pl.program_id(2)
    lane = lax.broadcasted_iota(jnp.int32, (GRID_W, LANES), 1)
    nk = NA_KR * GRID_W
    starts = []
    for i in range(rb):
        r = blk * rb + i
        r0 = jnp.clip(r - NA_KR // 2, 0, rows - NA_KR)
        dr0 = r0 - r + (NA_KR - 1)
        start = pl.multiple_of(r0 * GRID_W, GRID_W)
        starts.append(start)
        qr = q_ref[i * GRID_W:(i + 1) * GRID_W, :]
        kw = k_ref[pl.ds(start, nk), :]
        for hh in range(2):
            in_head = (lane >= NA_HEAD_DIM) if hh else (lane < NA_HEAD_DIM)
            qm = jnp.where(in_head, qr, jnp.zeros_like(qr))
            s = lax.dot_general(qm, kw, (((1,), (1,)), ((), ())), preferred_element_type=F32)
            s_ref[(2 * i + hh) * GRID_W:(2 * i + hh + 1) * GRID_W, :] = s + bias_ref[hh, dr0]
    s = s_ref[...]
    p = jnp.exp(s - jnp.max(s, axis=-1, keepdims=True))
    p_ref[...] = (p * (1.0 / jnp.sum(p, axis=-1, keepdims=True))).astype(BF16)
    for i in range(rb):
        vw = v_ref[pl.ds(starts[i], nk), :]
        o0 = _dot(p_ref[(2 * i) * GRID_W:(2 * i + 1) * GRID_W, :], vw)
        o1 = _dot(p_ref[(2 * i + 1) * GRID_W:(2 * i + 2) * GRID_W, :], vw)
        o_ref[i * GRID_W:(i + 1) * GRID_W, :] = jnp.where(lane < NA_HEAD_DIM, o0, o1).astype(o_ref.dtype)


def _na_bias_table(rpb):
    c = np.arange(GRID_W)
    c0 = np.clip(c - NA_KW // 2, 0, GRID_W - NA_KW)
    col_in = (c[None, :] >= c0[:, None]) & (c[None, :] < c0[:, None] + NA_KW)
    dc_idx = np.clip(c[None, :] - c[:, None] + (NA_KW - 1), 0, 2 * NA_KW - 2)
    onehot = jnp.asarray((dc_idx[:, :, None] == np.arange(2 * NA_KW - 1)).astype(np.float32))
    rows = jnp.stack([rpb.astype(F32)[:, d:d + NA_KR] for d in range(NA_KR)], axis=1)
    b = jnp.einsum("hdjc,qkc->hdqjk", rows, onehot, precision=lax.Precision.HIGHEST)
    b = jnp.where(jnp.asarray(col_in)[None, None, :, None, :], b, NEG)
    return b.reshape(NA_HEADS, NA_KR, GRID_W, NA_KR * GRID_W)


def _na(qk, v, bias, B, S, rb=16):
    rows = S // GRID_W
    assert rows >= NA_KR and rows % rb == 0
    nblk = rows // rb
    tq = rb * GRID_W
    kofs = NA_WIDTH // LANES
    return pl.pallas_call(
        functools.partial(_na_kernel, rows, rb),
        grid=(B, NA_HEADS // 2, nblk),
        in_specs=[pl.BlockSpec((tq, LANES), lambda b, p, r: (b * nblk + r, p)),
                  pl.BlockSpec((S, LANES), lambda b, p, r: (b, kofs + p)),
                  pl.BlockSpec((S, LANES), lambda b, p, r: (b, p)),
                  pl.BlockSpec((2, NA_KR, GRID_W, NA_KR * GRID_W), lambda b, p, r: (p, 0, 0, 0))],
        out_specs=pl.BlockSpec((tq, LANES), lambda b, p, r: (b * nblk + r, p)),
        out_shape=jax.ShapeDtypeStruct((B * S, NA_WIDTH), BF16),
        scratch_shapes=[pltpu.VMEM((2 * tq, NA_KR * GRID_W), F32), pltpu.VMEM((2 * tq, NA_KR * GRID_W), BF16)],
        compiler_params=_params("parallel", "parallel", "parallel"),
        name="na_attention",
    )(qk, qk, v, bias)


def _hy_prep_kernel(S, hy_ref, prev_ref, next_ref, w_ref, b_ref, z_ref, x0_ref):
    tm, C = hy_ref.shape
    tok0 = pl.program_id(0) * tm
    h = hy_ref[...]
    row = lax.broadcasted_iota(jnp.int32, (tm, C), 0)
    prev_row = jnp.where(tok0 % S == 0, 0.0, prev_ref[7:8, :])
    next_row = jnp.where((tok0 + tm) % S == 0, 0.0, next_ref[0:1, :])
    up = jnp.where(row == 0, prev_row, pltpu.roll(h, 1, axis=0))
    dn = jnp.where(row == tm - 1, next_row, pltpu.roll(h, tm - 1, axis=0))
    u = up * w_ref[0:1, :] + h * w_ref[1:2, :] + dn * w_ref[2:3, :] + b_ref[...]
    c = C // 3
    x0_ref[...] = u[:, :c]
    z_ref[...] = u[:, 2 * c:] * u[:, c:2 * c]


def _hy_prep(hy, conv_w, conv_b, S, tm=512):
    T, C = hy.shape
    nb = tm // 8
    last = T // 8 - 1
    return pl.pallas_call(
        functools.partial(_hy_prep_kernel, S),
        grid=(T // tm,),
        in_specs=[pl.BlockSpec((tm, C), lambda i: (i, 0)),
                  pl.BlockSpec((8, C), lambda i: (jnp.maximum(i * nb - 1, 0), 0)),
                  pl.BlockSpec((8, C), lambda i: (jnp.minimum((i + 1) * nb, last), 0)),
                  pl.BlockSpec((3, C), lambda i: (0, 0)),
                  pl.BlockSpec((1, C), lambda i: (0, 0))],
        out_specs=[pl.BlockSpec((tm, C // 3), lambda i: (i, 0)),
                   pl.BlockSpec((tm, C // 3), lambda i: (i, 0))],
        out_shape=[jax.ShapeDtypeStruct((T, C // 3), F32), jax.ShapeDtypeStruct((T, C // 3), F32)],
        compiler_params=_params("parallel"),
        name="hyena_prep",
    )(hy, hy, hy, conv_w, conv_b.reshape(1, C))


def _hy_filter_kernel(L, pos_ref, dec_ref, w1_ref, b1_ref, fr_ref, w2_ref, b2_ref, w3_ref, g_ref):
    tr, C = g_ref.shape
    fr = fr_ref[...]
    w1h, w1l = _split(w1_ref[...])
    w2h, w2l = _split(w2_ref[...])
    w3h, w3l = _split(w3_ref[...])
    ph, plo = _split(pos_ref[...])
    h = jnp.sin(fr * (_dot3(ph, plo, w1h, w1l) + _dot(plo, w1l) + b1_ref[...]))
    hh, hl = _split(h)
    h = jnp.sin(fr * (_dot3(hh, hl, w2h, w2l) + _dot(hl, w2l) + b2_ref[...]))
    hh, hl = _split(h)
    h = _dot3(hh, hl, w3h, w3l) + _dot(hl, w3l)
    dec = jnp.exp(-(pos_ref[:, 0:1] * dec_ref[...]))
    hf = h[:, :C] * dec
    hb = h[:, C:] * dec
    n = pl.program_id(0) * tr + lax.broadcasted_iota(jnp.int32, (tr, C), 0)
    g = jnp.where(n < L, hf, hb)
    g = jnp.where(n == L, 0.0, g)
    g_ref[...] = jnp.where(n == 0, hf + hb, g)


def _hy_filter(L, pos_ext, dec_ext, w1p, b1, freq, w2, b2, w3, tr=1024):
    N = 2 * L
    tr = min(tr, N)
    H = HY_FFN_HIDDEN
    full = lambda shape: pl.BlockSpec(shape, lambda i: (0, 0))
    return pl.pallas_call(
        functools.partial(_hy_filter_kernel, L),
        grid=(N // tr,),
        in_specs=[pl.BlockSpec((tr, pos_ext.shape[1]), lambda i: (i, 0)),
                  full((1, HY_WIDTH)),
                  full(w1p.shape), full((1, H)), full((1, H)), full((H, H)), full((1, H)),
                  full((H, 2 * HY_WIDTH))],
        out_specs=pl.BlockSpec((tr, HY_WIDTH), lambda i: (i, 0)),
        out_shape=jax.ShapeDtypeStruct((N, HY_WIDTH), F32),
        compiler_params=_params("parallel"),
        name="hyena_filter",
    )(pos_ext, dec_ext, w1p, b1.reshape(1, H), freq.reshape(1, H), w2, b2.reshape(1, H), w3)


def _dft_pair_kernel(ma_ref, mb_ref, x_ref, or_ref, oi_ref):
    xh, xl = _split(x_ref[...])
    or_ref[...] = _dot3(ma_ref[0], ma_ref[1], xh, xl)
    oi_ref[...] = _dot3(mb_ref[0], mb_ref[1], xh, xl)


def _dft_pair(ma, mb, x, tc=2048):
    R = ma.shape[1]
    K, C = x.shape
    return pl.pallas_call(
        _dft_pair_kernel,
        grid=(C // tc,),
        in_specs=[pl.BlockSpec(ma.shape, lambda j: (0, 0, 0)),
                  pl.BlockSpec(mb.shape, lambda j: (0, 0, 0)),
                  pl.BlockSpec((K, tc), lambda j: (0, j))],
        out_specs=[pl.BlockSpec((R, tc), lambda j: (0, j)), pl.BlockSpec((R, tc), lambda j: (0, j))],
        out_shape=[jax.ShapeDtypeStruct((R, C), F32), jax.ShapeDtypeStruct((R, C), F32)],
        compiler_params=_params("parallel"),
        name="dft_outer",
    )(ma, mb, x)


def _spec_conv_kernel(ar_ref, ai_ref, fr_ref, fi_ref, wf_ref, wi_ref, br_ref, bi_ref):
    kb, n2, _ = ar_ref.shape
    for k in range(kb):
        ah, al = _split(jnp.concatenate([ar_ref[k], ai_ref[k]], axis=0))
        x = _dot3(wf_ref[k, 0], wf_ref[k, 1], ah, al)
        fh, fl = _split(jnp.concatenate([fr_ref[k], fi_ref[k]], axis=0))
        g = _dot3(wf_ref[k, 0], wf_ref[k, 1], fh, fl)
        xr, xi, gr, gi = x[:n2], x[n2:], g[:n2], g[n2:]
        yh, yl = _split(jnp.concatenate([xr * gr - xi * gi, xr * gi + xi * gr], axis=0))
        b = _dot3(wi_ref[k, 0], wi_ref[k, 1], yh, yl)
        br_ref[k] = b[:n2]
        bi_ref[k] = b[n2:]


def _spec_conv(ar, ai, fr, fi, wf, wi, kb=4):
    n1, n2, C = ar.shape
    kb = min(kb, n1)
    blk = pl.BlockSpec((kb, n2, C), lambda k: (k, 0, 0))
    wblk = pl.BlockSpec((kb, 2, 2 * n2, 2 * n2), lambda k: (k, 0, 0, 0))
    return pl.pallas_call(
        _spec_conv_kernel,
        grid=(n1 // kb,),
        in_specs=[blk, blk, blk, blk, wblk, wblk],
        out_specs=[blk, blk],
        out_shape=[jax.ShapeDtypeStruct((n1, n2, C), F32), jax.ShapeDtypeStruct((n1, n2, C), F32)],
        compiler_params=_params("parallel"),
        name="spectrum_conv",
    )(ar, ai, fr, fi, wf, wi)


def _idft_out_kernel(ma_ref, mb_ref, br_ref, bi_ref, z_ref, x0_ref, bias_ref, o_ref):
    brh, brl = _split(br_ref[...])
    bih, bil = _split(bi_ref[...])
    y = _dot3(ma_ref[0], ma_ref[1], brh, brl) + _dot3(mb_ref[0], mb_ref[1], bih, bil)
    o_ref[...] = ((y + z_ref[...] * bias_ref[...]) * x0_ref[...]).astype(o_ref.dtype)


def _idft_out(ma, mb, br, bi, z2, x02, bias_t, tc=2048):
    R = ma.shape[1]
    K, C = br.shape
    col = lambda rws: pl.BlockSpec((rws, tc), lambda j: (0, j))
    return pl.pallas_call(
        _idft_out_kernel,
        grid=(C // tc,),
        in_specs=[pl.BlockSpec(ma.shape, lambda j: (0, 0, 0)), pl.BlockSpec(mb.shape, lambda j: (0, 0, 0)),
                  col(K), col(K), col(R), col(R), pl.BlockSpec((1, tc), lambda j: (0, 0))],
        out_specs=col(R),
        out_shape=jax.ShapeDtypeStruct((R, C), BF16),
        compiler_params=_params("parallel"),
        name="idft_outer",
    )(ma, mb, br, bi, z2, x02, bias_t)


def _split_const(m):
    m32 = jnp.asarray(np.asarray(m, np.float32))
    hi, lo = _split(m32)
    return jnp.stack([hi, lo])


def _fft_constants(L):
    N = 2 * L
    n2 = FFT_N2
    n1 = N // n2
    h = n1 // 2
    k1 = np.arange(n1)[:, None].astype(np.float64)
    ang = 2.0 * np.pi * k1 * np.arange(n1)[None, :] / n1
    c, s = np.cos(ang), np.sin(ang)
    za = np.concatenate([c[:, :h], s[:, :h]], axis=1)
    zb = np.concatenate([-s[:, :h], c[:, :h]], axis=1)
    ga, gb = c, -s
    ya = np.concatenate([c[:h], s[:h]], axis=0)
    yb = np.concatenate([-s[:h], c[:h]], axis=0)
    kk = np.arange(n2)[:, None].astype(np.float64)
    nn = np.arange(n2)[None, :].astype(np.float64)
    base = 2.0 * np.pi * kk * nn / n2
    tw = 2.0 * np.pi * np.arange(n1)[:, None, None] * nn[None] / N
    f = lambda m: jnp.asarray(np.asarray(m, np.float32))
    cb, sb, ct, st = f(np.cos(base))[None], f(np.sin(base))[None], f(np.cos(tw)), f(np.sin(tw))
    pr = cb * ct - sb * st
    pi = -(sb * ct + cb * st)
    wf = jnp.concatenate([jnp.concatenate([pr, -pi], axis=2), jnp.concatenate([pi, pr], axis=2)], axis=1)
    qr = jnp.transpose(pr, (0, 2, 1)) * (1.0 / N)
    qi = jnp.transpose(-pi, (0, 2, 1)) * (1.0 / N)
    wi = jnp.concatenate([jnp.concatenate([qr, -qi], axis=2), jnp.concatenate([qi, qr], axis=2)], axis=1)
    sp = lambda m: _split_const(m)
    wfs = jnp.stack([*_split(wf)], axis=1)
    wis = jnp.stack([*_split(wi)], axis=1)
    return dict(za=sp(za), zb=sp(zb), ga=sp(ga), gb=sp(gb), ya=sp(ya), yb=sp(yb), wf=wfs, wi=wis, n1=n1)


def _hy_pos_tables(L):
    t = jnp.linspace(0.0, 1.0, L, dtype=F32)[:, None]
    w = 2.0 * math.pi * jnp.arange(L, dtype=F32)[:, None] / L
    f = jnp.linspace(1e-4, HY_POS_BANDS - 1, HY_POS_BANDS, dtype=F32)[None, :]
    z = jnp.concatenate([t, jnp.cos(f * w), -jnp.sin(f * w)], axis=-1)
    max_decay = math.log(HY_DECAY_TARGET) / HY_FAST_PCT
    min_decay = math.log(HY_DECAY_TARGET) / HY_SLOW_PCT
    deltas = jnp.linspace(min_decay, max_decay, HY_WIDTH, dtype=F32)
    zp = jnp.pad(z, ((0, 0), (0, HY_FFN_HIDDEN - HY_POS_DIM)))
    ext = jnp.concatenate([zp, zp[0:1], jnp.flip(zp[1:], axis=0)], axis=0)
    return ext, jnp.abs(deltas)[None, :]


_TRI_ROWS = 16 + 8 * 7 + 8


def _tri_tables(tm):
    flat = [b for b in range(16)]
    for a in range(1, 8):
        flat += [16 * a + b for b in range(8)]
    flat += [16 * a for a in range(8, 16)]
    return jnp.asarray(np.tile(np.asarray(flat, np.float32)[:, None], (1, tm)))


def _sort16_pairs():
    def merge(lo, hi, r):
        step = r * 2
        if step < hi - lo:
            yield from merge(lo, hi, step)
            yield from merge(lo + r, hi, step)
            yield from [(i, i + r) for i in range(lo + r, hi - r, step)]
        else:
            yield (lo, lo + r)

    def sort(lo, hi):
        if hi - lo >= 1:
            mid = lo + (hi - lo) // 2
            yield from sort(lo, mid)
            yield from sort(mid + 1, hi)
            yield from merge(lo, hi, 1)
    return list(sort(0, 15))


_SORT16 = _sort16_pairs()
_BITONIC16 = [(i, i + d) for d in (8, 4, 2, 1) for i in range(16) if not i & d]


def _cmpx(x, pairs):
    x = list(x)
    for i, j in pairs:
        x[i], x[j] = jnp.maximum(x[i], x[j]), jnp.minimum(x[i], x[j])
    return x


def _top16_sorted(s):
    n, tm = s.shape
    sub = n // PEER_TOPK
    x = _cmpx([s[r * sub:(r + 1) * sub] for r in range(PEER_TOPK)], _SORT16)
    shift = sub // 2
    while shift:
        y = [pltpu.roll(v, shift, axis=0) for v in x]
        x = _cmpx([jnp.maximum(x[i], y[PEER_TOPK - 1 - i]) for i in range(PEER_TOPK)], _BITONIC16)
        shift //= 2
    rowk = lax.broadcasted_iota(jnp.int32, (PEER_TOPK, tm), 0)
    vals = jnp.zeros((PEER_TOPK, tm), F32)
    for r in range(PEER_TOPK):
        vals = jnp.where(rowk == r, jnp.concatenate([x[r], x[r]], axis=0), vals)
    return vals


def _topk_rows(scores, k, exact):
    n, tm = scores[0].shape
    if not exact:
        return [(_top16_sorted(s), None) for s in scores]
    rowk = lax.broadcasted_iota(jnp.int32, (k, tm), 0)
    rowf = lax.broadcasted_iota(jnp.int32, (n, tm), 0).astype(F32)
    state = [(s, jnp.zeros((k, tm), F32), jnp.full((n, tm), float(k), F32)) for s in scores]
    for r in range(k):
        nxt = []
        for s, vals, rank in state:
            m = jnp.max(s, axis=0, keepdims=True)
            sel = rowf == jnp.min(jnp.where(s == m, rowf, float(n)), axis=0, keepdims=True)
            nxt.append((jnp.where(sel, -jnp.inf, s), jnp.where(rowk == r, m, vals),
                        jnp.where(sel, float(r), rank)))
        state = nxt
    return [(vals, rank) for _, vals, rank in state]


def _stair_cells(t1, t2, op):
    pieces = [op(t1[0:1], t2)]
    pieces += [op(t1[a:a + 1], t2[0:8]) for a in range(1, 8)]
    pieces += [op(t1[8:16], t2[0:1])]
    return jnp.concatenate(pieces, axis=0)


def _peer_tables(scores, flat, exact):
    K = PEER_TOPK
    tm = scores[0].shape[1]
    (sv1, rank1), (sv2, rank2) = _topk_rows(scores, K, exact)
    cand = _stair_cells(sv1, sv2, lambda x, y: x + y)
    if exact:
        row16 = lax.broadcasted_iota(jnp.int32, (K, tm), 0).astype(F32)
        length = jnp.zeros((K, tm), F32)
        for r in range(K):
            m = jnp.max(cand, axis=0, keepdims=True)
            f = jnp.min(jnp.where(cand == m, flat, 1e9), axis=0, keepdims=True)
            cand = jnp.where(flat == f, -jnp.inf, cand)
            length = jnp.where(row16 == jnp.floor(f * (1.0 / K)), length + 1.0, length)
        picked = cand == -jnp.inf
    else:
        for r in range(K):
            cand = jnp.where(cand == jnp.max(cand, axis=0, keepdims=True), -jnp.inf, cand)
        picked = cand == -jnp.inf
        cnt = jnp.where(picked, 1.0, 0.0)
        rows = [jnp.sum(cnt[0:16], axis=0, keepdims=True)]
        rows += [jnp.sum(cnt[8 + 8 * a:16 + 8 * a], axis=0, keepdims=True) for a in range(1, 8)]
        length = jnp.concatenate(rows + [cnt[72:80]], axis=0)
    s1, s2 = scores
    lfull = jnp.zeros_like(s1)
    if exact:
        ok = jnp.ones((1, tm), jnp.bool_)
        for r in range(K):
            lfull = jnp.where(rank1 == float(r), length[r:r + 1], lfull)
    else:
        rank2 = jnp.zeros_like(s2)
        for r in range(K):
            lfull = jnp.where(s1 == sv1[r:r + 1], length[r:r + 1], lfull)
            rank2 = jnp.where(sv2[r:r + 1] > s2, float(r + 1), rank2)
        reach = lambda s, sv: jnp.sum(jnp.where(s >= sv[K - 1:K], 1.0, 0.0), axis=0, keepdims=True)
        ok = (reach(s1, sv1) == float(K)) & (reach(s2, sv2) == float(K)) & \
             (jnp.sum(length, axis=0, keepdims=True) == float(K))
    es1 = jnp.exp(sv1 - sv1[0:1])
    es2 = jnp.exp(sv2 - sv2[0:1])
    ecand = _stair_cells(es1, es2, lambda x, y: x * y)
    z = jnp.sum(jnp.where(picked, ecand, 0.0), axis=0, keepdims=True)
    e1 = jnp.exp(scores[0] - sv1[0:1]) / z
    e2 = jnp.exp(scores[1] - sv2[0:1])
    return (lfull, e1, rank2, e2), jnp.where(ok, 1.0, 0.0)


def _peer_topk_kernel(hn_ref, wq_ref, sk_ref, flat_ref, l_ref, e1_ref, rb_ref, e2_ref):
    flat = flat_ref[...]

    def head(h, carry):
        q = _dot(hn_ref[...], wq_ref[h])
        scores = []
        for p in range(2):
            qh, ql = _split(q[:, p * PEER_HALF:(p + 1) * PEER_HALF])
            kh, kl = _split(sk_ref[h, p])
            nt = (((1,), (1,)), ((), ()))
            dg = lambda a, b: lax.dot_general(a, b, nt, preferred_element_type=F32)
            scores.append(dg(kh, qh) + dg(kh, ql) + dg(kl, qh))

        def write(tabs, cs):
            l_ref[h, :, cs], e1_ref[h, :, cs] = tabs[0], tabs[1]
            rb_ref[h, :, cs], e2_ref[h, :, cs] = tabs[2].astype(rb_ref.dtype), tabs[3].astype(e2_ref.dtype)

        wd = flat.shape[1]
        for c in range(q.shape[0] // wd):
            cs = slice(c * wd, (c + 1) * wd)
            sc = [s[:, cs] for s in scores]
            tabs, ok = _peer_tables(sc, flat, exact=False)
            write(tabs, cs)

            @pl.when(jnp.min(ok) < 0.5)
            def _():
                write(_peer_tables(sc, flat, exact=True)[0], cs)
        return carry

    lax.fori_loop(0, PEER_HEADS, head, 0)


def _peer_topk(hn, wq_heads, subkeys, tm=512, wd=256):
    T, D = hn.shape
    H = wq_heads.shape[0]
    tm = min(tm, T)
    wd = min(wd, tm)
    out = jax.ShapeDtypeStruct((H, PEER_NKEYS, T), F32)
    outb = jax.ShapeDtypeStruct((H, PEER_NKEYS, T), BF16)
    oblk = pl.BlockSpec((H, PEER_NKEYS, tm), lambda i: (0, 0, i))
    return pl.pallas_call(
        _peer_topk_kernel,
        grid=(T // tm,),
        in_specs=[pl.BlockSpec((tm, D), lambda i: (i, 0)),
                  pl.BlockSpec(wq_heads.shape, lambda i: (0, 0, 0)),
                  pl.BlockSpec((H, 2, PEER_NKEYS, PEER_HALF), lambda i: (0, 0, 0, 0)),
                  pl.BlockSpec((_TRI_ROWS, wd), lambda i: (0, 0))],
        out_specs=[oblk, oblk, oblk, oblk],
        out_shape=[out, out, outb, outb],
        compiler_params=_params("parallel"),
        name="peer_topk",
    )(hn, wq_heads, subkeys, _tri_tables(wd))


def _gelu_tanh(x):
    c = math.sqrt(2.0 / math.pi)
    h = 0.5 * x
    return h + h * jnp.tanh(x * (c + (c * 0.044715) * (x * x)))


def _peer_dense_kernel(ni, hnT_ref, u_ref, vT_ref, l_ref, e1_ref, rb_ref, e2_ref, x_ref, o_ref,
                       acc_ref, act_ref, a_ref):
    j = pl.program_id(1)
    nk = PEER_NKEYS

    @pl.when(j == 0)
    def _():
        acc_ref[...] = jnp.zeros_like(acc_ref)

    tm = hnT_ref.shape[1]
    i0 = pl.multiple_of(j * ni, ni)
    zero = jnp.zeros((), BF16)
    a_ref[...] = _dot(u_ref[...], hnT_ref[...])
    for ii in range(ni):
        w = None
        for h in range(PEER_HEADS):
            lrow = jnp.broadcast_to(l_ref[h, pl.ds(i0 + ii, 1), :], (BF16_ROWS, tm)).astype(BF16)
            erow = jnp.broadcast_to(e1_ref[h, pl.ds(i0 + ii, 1), :], (BF16_ROWS, tm)).astype(BF16)
            wh = jnp.where(rb_ref[h] < lrow[None], e2_ref[h], zero) * erow[None]
            w = wh if w is None else w + wh
        a = a_ref[ii * nk:(ii + 1) * nk, :].astype(BF16)
        act_ref[ii * nk:(ii + 1) * nk, :] = _gelu_tanh(a) * w.reshape(nk, tm)
    acc_ref[...] += _dot(vT_ref[...], act_ref[...])

    @pl.when(j == pl.num_programs(1) - 1)
    def _():
        o_ref[...] = x_ref[...] + acc_ref[...].T


def _peer_dense(hnT, u_bf, vT_bf, tabs, x, tm=512, ni=16):
    D, T = hnT.shape
    NE = u_bf.shape[0]
    te = ni * PEER_NKEYS
    tm = min(tm, T)
    tab = pl.BlockSpec((PEER_HEADS, PEER_NKEYS, tm), lambda i, j: (0, 0, i))
    grp = PEER_NKEYS // BF16_ROWS
    tabb = pl.BlockSpec((PEER_HEADS, grp, BF16_ROWS, tm), lambda i, j: (0, 0, 0, i))
    lt, e1, rb, e2 = tabs
    rb, e2 = [t.reshape(PEER_HEADS, grp, BF16_ROWS, T) for t in (rb, e2)]
    return pl.pallas_call(
        functools.partial(_peer_dense_kernel, ni),
        grid=(T // tm, NE // te),
        in_specs=[pl.BlockSpec((D, tm), lambda i, j: (0, i)),
                  pl.BlockSpec((te, D), lambda i, j: (j, 0)),
                  pl.BlockSpec((D, te), lambda i, j: (0, j)),
                  tab, tab, tabb, tabb,
                  pl.BlockSpec((tm, D), lambda i, j: (i, 0))],
        out_specs=pl.BlockSpec((tm, D), lambda i, j: (i, 0)),
        out_shape=jax.ShapeDtypeStruct((T, D), F32),
        scratch_shapes=[pltpu.VMEM((D, tm), F32), pltpu.VMEM((te, tm), BF16), pltpu.VMEM((te, tm), F32)],
        compiler_params=_params("parallel", "arbitrary"),
        name="peer_experts",
    )(hnT, u_bf, vT_bf, lt, e1, rb, e2, x)


def kernel(x, norm1_g, w_in, gate_b, na_q_g, na_k_g, na_rpb, hy_conv_w, hy_conv_b, hy_w1, hy_b1, hy_freq,
           hy_w2, hy_b2, hy_w3, hy_bias, w_up_na, w_up_hy, w_out, norm2_g, peer_wq, peer_subkeys, peer_u,
           peer_v):
    B, S, D = x.shape
    assert B == 2, "the long convolution packs the two batches as one complex sequence"
    depth = w_in.shape[0]
    T = B * S
    xt = x.reshape(T, D)

    fc = _fft_constants(S)
    n1 = fc["n1"]
    pos_ext, dec_ext = _hy_pos_tables(S)
    head_sum = jnp.asarray(np.kron(np.eye(NA_HEADS), np.ones((NA_HEAD_DIM, NA_HEAD_DIM))), BF16)
    o_qk, o_v, o_hy = 2 * NA_WIDTH, 3 * NA_WIDTH, 3 * NA_WIDTH + 3 * HY_WIDTH
    tc = 2048
    ncol = FFT_N2 * HY_WIDTH

    for l in range(depth):
        (hn,) = _rmsnorm(xt, norm1_g[l])
        w = w_in[l].astype(BF16)
        qk_gain = jnp.concatenate([jnp.tile(na_q_g[l], NA_HEADS) * (NA_HEAD_DIM ** -0.5),
                                   jnp.tile(na_k_g[l], NA_HEADS)]).reshape(1, o_qk).astype(F32)
        qk = _mm(hn, w, cols=(0, o_qk), out_dtype=BF16, epi=_epi_head_rmsnorm,
                 extras=[(qk_gain, "col"), (head_sum, "full")], name="proj_qk")
        v = _mm(hn, w, cols=(o_qk, o_v), out_dtype=BF16, name="proj_v")
        hy = _mm(hn, w, cols=(o_v, o_hy), out_dtype=F32, name="proj_hy")
        gates = _mm(hn, w, cols=(o_hy, w.shape[1]), out_dtype=BF16, epi=_epi_gate,
                    extras=[(gate_b[l].reshape(1, 2 * D), "col")], tn=1024, name="proj_gates")

        a_out = _na(qk, v, _na_bias_table(na_rpb[l]), B, S)

        z, x0 = _hy_prep(hy, hy_conv_w[l], hy_conv_b[l], S)
        w1p = jnp.pad(hy_w1[l], ((0, HY_FFN_HIDDEN - HY_POS_DIM), (0, 0)))
        g = _hy_filter(S, pos_ext, dec_ext, w1p, hy_b1[l], hy_freq[l], hy_w2[l], hy_b2[l], hy_w3[l])
        gr, gi = _dft_pair(fc["ga"], fc["gb"], g.reshape(n1, ncol), tc)
        z2 = z.reshape(n1, ncol)
        ar, ai = _dft_pair(fc["za"], fc["zb"], z2, tc)
        cube = lambda t: t.reshape(n1, FFT_N2, HY_WIDTH)
        br, bi = _spec_conv(cube(ar), cube(ai), cube(gr), cube(gi), fc["wf"], fc["wi"])
        bias_t = jnp.tile(hy_bias[l], tc // HY_WIDTH).reshape(1, tc)
        b_out = _idft_out(fc["ya"], fc["yb"], br.reshape(n1, ncol), bi.reshape(n1, ncol),
                          z2, x0.reshape(n1, ncol), bias_t, tc).reshape(T, HY_WIDTH)

        merged = _merge(a_out, b_out, w_up_na[l].astype(BF16), w_up_hy[l].astype(BF16), gates)
        xt = _mm(merged, w_out[l].astype(BF16), out_dtype=F32, epi=_epi_residual,
                 extras=[(xt, "tile")], tn=1024, name="proj_out")

        hn2, hn2T = _rmsnorm(xt, norm2_g[l], transposed=True)
        wq_heads = peer_wq[l].reshape(D, PEER_HEADS, 2 * PEER_HALF).transpose(1, 0, 2).astype(BF16)
        tabs = _peer_topk(hn2, wq_heads, peer_subkeys[l])
        xt = _peer_dense(hn2T, peer_u[l].astype(BF16), peer_v[l].T.astype(BF16), tabs, xt)
    return xt.reshape(B, S, D)
```

```python
import functools
import math

import numpy as np
import jax
import jax.numpy as jnp
from jax import lax
from jax.experimental import pallas as pl
from jax.experimental.pallas import tpu as pltpu

F32 = jnp.float32
BF16 = jnp.bfloat16

GRID_W = 64
NA_HEADS = 8
NA_HEAD_DIM = 64
NA_WIDTH = NA_HEADS * NA_HEAD_DIM
NA_KR = 8
NA_KW = 16
HY_WIDTH = 512
HY_POS_BANDS = 16
HY_POS_DIM = 1 + 2 * HY_POS_BANDS
HY_FFN_HIDDEN = 64
HY_DECAY_TARGET = 1e-2
HY_FAST_PCT = 0.3
HY_SLOW_PCT = 1.5
PEER_HEADS = 8
PEER_NKEYS = 128
PEER_TOPK = 16
PEER_HALF = 128
EPS = 1e-6
NEG = -1e30

LANES = 128
VMEM_LIMIT_BYTES = 56 * 1024 * 1024
FFT_N2 = 128
BF16_ROWS = 16


def _params(*sem):
    return pltpu.CompilerParams(dimension_semantics=sem, vmem_limit_bytes=VMEM_LIMIT_BYTES)


def _split(x):
    hi = x.astype(BF16)
    lo = (x - hi.astype(F32)).astype(BF16)
    return hi, lo


def _dot(a, b):
    return jnp.dot(a, b, preferred_element_type=F32)


def _dot3(ah, al, bh, bl):
    return _dot(ah, bh) + _dot(ah, bl) + _dot(al, bh)


def _rmsnorm_kernel(x_ref, g_ref, o_ref, *maybe_ot_ref):
    x = x_ref[...]
    y = x * lax.rsqrt(jnp.mean(x * x, axis=-1, keepdims=True) + EPS)
    y = y * g_ref[...]
    o_ref[...] = y.astype(BF16)
    for ot_ref in maybe_ot_ref:
        ot_ref[...] = y.T.astype(BF16)


def _rmsnorm(x, g, transposed=False, tm=512):
    T, D = x.shape
    out_specs = [pl.BlockSpec((tm, D), lambda i: (i, 0))]
    out_shape = [jax.ShapeDtypeStruct((T, D), BF16)]
    if transposed:
        out_specs.append(pl.BlockSpec((D, tm), lambda i: (0, i)))
        out_shape.append(jax.ShapeDtypeStruct((D, T), BF16))
    return pl.pallas_call(
        _rmsnorm_kernel,
        grid=(T // tm,),
        in_specs=[pl.BlockSpec((tm, D), lambda i: (i, 0)),
                  pl.BlockSpec((1, D), lambda i: (0, 0))],
        out_specs=out_specs,
        out_shape=out_shape,
        compiler_params=_params("parallel"),
        name="rmsnorm",
    )(x, g.reshape(1, D))


def _mm_kernel(epi, n_extra, a_ref, w_ref, *rest):
    o_ref = rest[n_extra]
    acc = _dot(a_ref[...], w_ref[...])
    o_ref[...] = epi(acc, *[r[...] for r in rest[:n_extra]]).astype(o_ref.dtype)


def _mm(a, w, *, out_dtype, epi=lambda acc: acc, extras=(), tm=1024, tn=512, cols=None, name="mm"):
    M, K = a.shape
    c0, c1 = cols if cols is not None else (0, w.shape[1])
    N = c1 - c0
    tm, tn = min(tm, M), min(tn, N)
    assert c0 % tn == 0 and N % tn == 0
    j0 = c0 // tn
    out_spec = pl.BlockSpec((tm, tn), lambda i, j: (i, j))
    out_shape = jax.ShapeDtypeStruct((M, N), out_dtype)
    in_specs = [pl.BlockSpec((tm, K), lambda i, j: (i, 0)),
                pl.BlockSpec((K, tn), lambda i, j: (0, j0 + j))]
    args = [a, w]
    for arr, kind in extras:
        if kind == "col":
            in_specs.append(pl.BlockSpec((1, tn), lambda i, j: (0, j)))
        elif kind == "tile":
            in_specs.append(pl.BlockSpec((tm, tn), lambda i, j: (i, j)))
        else:
            in_specs.append(pl.BlockSpec(arr.shape, lambda i, j, nd=arr.ndim: (0,) * nd))
        args.append(arr)
    return pl.pallas_call(
        functools.partial(_mm_kernel, epi, len(extras)),
        grid=(M // tm, N // tn),
        in_specs=in_specs,
        out_specs=out_spec,
        out_shape=out_shape,
        compiler_params=_params("parallel", "parallel"),
        name=name,
    )(*args)


def _epi_head_rmsnorm(acc, gain, head_sum):
    hi, lo = _split(acc * acc)
    ms = (_dot(hi, head_sum) + _dot(lo, head_sum)) * (1.0 / NA_HEAD_DIM)
    return acc * lax.rsqrt(ms + EPS) * gain


def _epi_gate(acc, bias):
    return jax.nn.sigmoid(acc + bias)


def _epi_residual(acc, res):
    return res + acc


def _merge_kernel(a_ref, b_ref, wa_ref, wb_ref, g_ref, o_ref):
    d = o_ref.shape[1]
    ya = _dot(a_ref[...], wa_ref[...])
    yb = _dot(b_ref[...], wb_ref[...])
    g = g_ref[...].astype(F32)
    o_ref[...] = (g[:, :d] * ya + g[:, d:] * yb).astype(o_ref.dtype)


def _merge(a_out, b_out, w_na, w_hy, gates, tm=512):
    T, wa = a_out.shape
    D = w_na.shape[1]
    return pl.pallas_call(
        _merge_kernel,
        grid=(T // tm,),
        in_specs=[pl.BlockSpec((tm, wa), lambda i: (i, 0)),
                  pl.BlockSpec((tm, b_out.shape[1]), lambda i: (i, 0)),
                  pl.BlockSpec(w_na.shape, lambda i: (0, 0)),
                  pl.BlockSpec(w_hy.shape, lambda i: (0, 0)),
                  pl.BlockSpec((tm, 2 * D), lambda i: (i, 0))],
        out_specs=pl.BlockSpec((tm, D), lambda i: (i, 0)),
        out_shape=jax.ShapeDtypeStruct((T, D), BF16),
        compiler_params=_params("parallel"),
        name="merge",
    )(a_out, b_out, w_na, w_hy, gates)


def _na_kernel(rows, rb, q_ref, k_ref, v_ref, bias_ref, o_ref, s_ref, p_ref):
    blk = pl.program_id(2)
    lane = lax.broadcasted_iota(jnp.int32, (GRID_W, LANES), 1)
    nk = NA_KR * GRID_W
    starts = []
    for i in range(rb):
        r = blk * rb + i
        r0 = jnp.clip(r - NA_KR // 2, 0, rows - NA_KR)
        dr0 = r0 - r + (NA_KR - 1)
        start = pl.multiple_of(r0 * GRID_W, GRID_W)
        starts.append(start)
        qr = q_ref[i * GRID_W:(i + 1) * GRID_W, :]
        kw = k_ref[pl.ds(start, nk), :]
        for hh in range(2):
            in_head = (lane >= NA_HEAD_DIM) if hh else (lane < NA_HEAD_DIM)
            qm = jnp.where(in_head, qr, jnp.zeros_like(qr))
            s = lax.dot_general(qm, kw, (((1,), (1,)), ((), ())), preferred_element_type=F32)
            s_ref[(2 * i + hh) * GRID_W:(2 * i + hh + 1) * GRID_W, :] = s + bias_ref[hh, dr0]
    s = s_ref[...]
    p = jnp.exp(s - jnp.max(s, axis=-1, keepdims=True))
    p_ref[...] = (p * (1.0 / jnp.sum(p, axis=-1, keepdims=True))).astype(BF16)
    for i in range(rb):
        vw = v_ref[pl.ds(starts[i], nk), :]
        o0 = _dot(p_ref[(2 * i) * GRID_W:(2 * i + 1) * GRID_W, :], vw)
        o1 = _dot(p_ref[(2 * i + 1) * GRID_W:(2 * i + 2) * GRID_W, :], vw)
        o_ref[i * GRID_W:(i + 1) * GRID_W, :] = jnp.where(lane < NA_HEAD_DIM, o0, o1).astype(o_ref.dtype)


def _na_bias_table(rpb):
    c = np.arange(GRID_W)
    c0 = np.clip(c - NA_KW // 2, 0, GRID_W - NA_KW)
    col_in = (c[None, :] >= c0[:, None]) & (c[None, :] < c0[:, None] + NA_KW)
    dc_idx = np.clip(c[None, :] - c[:, None] + (NA_KW - 1), 0, 2 * NA_KW - 2)
    onehot = jnp.asarray((dc_idx[:, :, None] == np.arange(2 * NA_KW - 1)).astype(np.float32))
    rows = jnp.stack([rpb.astype(F32)[:, d:d + NA_KR] for d in range(NA_KR)], axis=1)
    b = jnp.einsum("hdjc,qkc->hdqjk", rows, onehot, precision=lax.Precision.HIGHEST)
    b = jnp.where(jnp.asarray(col_in)[None, None, :, None, :], b, NEG)
    return b.reshape(NA_HEADS, NA_KR, GRID_W, NA_KR * GRID_W)


def _na(qk, v, bias, B, S, rb=16):
    rows = S // GRID_W
    assert rows >= NA_KR and rows % rb == 0
    nblk = rows // rb
    tq = rb * GRID_W
    kofs = NA_WIDTH // LANES
    return pl.pallas_call(
        functools.partial(_na_kernel, rows, rb),
        grid=(B, NA_HEADS // 2, nblk),
        in_specs=[pl.BlockSpec((tq, LANES), lambda b, p, r: (b * nblk + r, p)),
                  pl.BlockSpec((S, LANES), lambda b, p, r: (b, kofs + p)),
                  pl.BlockSpec((S, LANES), lambda b, p, r: (b, p)),
                  pl.BlockSpec((2, NA_KR, GRID_W, NA_KR * GRID_W), lambda b, p, r: (p, 0, 0, 0))],
        out_specs=pl.BlockSpec((tq, LANES), lambda b, p, r: (b * nblk + r, p)),
        out_shape=jax.ShapeDtypeStruct((B * S, NA_WIDTH), BF16),
        scratch_shapes=[pltpu.VMEM((2 * tq, NA_KR * GRID_W), F32), pltpu.VMEM((2 * tq, NA_KR * GRID_W), BF16)],
        compiler_params=_params("parallel", "parallel", "parallel"),
        name="na_attention",
    )(qk, qk, v, bias)


def _hy_prep_kernel(S, hy_ref, prev_ref, next_ref, w_ref, b_ref, z_ref, x0_ref):
    tm, C = hy_ref.shape
    tok0 = pl.program_id(0) * tm
    h = hy_ref[...]
    row = lax.broadcasted_iota(jnp.int32, (tm, C), 0)
    prev_row = jnp.where(tok0 % S == 0, 0.0, prev_ref[7:8, :])
    next_row = jnp.where((tok0 + tm) % S == 0, 0.0, next_ref[0:1, :])
    up = jnp.where(row == 0, prev_row, pltpu.roll(h, 1, axis=0))
    dn = jnp.where(row == tm - 1, next_row, pltpu.roll(h, tm - 1, axis=0))
    u = up * w_ref[0:1, :] + h * w_ref[1:2, :] + dn * w_ref[2:3, :] + b_ref[...]
    c = C // 3
    x0_ref[...] = u[:, :c]
    z_ref[...] = u[:, 2 * c:] * u[:, c:2 * c]


def _hy_prep(hy, conv_w, conv_b, S, tm=512):
    T, C = hy.shape
    nb = tm // 8
    last = T // 8 - 1
    return pl.pallas_call(
        functools.partial(_hy_prep_kernel, S),
        grid=(T // tm,),
        in_specs=[pl.BlockSpec((tm, C), lambda i: (i, 0)),
                  pl.BlockSpec((8, C), lambda i: (jnp.maximum(i * nb - 1, 0), 0)),
                  pl.BlockSpec((8, C), lambda i: (jnp.minimum((i + 1) * nb, last), 0)),
                  pl.BlockSpec((3, C), lambda i: (0, 0)),
                  pl.BlockSpec((1, C), lambda i: (0, 0))],
        out_specs=[pl.BlockSpec((tm, C // 3), lambda i: (i, 0)),
                   pl.BlockSpec((tm, C // 3), lambda i: (i, 0))],
        out_shape=[jax.ShapeDtypeStruct((T, C // 3), F32), jax.ShapeDtypeStruct((T, C // 3), F32)],
        compiler_params=_params("parallel"),
        name="hyena_prep",
    )(hy, hy, hy, conv_w, conv_b.reshape(1, C))


def _hy_filter_kernel(L, pos_ref, dec_ref, w1_ref, b1_ref, fr_ref, w2_ref, b2_ref, w3_ref, g_ref):
    tr, C = g_ref.shape
    fr = fr_ref[...]
    w1h, w1l = _split(w1_ref[...])
    w2h, w2l = _split(w2_ref[...])
    w3h, w3l = _split(w3_ref[...])
    ph, plo = _split(pos_ref[...])
    h = jnp.sin(fr * (_dot3(ph, plo, w1h, w1l) + _dot(plo, w1l) + b1_ref[...]))
    hh, hl = _split(h)
    h = jnp.sin(fr * (_dot3(hh, hl, w2h, w2l) + _dot(hl, w2l) + b2_ref[...]))
    hh, hl = _split(h)
    h = _dot3(hh, hl, w3h, w3l) + _dot(hl, w3l)
    dec = jnp.exp(-(pos_ref[:, 0:1] * dec_ref[...]))
    hf = h[:, :C] * dec
    hb = h[:, C:] * dec
    n = pl.program_id(0) * tr + lax.broadcasted_iota(jnp.int32, (tr, C), 0)
    g = jnp.where(n < L, hf, hb)
    g = jnp.where(n == L, 0.0, g)
    g_ref[...] = jnp.where(n == 0, hf + hb, g)


def _hy_filter(L, pos_ext, dec_ext, w1p, b1, freq, w2, b2, w3, tr=1024):
    N = 2 * L
    tr = min(tr, N)
    H = HY_FFN_HIDDEN
    full = lambda shape: pl.BlockSpec(shape, lambda i: (0, 0))
    return pl.pallas_call(
        functools.partial(_hy_filter_kernel, L),
        grid=(N // tr,),
        in_specs=[pl.BlockSpec((tr, pos_ext.shape[1]), lambda i: (i, 0)),
                  full((1, HY_WIDTH)),
                  full(w1p.shape), full((1, H)), full((1, H)), full((H, H)), full((1, H)),
                  full((H, 2 * HY_WIDTH))],
        out_specs=pl.BlockSpec((tr, HY_WIDTH), lambda i: (i, 0)),
        out_shape=jax.ShapeDtypeStruct((N, HY_WIDTH), F32),
        compiler_params=_params("parallel"),
        name="hyena_filter",
    )(pos_ext, dec_ext, w1p, b1.reshape(1, H), freq.reshape(1, H), w2, b2.reshape(1, H), w3)


def _dft_pair_kernel(ma_ref, mb_ref, x_ref, or_ref, oi_ref):
    xh, xl = _split(x_ref[...])
    or_ref[...] = _dot3(ma_ref[0], ma_ref[1], xh, xl)
    oi_ref[...] = _dot3(mb_ref[0], mb_ref[1], xh, xl)


def _dft_pair(ma, mb, x, tc=2048):
    R = ma.shape[1]
    K, C = x.shape
    return pl.pallas_call(
        _dft_pair_kernel,
        grid=(C // tc,),
        in_specs=[pl.BlockSpec(ma.shape, lambda j: (0, 0, 0)),
                  pl.BlockSpec(mb.shape, lambda j: (0, 0, 0)),
                  pl.BlockSpec((K, tc), lambda j: (0, j))],
        out_specs=[pl.BlockSpec((R, tc), lambda j: (0, j)), pl.BlockSpec((R, tc), lambda j: (0, j))],
        out_shape=[jax.ShapeDtypeStruct((R, C), F32), jax.ShapeDtypeStruct((R, C), F32)],
        compiler_params=_params("parallel"),
        name="dft_outer",
    )(ma, mb, x)


def _spec_conv_kernel(ar_ref, ai_ref, fr_ref, fi_ref, twc_ref, tws_ref, wf_ref, wi_ref, br_ref, bi_ref):
    kb, n2, C = ar_ref.shape
    wfh, wfl, wih, wil = wf_ref[0], wf_ref[1], wi_ref[0], wi_ref[1]
    for k in range(kb):
        c = jnp.concatenate([twc_ref[k]] * (C // LANES), axis=1)
        s = jnp.concatenate([tws_ref[k]] * (C // LANES), axis=1)

        def inner_dft(re, im):
            h, l = _split(jnp.concatenate([re * c + im * s, im * c - re * s], axis=0))
            return _dot3(wfh, wfl, h, l)

        x = inner_dft(ar_ref[k], ai_ref[k])
        g = inner_dft(fr_ref[k], fi_ref[k])
        xr, xi, gr, gi = x[:n2], x[n2:], g[:n2], g[n2:]
        yh, yl = _split(jnp.concatenate([xr * gr - xi * gi, xr * gi + xi * gr], axis=0))
        b = _dot3(wih, wil, yh, yl)
        br, bi = b[:n2], b[n2:]
        br_ref[k] = br * c - bi * s
        bi_ref[k] = bi * c + br * s


def _spec_conv(ar, ai, fr, fi, fc, kb=4):
    n1, n2, C = ar.shape
    kb = min(kb, n1)
    blk = pl.BlockSpec((kb, n2, C), lambda k: (k, 0, 0))
    tblk = pl.BlockSpec((kb, n2, LANES), lambda k: (k, 0, 0))
    wblk = pl.BlockSpec((2, 2 * n2, 2 * n2), lambda k: (0, 0, 0))
    return pl.pallas_call(
        _spec_conv_kernel,
        grid=(n1 // kb,),
        in_specs=[blk, blk, blk, blk, tblk, tblk, wblk, wblk],
        out_specs=[blk, blk],
        out_shape=[jax.ShapeDtypeStruct((n1, n2, C), F32), jax.ShapeDtypeStruct((n1, n2, C), F32)],
        compiler_params=_params("parallel"),
        name="spectrum_conv",
    )(ar, ai, fr, fi, fc["twc"], fc["tws"], fc["wf"], fc["wi"])


def _idft_out_kernel(ma_ref, mb_ref, br_ref, bi_ref, z_ref, x0_ref, bias_ref, o_ref):
    brh, brl = _split(br_ref[...])
    bih, bil = _split(bi_ref[...])
    y = _dot3(ma_ref[0], ma_ref[1], brh, brl) + _dot3(mb_ref[0], mb_ref[1], bih, bil)
    o_ref[...] = ((y + z_ref[...] * bias_ref[...]) * x0_ref[...]).astype(o_ref.dtype)


def _idft_out(ma, mb, br, bi, z2, x02, bias_t, tc=2048):
    R = ma.shape[1]
    K, C = br.shape
    col = lambda rws: pl.BlockSpec((rws, tc), lambda j: (0, j))
    return pl.pallas_call(
        _idft_out_kernel,
        grid=(C // tc,),
        in_specs=[pl.BlockSpec(ma.shape, lambda j: (0, 0, 0)), pl.BlockSpec(mb.shape, lambda j: (0, 0, 0)),
                  col(K), col(K), col(R), col(R), pl.BlockSpec((1, tc), lambda j: (0, 0))],
        out_specs=col(R),
        out_shape=jax.ShapeDtypeStruct((R, C), BF16),
        compiler_params=_params("parallel"),
        name="idft_outer",
    )(ma, mb, br, bi, z2, x02, bias_t)


def _split_const(m):
    m32 = jnp.asarray(np.asarray(m, np.float32))
    hi, lo = _split(m32)
    return jnp.stack([hi, lo])


def _fft_constants(L):
    N = 2 * L
    n2 = FFT_N2
    n1 = N // n2
    h = n1 // 2
    k1 = np.arange(n1)[:, None].astype(np.float64)
    ang = 2.0 * np.pi * k1 * np.arange(n1)[None, :] / n1
    c, s = np.cos(ang), np.sin(ang)
    za = np.concatenate([c[:, :h], s[:, :h]], axis=1)
    zb = np.concatenate([-s[:, :h], c[:, :h]], axis=1)
    ga, gb = c, -s
    ya = np.concatenate([c[:h], s[:h]], axis=0)
    yb = np.concatenate([-s[:h], c[:h]], axis=0)
    kk = np.arange(n2)[:, None].astype(np.float64)
    nn = np.arange(n2)[None, :].astype(np.float64)
    base = 2.0 * np.pi * kk * nn / n2
    cb, sb = np.cos(base), np.sin(base)
    wf = np.block([[cb, sb], [-sb, cb]])
    wi = np.block([[cb.T, -sb.T], [sb.T, cb.T]]) / N
    tw = 2.0 * np.pi * np.arange(n1)[:, None] * nn / N
    rep = lambda m: jnp.broadcast_to(jnp.asarray(np.asarray(m, np.float32))[:, :, None], (n1, n2, LANES))
    sp = lambda m: _split_const(m)
    return dict(za=sp(za), zb=sp(zb), ga=sp(ga), gb=sp(gb), ya=sp(ya), yb=sp(yb), wf=sp(wf), wi=sp(wi),
                twc=rep(np.cos(tw)), tws=rep(np.sin(tw)), n1=n1)


def _hy_pos_tables(L):
    ext = lambda a: jnp.concatenate([a, a[0:1], jnp.flip(a[1:], axis=0)], axis=0)
    t = ext(jnp.linspace(0.0, 1.0, L, dtype=F32)[:, None])
    w = ext(2.0 * math.pi * jnp.arange(L, dtype=F32)[:, None] / L)
    f = jnp.linspace(1e-4, HY_POS_BANDS - 1, HY_POS_BANDS, dtype=F32)[None, :]
    z = jnp.concatenate([t, jnp.cos(f * w), -jnp.sin(f * w)], axis=-1)
    max_decay = math.log(HY_DECAY_TARGET) / HY_FAST_PCT
    min_decay = math.log(HY_DECAY_TARGET) / HY_SLOW_PCT
    deltas = jnp.linspace(min_decay, max_decay, HY_WIDTH, dtype=F32)
    zp = jnp.pad(z, ((0, 0), (0, HY_FFN_HIDDEN - HY_POS_DIM)))
    return zp, jnp.abs(deltas)[None, :]


_TRI_ROWS = 16 + 8 * 7 + 8


def _tri_tables(tm):
    flat = [b for b in range(16)]
    for a in range(1, 8):
        flat += [16 * a + b for b in range(8)]
    flat += [16 * a for a in range(8, 16)]
    return jnp.asarray(np.tile(np.asarray(flat, np.float32)[:, None], (1, tm)))


def _sort16_pairs():
    def merge(lo, hi, r):
        step = r * 2
        if step < hi - lo:
            yield from merge(lo, hi, step)
            yield from merge(lo + r, hi, step)
            yield from [(i, i + r) for i in range(lo + r, hi - r, step)]
        else:
            yield (lo, lo + r)

    def sort(lo, hi):
        if hi - lo >= 1:
            mid = lo + (hi - lo) // 2
            yield from sort(lo, mid)
            yield from sort(mid + 1, hi)
            yield from merge(lo, hi, 1)
    return list(sort(0, 15))


_SORT16 = _sort16_pairs()
_BITONIC16 = [(i, i + d) for d in (8, 4, 2, 1) for i in range(16) if not i & d]


def _cmpx(x, pairs):
    x = list(x)
    for i, j in pairs:
        x[i], x[j] = jnp.maximum(x[i], x[j]), jnp.minimum(x[i], x[j])
    return x


def _top16_sorted(s):
    n, tm = s.shape
    sub = n // PEER_TOPK
    x = _cmpx([s[r * sub:(r + 1) * sub] for r in range(PEER_TOPK)], _SORT16)
    shift = sub // 2
    while shift:
        y = [pltpu.roll(v, shift, axis=0) for v in x]
        x = _cmpx([jnp.maximum(x[i], y[PEER_TOPK - 1 - i]) for i in range(PEER_TOPK)], _BITONIC16)
        shift //= 2
    rowk = lax.broadcasted_iota(jnp.int32, (PEER_TOPK, tm), 0)
    vals = jnp.zeros((PEER_TOPK, tm), F32)
    for r in range(PEER_TOPK):
        vals = jnp.where(rowk == r, jnp.concatenate([x[r], x[r]], axis=0), vals)
    return vals


def _topk_rows(scores, k, exact):
    n, tm = scores[0].shape
    if not exact:
        return [(_top16_sorted(s), None) for s in scores]
    rowk = lax.broadcasted_iota(jnp.int32, (k, tm), 0)
    rowf = lax.broadcasted_iota(jnp.int32, (n, tm), 0).astype(F32)
    state = [(s, jnp.zeros((k, tm), F32), jnp.full((n, tm), float(k), F32)) for s in scores]
    for r in range(k):
        nxt = []
        for s, vals, rank in state:
            m = jnp.max(s, axis=0, keepdims=True)
            sel = rowf == jnp.min(jnp.where(s == m, rowf, float(n)), axis=0, keepdims=True)
            nxt.append((jnp.where(sel, -jnp.inf, s), jnp.where(rowk == r, m, vals),
                        jnp.where(sel, float(r), rank)))
        state = nxt
    return [(vals, rank) for _, vals, rank in state]


def _stair_cells(t1, t2, op):
    pieces = [op(t1[0:1], t2)]
    pieces += [op(t1[a:a + 1], t2[0:8]) for a in range(1, 8)]
    pieces += [op(t1[8:16], t2[0:1])]
    return jnp.concatenate(pieces, axis=0)


def _peer_tables(scores, flat, exact):
    K = PEER_TOPK
    tm = scores[0].shape[1]
    (sv1, rank1), (sv2, rank2) = _topk_rows(scores, K, exact)
    cand = _stair_cells(sv1, sv2, lambda x, y: x + y)
    if exact:
        row16 = lax.broadcasted_iota(jnp.int32, (K, tm), 0).astype(F32)
        length = jnp.zeros((K, tm), F32)
        for r in range(K):
            m = jnp.max(cand, axis=0, keepdims=True)
            f = jnp.min(jnp.where(cand == m, flat, 1e9), axis=0, keepdims=True)
            cand = jnp.where(flat == f, -jnp.inf, cand)
            length = jnp.where(row16 == jnp.floor(f * (1.0 / K)), length + 1.0, length)
        picked = cand == -jnp.inf
    else:
        for r in range(K):
            cand = jnp.where(cand == jnp.max(cand, axis=0, keepdims=True), -jnp.inf, cand)
        picked = cand == -jnp.inf
        cnt = jnp.where(picked, 1.0, 0.0)
        rows = [jnp.sum(cnt[0:16], axis=0, keepdims=True)]
        rows += [jnp.sum(cnt[8 + 8 * a:16 + 8 * a], axis=0, keepdims=True) for a in range(1, 8)]
        length = jnp.concatenate(rows + [cnt[72:80]], axis=0)
    s1, s2 = scores
    lfull = jnp.zeros_like(s1)
    if exact:
        ok = jnp.ones((1, tm), jnp.bool_)
        for r in range(K):
            lfull = jnp.where(rank1 == float(r), length[r:r + 1], lfull)
    else:
        rank2 = jnp.zeros_like(s2)
        for r in range(K):
            lfull = jnp.where(s1 == sv1[r:r + 1], length[r:r + 1], lfull)
            rank2 = jnp.where(sv2[r:r + 1] > s2, float(r + 1), rank2)
        reach = lambda s, sv: jnp.sum(jnp.where(s >= sv[K - 1:K], 1.0, 0.0), axis=0, keepdims=True)
        ok = (reach(s1, sv1) == float(K)) & (reach(s2, sv2) == float(K)) & \
             (jnp.sum(length, axis=0, keepdims=True) == float(K))
    es1 = jnp.exp(sv1 - sv1[0:1])
    es2 = jnp.exp(sv2 - sv2[0:1])
    ecand = _stair_cells(es1, es2, lambda x, y: x * y)
    z = jnp.sum(jnp.where(picked, ecand, 0.0), axis=0, keepdims=True)
    e1 = jnp.exp(scores[0] - sv1[0:1]) / z
    e2 = jnp.exp(scores[1] - sv2[0:1])
    return (lfull, e1, rank2, e2), jnp.where(ok, 1.0, 0.0)


def _peer_topk_kernel(hn_ref, wq_ref, sk_ref, flat_ref, l_ref, e1_ref, rb_ref, e2_ref):
    flat = flat_ref[...]

    def head(h, carry):
        q = _dot(hn_ref[...], wq_ref[h])
        scores = []
        for p in range(2):
            qh, ql = _split(q[:, p * PEER_HALF:(p + 1) * PEER_HALF])
            kh, kl = _split(sk_ref[h, p])
            nt = (((1,), (1,)), ((), ()))
            dg = lambda a, b: lax.dot_general(a, b, nt, preferred_element_type=F32)
            scores.append(dg(kh, qh) + dg(kh, ql) + dg(kl, qh))

        def write(tabs, cs):
            l_ref[h, :, cs], e1_ref[h, :, cs] = tabs[0], tabs[1]
            rb_ref[h, :, cs], e2_ref[h, :, cs] = tabs[2].astype(rb_ref.dtype), tabs[3].astype(e2_ref.dtype)

        wd = flat.shape[1]
        for c in range(q.shape[0] // wd):
            cs = slice(c * wd, (c + 1) * wd)
            sc = [s[:, cs] for s in scores]
            tabs, ok = _peer_tables(sc, flat, exact=False)
            write(tabs, cs)

            @pl.when(jnp.min(ok) < 0.5)
            def _():
                write(_peer_tables(sc, flat, exact=True)[0], cs)
        return carry

    lax.fori_loop(0, PEER_HEADS, head, 0)


def _peer_topk(hn, wq_heads, subkeys, tm=512, wd=256):
    T, D = hn.shape
    H = wq_heads.shape[0]
    tm = min(tm, T)
    wd = min(wd, tm)
    out = jax.ShapeDtypeStruct((H, PEER_NKEYS, T), F32)
    outb = jax.ShapeDtypeStruct((H, PEER_NKEYS, T), BF16)
    oblk = pl.BlockSpec((H, PEER_NKEYS, tm), lambda i: (0, 0, i))
    return pl.pallas_call(
        _peer_topk_kernel,
        grid=(T // tm,),
        in_specs=[pl.BlockSpec((tm, D), lambda i: (i, 0)),
                  pl.BlockSpec(wq_heads.shape, lambda i: (0, 0, 0)),
                  pl.BlockSpec((H, 2, PEER_NKEYS, PEER_HALF), lambda i: (0, 0, 0, 0)),
                  pl.BlockSpec((_TRI_ROWS, wd), lambda i: (0, 0))],
        out_specs=[oblk, oblk, oblk, oblk],
        out_shape=[out, out, outb, outb],
        compiler_params=_params("parallel"),
        name="peer_topk",
    )(hn, wq_heads, subkeys, _tri_tables(wd))


def _gelu_tanh(x):
    c = math.sqrt(2.0 / math.pi)
    h = 0.5 * x
    return h + h * jnp.tanh(x * (c + (c * 0.044715) * (x * x)))


def _peer_dense_kernel(ni, hnT_ref, u_ref, vT_ref, l_ref, e1_ref, rb_ref, e2_ref, x_ref, o_ref,
                       acc_ref, act_ref, a_ref):
    j = pl.program_id(1)
    nk = PEER_NKEYS

    @pl.when(j == 0)
    def _():
        acc_ref[...] = jnp.zeros_like(acc_ref)

    tm = hnT_ref.shape[1]
    i0 = pl.multiple_of(j * ni, ni)
    zero = jnp.zeros((), BF16)
    a_ref[...] = _dot(u_ref[...], hnT_ref[...])
    for ii in range(ni):
        w = None
        for h in range(PEER_HEADS):
            lrow = jnp.broadcast_to(l_ref[h, pl.ds(i0 + ii, 1), :], (BF16_ROWS, tm)).astype(BF16)
            erow = jnp.broadcast_to(e1_ref[h, pl.ds(i0 + ii, 1), :], (BF16_ROWS, tm)).astype(BF16)
            wh = jnp.where(rb_ref[h] < lrow[None], e2_ref[h], zero) * erow[None]
            w = wh if w is None else w + wh
        a = a_ref[ii * nk:(ii + 1) * nk, :].astype(BF16)
        act_ref[ii * nk:(ii + 1) * nk, :] = _gelu_tanh(a) * w.reshape(nk, tm)
    acc_ref[...] += _dot(vT_ref[...], act_ref[...])

    @pl.when(j == pl.num_programs(1) - 1)
    def _():
        o_ref[...] = x_ref[...] + acc_ref[...].T


def _peer_dense(hnT, u_bf, vT_bf, tabs, x, tm=512, ni=16):
    D, T = hnT.shape
    NE = u_bf.shape[0]
    te = ni * PEER_NKEYS
    tm = min(tm, T)
    tab = pl.BlockSpec((PEER_HEADS, PEER_NKEYS, tm), lambda i, j: (0, 0, i))
    grp = PEER_NKEYS // BF16_ROWS
    tabb = pl.BlockSpec((PEER_HEADS, grp, BF16_ROWS, tm), lambda i, j: (0, 0, 0, i))
    lt, e1, rb, e2 = tabs
    rb, e2 = [t.reshape(PEER_HEADS, grp, BF16_ROWS, T) for t in (rb, e2)]
    return pl.pallas_call(
        functools.partial(_peer_dense_kernel, ni),
        grid=(T // tm, NE // te),
        in_specs=[pl.BlockSpec((D, tm), lambda i, j: (0, i)),
                  pl.BlockSpec((te, D), lambda i, j: (j, 0)),
                  pl.BlockSpec((D, te), lambda i, j: (0, j)),
                  tab, tab, tabb, tabb,
                  pl.BlockSpec((tm, D), lambda i, j: (i, 0))],
        out_specs=pl.BlockSpec((tm, D), lambda i, j: (i, 0)),
        out_shape=jax.ShapeDtypeStruct((T, D), F32),
        scratch_shapes=[pltpu.VMEM((D, tm), F32), pltpu.VMEM((te, tm), BF16), pltpu.VMEM((te, tm), F32)],
        compiler_params=_params("parallel", "arbitrary"),
        name="peer_experts",
    )(hnT, u_bf, vT_bf, lt, e1, rb, e2, x)


def kernel(x, norm1_g, w_in, gate_b, na_q_g, na_k_g, na_rpb, hy_conv_w, hy_conv_b, hy_w1, hy_b1, hy_freq,
           hy_w2, hy_b2, hy_w3, hy_bias, w_up_na, w_up_hy, w_out, norm2_g, peer_wq, peer_subkeys, peer_u,
           peer_v):
    B, S, D = x.shape
    assert B == 2, "the long convolution packs the two batches as one complex sequence"
    depth = w_in.shape[0]
    T = B * S
    xt = x.reshape(T, D)

    fc = _fft_constants(S)
    n1 = fc["n1"]
    pos_ext, dec_ext = _hy_pos_tables(S)
    head_sum = jnp.asarray(np.kron(np.eye(NA_HEADS), np.ones((NA_HEAD_DIM, NA_HEAD_DIM))), BF16)
    o_qk, o_v, o_hy = 2 * NA_WIDTH, 3 * NA_WIDTH, 3 * NA_WIDTH + 3 * HY_WIDTH
    tc = 2048
    ncol = FFT_N2 * HY_WIDTH

    for l in range(depth):
        (hn,) = _rmsnorm(xt, norm1_g[l])
        w = w_in[l].astype(BF16)
        qk_gain = jnp.concatenate([jnp.tile(na_q_g[l], NA_HEADS) * (NA_HEAD_DIM ** -0.5),
                                   jnp.tile(na_k_g[l], NA_HEADS)]).reshape(1, o_qk).astype(F32)
        qk = _mm(hn, w, cols=(0, o_qk), out_dtype=BF16, epi=_epi_head_rmsnorm,
                 extras=[(qk_gain, "col"), (head_sum, "full")], name="proj_qk")
        v = _mm(hn, w, cols=(o_qk, o_v), out_dtype=BF16, name="proj_v")
        hy = _mm(hn, w, cols=(o_v, o_hy), out_dtype=F32, name="proj_hy")
        gates = _mm(hn, w, cols=(o_hy, w.shape[1]), out_dtype=BF16, epi=_epi_gate,
                    extras=[(gate_b[l].reshape(1, 2 * D), "col")], tn=1024, name="proj_gates")

        a_out = _na(qk, v, _na_bias_table(na_rpb[l]), B, S)

        z, x0 = _hy_prep(hy, hy_conv_w[l], hy_conv_b[l], S)
        w1p = jnp.pad(hy_w1[l], ((0, HY_FFN_HIDDEN - HY_POS_DIM), (0, 0)))
        g = _hy_filter(S, pos_ext, dec_ext, w1p, hy_b1[l], hy_freq[l], hy_w2[l], hy_b2[l], hy_w3[l])
        gr, gi = _dft_pair(fc["ga"], fc["gb"], g.reshape(n1, ncol), tc)
        z2 = z.reshape(n1, ncol)
        ar, ai = _dft_pair(fc["za"], fc["zb"], z2, tc)
        cube = lambda t: t.reshape(n1, FFT_N2, HY_WIDTH)
        br, bi = _spec_conv(cube(ar), cube(ai), cube(gr), cube(gi), fc)
        bias_t = jnp.tile(hy_bias[l], tc // HY_WIDTH).reshape(1, tc)
        b_out = _idft_out(fc["ya"], fc["yb"], br.reshape(n1, ncol), bi.reshape(n1, ncol),
                          z2, x0.reshape(n1, ncol), bias_t, tc).reshape(T, HY_WIDTH)

        merged = _merge(a_out, b_out, w_up_na[l].astype(BF16), w_up_hy[l].astype(BF16), gates)
        xt = _mm(merged, w_out[l].astype(BF16), out_dtype=F32, epi=_epi_residual,
                 extras=[(xt, "tile")], tn=1024, name="proj_out")

        hn2, hn2T = _rmsnorm(xt, norm2_g[l], transposed=True)
        wq_heads = peer_wq[l].reshape(D, PEER_HEADS, 2 * PEER_HALF).transpose(1, 0, 2).astype(BF16)
        tabs = _peer_topk(hn2, wq_heads, peer_subkeys[l])
        xt = _peer_dense(hn2T, peer_u[l].astype(BF16), peer_v[l].T.astype(BF16), tabs, xt)
    return xt.reshape(B, S, D)
```

```python
import functools
import math

import numpy as np
import jax
import jax.numpy as jnp
from jax import lax
from jax.experimental import pallas as pl
from jax.experimental.pallas import tpu as pltpu

F32 = jnp.float32
BF16 = jnp.bfloat16

GRID_W = 64
NA_HEADS = 8
NA_HEAD_DIM = 64
NA_WIDTH = NA_HEADS * NA_HEAD_DIM
NA_KR = 8
NA_KW = 16
HY_WIDTH = 512
HY_POS_BANDS = 16
HY_POS_DIM = 1 + 2 * HY_POS_BANDS
HY_FFN_HIDDEN = 64
HY_DECAY_TARGET = 1e-2
HY_FAST_PCT = 0.3
HY_SLOW_PCT = 1.5
PEER_HEADS = 8
PEER_NKEYS = 128
PEER_TOPK = 16
PEER_HALF = 128
EPS = 1e-6
NEG = -1e30

LANES = 128
VMEM_LIMIT_BYTES = 56 * 1024 * 1024
FFT_N2 = 128
BF16_ROWS = 16


def _params(*sem):
    return pltpu.CompilerParams(dimension_semantics=sem, vmem_limit_bytes=VMEM_LIMIT_BYTES)


def _split(x):
    hi = x.astype(BF16)
    lo = (x - hi.astype(F32)).astype(BF16)
    return hi, lo


def _dot(a, b):
    return jnp.dot(a, b, preferred_element_type=F32)


def _dot3(ah, al, bh, bl):
    return _dot(ah, bh) + _dot(ah, bl) + _dot(al, bh)


def _rmsnorm_kernel(x_ref, g_ref, o_ref, *maybe_ot_ref):
    x = x_ref[...]
    y = x * lax.rsqrt(jnp.mean(x * x, axis=-1, keepdims=True) + EPS)
    y = y * g_ref[...]
    o_ref[...] = y.astype(BF16)
    for ot_ref in maybe_ot_ref:
        ot_ref[...] = y.T.astype(BF16)


def _rmsnorm(x, g, transposed=False, tm=512):
    T, D = x.shape
    out_specs = [pl.BlockSpec((tm, D), lambda i: (i, 0))]
    out_shape = [jax.ShapeDtypeStruct((T, D), BF16)]
    if transposed:
        out_specs.append(pl.BlockSpec((D, tm), lambda i: (0, i)))
        out_shape.append(jax.ShapeDtypeStruct((D, T), BF16))
    return pl.pallas_call(
        _rmsnorm_kernel,
        grid=(T // tm,),
        in_specs=[pl.BlockSpec((tm, D), lambda i: (i, 0)),
                  pl.BlockSpec((1, D), lambda i: (0, 0))],
        out_specs=out_specs,
        out_shape=out_shape,
        compiler_params=_params("parallel"),
        name="rmsnorm",
    )(x, g.reshape(1, D))


def _epi_head_rmsnorm(acc, gain, head_sum):
    hi, lo = _split(acc * acc)
    ms = (_dot(hi, head_sum) + _dot(lo, head_sum)) * (1.0 / NA_HEAD_DIM)
    return acc * lax.rsqrt(ms + EPS) * gain


def _epi_gate(acc, bias):
    return jax.nn.sigmoid(acc + bias)


def _proj_in_kernel(jq, jv, jh, hn_ref, w_ref, gain_ref, hs_ref, gb_ref, qk_ref, v_ref, hy_ref, g_ref):
    j = pl.program_id(1)
    acc = _dot(hn_ref[...], w_ref[...])

    @pl.when(j < jq)
    def _():
        qk_ref[...] = _epi_head_rmsnorm(acc, gain_ref[...], hs_ref[...]).astype(qk_ref.dtype)

    @pl.when((j >= jq) & (j < jv))
    def _():
        v_ref[...] = acc.astype(v_ref.dtype)

    @pl.when((j >= jv) & (j < jh))
    def _():
        hy_ref[...] = acc

    @pl.when(j >= jh)
    def _():
        g_ref[...] = _epi_gate(acc, gb_ref[...]).astype(g_ref.dtype)


def _proj_in(hn, w, qk_gain, head_sum, gate_b, widths, tm=1024, tn=512):
    T, D = hn.shape
    wq, wv, wh, wg = widths
    assert all(x % tn == 0 for x in widths)
    jq, jv, jh, jn = wq // tn, (wq + wv) // tn, (wq + wv + wh) // tn, sum(widths) // tn
    rng = lambda lo, hi: (lambda i, j: (i, jnp.clip(j - lo, 0, hi - lo - 1)))
    crng = lambda lo, hi: (lambda i, j: (0, jnp.clip(j - lo, 0, hi - lo - 1)))
    blk = lambda f: pl.BlockSpec((tm, tn), f)
    return pl.pallas_call(
        functools.partial(_proj_in_kernel, jq, jv, jh),
        grid=(T // tm, jn),
        in_specs=[pl.BlockSpec((tm, D), lambda i, j: (i, 0)),
                  pl.BlockSpec((D, tn), lambda i, j: (0, j)),
                  pl.BlockSpec((1, tn), crng(0, jq)),
                  pl.BlockSpec(head_sum.shape, lambda i, j: (0, 0)),
                  pl.BlockSpec((1, tn), crng(jh, jn))],
        out_specs=[blk(rng(0, jq)), blk(rng(jq, jv)), blk(rng(jv, jh)), blk(rng(jh, jn))],
        out_shape=[jax.ShapeDtypeStruct((T, wq), BF16), jax.ShapeDtypeStruct((T, wv), BF16),
                   jax.ShapeDtypeStruct((T, wh), F32), jax.ShapeDtypeStruct((T, wg), BF16)],
        compiler_params=_params("parallel", "arbitrary"),
        name="proj_in",
    )(hn, w, qk_gain, head_sum, gate_b)


def _merge_out_kernel(a_ref, b_ref, wa_ref, wb_ref, g_ref, wo_ref, x_ref, o_ref):
    d = o_ref.shape[1]
    ya = _dot(a_ref[...], wa_ref[...])
    yb = _dot(b_ref[...], wb_ref[...])
    g = g_ref[...].astype(F32)
    merged = (g[:, :d] * ya + g[:, d:] * yb).astype(BF16)
    o_ref[...] = x_ref[...] + _dot(merged, wo_ref[...])


def _merge_out(a_out, b_out, w_na, w_hy, gates, w_o, x, tm=512):
    T, wa = a_out.shape
    D = w_na.shape[1]
    full = lambda a: pl.BlockSpec(a.shape, lambda i: (0, 0))
    return pl.pallas_call(
        _merge_out_kernel,
        grid=(T // tm,),
        in_specs=[pl.BlockSpec((tm, wa), lambda i: (i, 0)),
                  pl.BlockSpec((tm, b_out.shape[1]), lambda i: (i, 0)),
                  full(w_na), full(w_hy),
                  pl.BlockSpec((tm, 2 * D), lambda i: (i, 0)),
                  full(w_o),
                  pl.BlockSpec((tm, D), lambda i: (i, 0))],
        out_specs=pl.BlockSpec((tm, D), lambda i: (i, 0)),
        out_shape=jax.ShapeDtypeStruct((T, D), F32),
        compiler_params=_params("parallel"),
        name="merge_out",
    )(a_out, b_out, w_na, w_hy, gates, w_o, x)


def _na_kernel(rows, rb, q_ref, k_ref, v_ref, bias_ref, o_ref, s_ref, p_ref):
    blk = pl.program_id(2)
    lane = lax.broadcasted_iota(jnp.int32, (GRID_W, LANES), 1)
    nk = NA_KR * GRID_W
    starts = []
    for i in range(rb):
        r = blk * rb + i
        r0 = jnp.clip(r - NA_KR // 2, 0, rows - NA_KR)
        dr0 = r0 - r + (NA_KR - 1)
        start = pl.multiple_of(r0 * GRID_W, GRID_W)
        starts.append(start)
        qr = q_ref[i * GRID_W:(i + 1) * GRID_W, :]
        kw = k_ref[pl.ds(start, nk), :]
        for hh in range(2):
            in_head = (lane >= NA_HEAD_DIM) if hh else (lane < NA_HEAD_DIM)
            qm = jnp.where(in_head, qr, jnp.zeros_like(qr))
            s = lax.dot_general(qm, kw, (((1,), (1,)), ((), ())), preferred_element_type=F32)
            s_ref[(2 * i + hh) * GRID_W:(2 * i + hh + 1) * GRID_W, :] = s + bias_ref[hh, dr0]
    s = s_ref[...]
    p = jnp.exp(s - jnp.max(s, axis=-1, keepdims=True))
    p_ref[...] = (p * (1.0 / jnp.sum(p, axis=-1, keepdims=True))).astype(BF16)
    for i in range(rb):
        vw = v_ref[pl.ds(starts[i], nk), :]
        o0 = _dot(p_ref[(2 * i) * GRID_W:(2 * i + 1) * GRID_W, :], vw)
        o1 = _dot(p_ref[(2 * i + 1) * GRID_W:(2 * i + 2) * GRID_W, :], vw)
        o_ref[i * GRID_W:(i + 1) * GRID_W, :] = jnp.where(lane < NA_HEAD_DIM, o0, o1).astype(o_ref.dtype)


def _na_bias_table(rpb):
    c = np.arange(GRID_W)
    c0 = np.clip(c - NA_KW // 2, 0, GRID_W - NA_KW)
    col_in = (c[None, :] >= c0[:, None]) & (c[None, :] < c0[:, None] + NA_KW)
    dc_idx = np.clip(c[None, :] - c[:, None] + (NA_KW - 1), 0, 2 * NA_KW - 2)
    onehot = jnp.asarray((dc_idx[:, :, None] == np.arange(2 * NA_KW - 1)).astype(np.float32))
    rows = jnp.stack([rpb.astype(F32)[:, d:d + NA_KR] for d in range(NA_KR)], axis=1)
    b = jnp.einsum("hdjc,qkc->hdqjk", rows, onehot, precision=lax.Precision.HIGHEST)
    b = jnp.where(jnp.asarray(col_in)[None, None, :, None, :], b, NEG)
    return b.reshape(NA_HEADS, NA_KR, GRID_W, NA_KR * GRID_W)


def _na(qk, v, bias, B, S, rb=16):
    rows = S // GRID_W
    assert rows >= NA_KR and rows % rb == 0
    nblk = rows // rb
    tq = rb * GRID_W
    kofs = NA_WIDTH // LANES
    return pl.pallas_call(
        functools.partial(_na_kernel, rows, rb),
        grid=(B, NA_HEADS // 2, nblk),
        in_specs=[pl.BlockSpec((tq, LANES), lambda b, p, r: (b * nblk + r, p)),
                  pl.BlockSpec((S, LANES), lambda b, p, r: (b, kofs + p)),
                  pl.BlockSpec((S, LANES), lambda b, p, r: (b, p)),
                  pl.BlockSpec((2, NA_KR, GRID_W, NA_KR * GRID_W), lambda b, p, r: (p, 0, 0, 0))],
        out_specs=pl.BlockSpec((tq, LANES), lambda b, p, r: (b * nblk + r, p)),
        out_shape=jax.ShapeDtypeStruct((B * S, NA_WIDTH), BF16),
        scratch_shapes=[pltpu.VMEM((2 * tq, NA_KR * GRID_W), F32), pltpu.VMEM((2 * tq, NA_KR * GRID_W), BF16)],
        compiler_params=_params("parallel", "parallel", "parallel"),
        name="na_attention",
    )(qk, qk, v, bias)


def _hy_prep_kernel(S, hy_ref, prev_ref, next_ref, w_ref, b_ref, z_ref, x0_ref):
    tm, C = hy_ref.shape
    tok0 = pl.program_id(0) * tm
    h = hy_ref[...]
    row = lax.broadcasted_iota(jnp.int32, (tm, C), 0)
    prev_row = jnp.where(tok0 % S == 0, 0.0, prev_ref[7:8, :])
    next_row = jnp.where((tok0 + tm) % S == 0, 0.0, next_ref[0:1, :])
    up = jnp.where(row == 0, prev_row, pltpu.roll(h, 1, axis=0))
    dn = jnp.where(row == tm - 1, next_row, pltpu.roll(h, tm - 1, axis=0))
    u = up * w_ref[0:1, :] + h * w_ref[1:2, :] + dn * w_ref[2:3, :] + b_ref[...]
    c = C // 3
    x0_ref[...] = u[:, :c]
    z_ref[...] = u[:, 2 * c:] * u[:, c:2 * c]


def _hy_prep(hy, conv_w, conv_b, S, tm=512):
    T, C = hy.shape
    nb = tm // 8
    last = T // 8 - 1
    return pl.pallas_call(
        functools.partial(_hy_prep_kernel, S),
        grid=(T // tm,),
        in_specs=[pl.BlockSpec((tm, C), lambda i: (i, 0)),
                  pl.BlockSpec((8, C), lambda i: (jnp.maximum(i * nb - 1, 0), 0)),
                  pl.BlockSpec((8, C), lambda i: (jnp.minimum((i + 1) * nb, last), 0)),
                  pl.BlockSpec((3, C), lambda i: (0, 0)),
                  pl.BlockSpec((1, C), lambda i: (0, 0))],
        out_specs=[pl.BlockSpec((tm, C // 3), lambda i: (i, 0)),
                   pl.BlockSpec((tm, C // 3), lambda i: (i, 0))],
        out_shape=[jax.ShapeDtypeStruct((T, C // 3), F32), jax.ShapeDtypeStruct((T, C // 3), F32)],
        compiler_params=_params("parallel"),
        name="hyena_prep",
    )(hy, hy, hy, conv_w, conv_b.reshape(1, C))


def _hy_filter_kernel(L, pos_ref, dec_ref, w1_ref, b1_ref, fr_ref, w2_ref, b2_ref, w3_ref, g_ref):
    tr, C = g_ref.shape
    fr = fr_ref[...]
    w1h, w1l = _split(w1_ref[...])
    w2h, w2l = _split(w2_ref[...])
    w3h, w3l = _split(w3_ref[...])
    ph, plo = _split(pos_ref[...])
    h = jnp.sin(fr * (_dot3(ph, plo, w1h, w1l) + _dot(plo, w1l) + b1_ref[...]))
    hh, hl = _split(h)
    h = jnp.sin(fr * (_dot3(hh, hl, w2h, w2l) + _dot(hl, w2l) + b2_ref[...]))
    hh, hl = _split(h)
    h = _dot3(hh, hl, w3h, w3l) + _dot(hl, w3l)
    dec = jnp.exp(-(pos_ref[:, 0:1] * dec_ref[...]))
    hf = h[:, :C] * dec
    hb = h[:, C:] * dec
    n = pl.program_id(0) * tr + lax.broadcasted_iota(jnp.int32, (tr, C), 0)
    g = jnp.where(n < L, hf, hb)
    g = jnp.where(n == L, 0.0, g)
    g_ref[...] = jnp.where(n == 0, hf + hb, g)


def _hy_filter(L, pos_ext, dec_ext, w1p, b1, freq, w2, b2, w3, tr=1024):
    N = 2 * L
    tr = min(tr, N)
    H = HY_FFN_HIDDEN
    full = lambda shape: pl.BlockSpec(shape, lambda i: (0, 0))
    return pl.pallas_call(
        functools.partial(_hy_filter_kernel, L),
        grid=(N // tr,),
        in_specs=[pl.BlockSpec((tr, pos_ext.shape[1]), lambda i: (i, 0)),
                  full((1, HY_WIDTH)),
                  full(w1p.shape), full((1, H)), full((1, H)), full((H, H)), full((1, H)),
                  full((H, 2 * HY_WIDTH))],
        out_specs=pl.BlockSpec((tr, HY_WIDTH), lambda i: (i, 0)),
        out_shape=jax.ShapeDtypeStruct((N, HY_WIDTH), F32),
        compiler_params=_params("parallel"),
        name="hyena_filter",
    )(pos_ext, dec_ext, w1p, b1.reshape(1, H), freq.reshape(1, H), w2, b2.reshape(1, H), w3)


def _dft_pair_kernel(ma_ref, mb_ref, x_ref, or_ref, oi_ref):
    xh, xl = _split(x_ref[...])
    or_ref[...] = _dot3(ma_ref[0], ma_ref[1], xh, xl)
    oi_ref[...] = _dot3(mb_ref[0], mb_ref[1], xh, xl)


def _dft_pair(ma, mb, x, tc=2048):
    R = ma.shape[1]
    K, C = x.shape
    return pl.pallas_call(
        _dft_pair_kernel,
        grid=(C // tc,),
        in_specs=[pl.BlockSpec(ma.shape, lambda j: (0, 0, 0)),
                  pl.BlockSpec(mb.shape, lambda j: (0, 0, 0)),
                  pl.BlockSpec((K, tc), lambda j: (0, j))],
        out_specs=[pl.BlockSpec((R, tc), lambda j: (0, j)), pl.BlockSpec((R, tc), lambda j: (0, j))],
        out_shape=[jax.ShapeDtypeStruct((R, C), F32), jax.ShapeDtypeStruct((R, C), F32)],
        compiler_params=_params("parallel"),
        name="dft_outer",
    )(ma, mb, x)


def _spec_conv_kernel(ar_ref, ai_ref, fr_ref, fi_ref, twc_ref, tws_ref, wf_ref, wi_ref, br_ref, bi_ref):
    kb, n2, C = ar_ref.shape
    wfh, wfl, wih, wil = wf_ref[0], wf_ref[1], wi_ref[0], wi_ref[1]
    for k in range(kb):
        c = jnp.concatenate([twc_ref[k]] * (C // LANES), axis=1)
        s = jnp.concatenate([tws_ref[k]] * (C // LANES), axis=1)

        def inner_dft(re, im):
            h, l = _split(jnp.concatenate([re * c + im * s, im * c - re * s], axis=0))
            return _dot3(wfh, wfl, h, l)

        x = inner_dft(ar_ref[k], ai_ref[k])
        g = inner_dft(fr_ref[k], fi_ref[k])
        xr, xi, gr, gi = x[:n2], x[n2:], g[:n2], g[n2:]
        yh, yl = _split(jnp.concatenate([xr * gr - xi * gi, xr * gi + xi * gr], axis=0))
        b = _dot3(wih, wil, yh, yl)
        br, bi = b[:n2], b[n2:]
        br_ref[k] = br * c - bi * s
        bi_ref[k] = bi * c + br * s


def _spec_conv(ar, ai, fr, fi, fc, kb=4):
    n1, n2, C = ar.shape
    kb = min(kb, n1)
    blk = pl.BlockSpec((kb, n2, C), lambda k: (k, 0, 0))
    tblk = pl.BlockSpec((kb, n2, LANES), lambda k: (k, 0, 0))
    wblk = pl.BlockSpec((2, 2 * n2, 2 * n2), lambda k: (0, 0, 0))
    return pl.pallas_call(
        _spec_conv_kernel,
        grid=(n1 // kb,),
        in_specs=[blk, blk, blk, blk, tblk, tblk, wblk, wblk],
        out_specs=[blk, blk],
        out_shape=[jax.ShapeDtypeStruct((n1, n2, C), F32), jax.ShapeDtypeStruct((n1, n2, C), F32)],
        compiler_params=_params("parallel"),
        name="spectrum_conv",
    )(ar, ai, fr, fi, fc["twc"], fc["tws"], fc["wf"], fc["wi"])


def _idft_out_kernel(ma_ref, mb_ref, br_ref, bi_ref, z_ref, x0_ref, bias_ref, o_ref):
    brh, brl = _split(br_ref[...])
    bih, bil = _split(bi_ref[...])
    y = _dot3(ma_ref[0], ma_ref[1], brh, brl) + _dot3(mb_ref[0], mb_ref[1], bih, bil)
    o_ref[...] = ((y + z_ref[...] * bias_ref[...]) * x0_ref[...]).astype(o_ref.dtype)


def _idft_out(ma, mb, br, bi, z2, x02, bias_t, tc=2048):
    R = ma.shape[1]
    K, C = br.shape
    col = lambda rws: pl.BlockSpec((rws, tc), lambda j: (0, j))
    return pl.pallas_call(
        _idft_out_kernel,
        grid=(C // tc,),
        in_specs=[pl.BlockSpec(ma.shape, lambda j: (0, 0, 0)), pl.BlockSpec(mb.shape, lambda j: (0, 0, 0)),
                  col(K), col(K), col(R), col(R), pl.BlockSpec((1, tc), lambda j: (0, 0))],
        out_specs=col(R),
        out_shape=jax.ShapeDtypeStruct((R, C), BF16),
        compiler_params=_params("parallel"),
        name="idft_outer",
    )(ma, mb, br, bi, z2, x02, bias_t)


def _split_const(m):
    m32 = jnp.asarray(np.asarray(m, np.float32))
    hi, lo = _split(m32)
    return jnp.stack([hi, lo])


def _fft_constants(L):
    N = 2 * L
    n2 = FFT_N2
    n1 = N // n2
    h = n1 // 2
    k1 = np.arange(n1)[:, None].astype(np.float64)
    ang = 2.0 * np.pi * k1 * np.arange(n1)[None, :] / n1
    c, s = np.cos(ang), np.sin(ang)
    za = np.concatenate([c[:, :h], s[:, :h]], axis=1)
    zb = np.concatenate([-s[:, :h], c[:, :h]], axis=1)
    ga, gb = c, -s
    ya = np.concatenate([c[:h], s[:h]], axis=0)
    yb = np.concatenate([-s[:h], c[:h]], axis=0)
    kk = np.arange(n2)[:, None].astype(np.float64)
    nn = np.arange(n2)[None, :].astype(np.float64)
    base = 2.0 * np.pi * kk * nn / n2
    cb, sb = np.cos(base), np.sin(base)
    wf = np.block([[cb, sb], [-sb, cb]])
    wi = np.block([[cb.T, -sb.T], [sb.T, cb.T]]) / N
    tw = 2.0 * np.pi * np.arange(n1)[:, None] * nn / N
    rep = lambda m: jnp.broadcast_to(jnp.asarray(np.asarray(m, np.float32))[:, :, None], (n1, n2, LANES))
    sp = lambda m: _split_const(m)
    return dict(za=sp(za), zb=sp(zb), ga=sp(ga), gb=sp(gb), ya=sp(ya), yb=sp(yb), wf=sp(wf), wi=sp(wi),
                twc=rep(np.cos(tw)), tws=rep(np.sin(tw)), n1=n1)


def _hy_pos_tables(L):
    ext = lambda a: jnp.concatenate([a, a[0:1], jnp.flip(a[1:], axis=0)], axis=0)
    t = ext(jnp.linspace(0.0, 1.0, L, dtype=F32)[:, None])
    w = ext(2.0 * math.pi * jnp.arange(L, dtype=F32)[:, None] / L)
    f = jnp.linspace(1e-4, HY_POS_BANDS - 1, HY_POS_BANDS, dtype=F32)[None, :]
    z = jnp.concatenate([t, jnp.cos(f * w), -jnp.sin(f * w)], axis=-1)
    max_decay = math.log(HY_DECAY_TARGET) / HY_FAST_PCT
    min_decay = math.log(HY_DECAY_TARGET) / HY_SLOW_PCT
    deltas = jnp.linspace(min_decay, max_decay, HY_WIDTH, dtype=F32)
    zp = jnp.pad(z, ((0, 0), (0, HY_FFN_HIDDEN - HY_POS_DIM)))
    return zp, jnp.abs(deltas)[None, :]


_TRI_ROWS = 16 + 8 * 7 + 8


def _tri_tables(tm):
    flat = [b for b in range(16)]
    for a in range(1, 8):
        flat += [16 * a + b for b in range(8)]
    flat += [16 * a for a in range(8, 16)]
    return jnp.asarray(np.tile(np.asarray(flat, np.float32)[:, None], (1, tm)))


def _sort16_pairs():
    def merge(lo, hi, r):
        step = r * 2
        if step < hi - lo:
            yield from merge(lo, hi, step)
            yield from merge(lo + r, hi, step)
            yield from [(i, i + r) for i in range(lo + r, hi - r, step)]
        else:
            yield (lo, lo + r)

    def sort(lo, hi):
        if hi - lo >= 1:
            mid = lo + (hi - lo) // 2
            yield from sort(lo, mid)
            yield from sort(mid + 1, hi)
            yield from merge(lo, hi, 1)
    return list(sort(0, 15))


_SORT16 = _sort16_pairs()
_BITONIC16 = [(i, i + d) for d in (8, 4, 2, 1) for i in range(16) if not i & d]


def _cmpx(x, pairs):
    x = list(x)
    for i, j in pairs:
        x[i], x[j] = jnp.maximum(x[i], x[j]), jnp.minimum(x[i], x[j])
    return x


def _top16_sorted(s):
    n, tm = s.shape
    sub = n // PEER_TOPK
    x = _cmpx([s[r * sub:(r + 1) * sub] for r in range(PEER_TOPK)], _SORT16)
    shift = sub // 2
    while shift:
        y = [pltpu.roll(v, shift, axis=0) for v in x]
        x = _cmpx([jnp.maximum(x[i], y[PEER_TOPK - 1 - i]) for i in range(PEER_TOPK)], _BITONIC16)
        shift //= 2
    rowk = lax.broadcasted_iota(jnp.int32, (PEER_TOPK, tm), 0)
    vals = jnp.zeros((PEER_TOPK, tm), F32)
    for r in range(PEER_TOPK):
        vals = jnp.where(rowk == r, jnp.concatenate([x[r], x[r]], axis=0), vals)
    return vals


def _topk_rows(scores, k, exact):
    n, tm = scores[0].shape
    if not exact:
        return [(_top16_sorted(s), None) for s in scores]
    rowk = lax.broadcasted_iota(jnp.int32, (k, tm), 0)
    rowf = lax.broadcasted_iota(jnp.int32, (n, tm), 0).astype(F32)
    state = [(s, jnp.zeros((k, tm), F32), jnp.full((n, tm), float(k), F32)) for s in scores]
    for r in range(k):
        nxt = []
        for s, vals, rank in state:
            m = jnp.max(s, axis=0, keepdims=True)
            sel = rowf == jnp.min(jnp.where(s == m, rowf, float(n)), axis=0, keepdims=True)
            nxt.append((jnp.where(sel, -jnp.inf, s), jnp.where(rowk == r, m, vals),
                        jnp.where(sel, float(r), rank)))
        state = nxt
    return [(vals, rank) for _, vals, rank in state]


def _stair_cells(t1, t2, op):
    pieces = [op(t1[0:1], t2)]
    pieces += [op(t1[a:a + 1], t2[0:8]) for a in range(1, 8)]
    pieces += [op(t1[8:16], t2[0:1])]
    return jnp.concatenate(pieces, axis=0)


def _peer_tables(scores, flat, exact):
    K = PEER_TOPK
    tm = scores[0].shape[1]
    (sv1, rank1), (sv2, rank2) = _topk_rows(scores, K, exact)
    cand = _stair_cells(sv1, sv2, lambda x, y: x + y)
    if exact:
        row16 = lax.broadcasted_iota(jnp.int32, (K, tm), 0).astype(F32)
        length = jnp.zeros((K, tm), F32)
        for r in range(K):
            m = jnp.max(cand, axis=0, keepdims=True)
            f = jnp.min(jnp.where(cand == m, flat, 1e9), axis=0, keepdims=True)
            cand = jnp.where(flat == f, -jnp.inf, cand)
            length = jnp.where(row16 == jnp.floor(f * (1.0 / K)), length + 1.0, length)
        picked = cand == -jnp.inf
    else:
        for r in range(K):
            cand = jnp.where(cand == jnp.max(cand, axis=0, keepdims=True), -jnp.inf, cand)
        picked = cand == -jnp.inf
        cnt = jnp.where(picked, 1.0, 0.0)
        rows = [jnp.sum(cnt[0:16], axis=0, keepdims=True)]
        rows += [jnp.sum(cnt[8 + 8 * a:16 + 8 * a], axis=0, keepdims=True) for a in range(1, 8)]
        length = jnp.concatenate(rows + [cnt[72:80]], axis=0)
    s1, s2 = scores
    lfull = jnp.zeros_like(s1)
    if exact:
        ok = jnp.ones((1, tm), jnp.bool_)
        for r in range(K):
            lfull = jnp.where(rank1 == float(r), length[r:r + 1], lfull)
    else:
        rank2 = jnp.zeros_like(s2)
        for r in range(K):
            lfull = jnp.where(s1 == sv1[r:r + 1], length[r:r + 1], lfull)
            rank2 = jnp.where(sv2[r:r + 1] > s2, float(r + 1), rank2)
        reach = lambda s, sv: jnp.sum(jnp.where(s >= sv[K - 1:K], 1.0, 0.0), axis=0, keepdims=True)
        ok = (reach(s1, sv1) == float(K)) & (reach(s2, sv2) == float(K)) & \
             (jnp.sum(length, axis=0, keepdims=True) == float(K))
    es1 = jnp.exp(sv1 - sv1[0:1])
    es2 = jnp.exp(sv2 - sv2[0:1])
    ecand = _stair_cells(es1, es2, lambda x, y: x * y)
    z = jnp.sum(jnp.where(picked, ecand, 0.0), axis=0, keepdims=True)
    e1 = jnp.exp(scores[0] - sv1[0:1]) / z
    e2 = jnp.exp(scores[1] - sv2[0:1])
    return (lfull, e1, rank2, e2), jnp.where(ok, 1.0, 0.0)


def _peer_topk_kernel(hn_ref, wq_ref, sk_ref, flat_ref, l_ref, e1_ref, rb_ref, e2_ref):
    flat = flat_ref[...]

    def head(h, carry):
        q = _dot(hn_ref[...], wq_ref[h])
        scores = []
        for p in range(2):
            qh, ql = _split(q[:, p * PEER_HALF:(p + 1) * PEER_HALF])
            kh, kl = _split(sk_ref[h, p])
            nt = (((1,), (1,)), ((), ()))
            dg = lambda a, b: lax.dot_general(a, b, nt, preferred_element_type=F32)
            scores.append(dg(kh, qh) + dg(kh, ql) + dg(kl, qh))

        def write(tabs, cs):
            l_ref[h, :, cs], e1_ref[h, :, cs] = tabs[0], tabs[1]
            rb_ref[h, :, cs], e2_ref[h, :, cs] = tabs[2].astype(rb_ref.dtype), tabs[3].astype(e2_ref.dtype)

        wd = flat.shape[1]
        for c in range(q.shape[0] // wd):
            cs = slice(c * wd, (c + 1) * wd)
            sc = [s[:, cs] for s in scores]
            tabs, ok = _peer_tables(sc, flat, exact=False)
            write(tabs, cs)

            @pl.when(jnp.min(ok) < 0.5)
            def _():
                write(_peer_tables(sc, flat, exact=True)[0], cs)
        return carry

    lax.fori_loop(0, PEER_HEADS, head, 0)


def _peer_topk(hn, wq_heads, subkeys, tm=512, wd=256):
    T, D = hn.shape
    H = wq_heads.shape[0]
    tm = min(tm, T)
    wd = min(wd, tm)
    out = jax.ShapeDtypeStruct((H, PEER_NKEYS, T), F32)
    outb = jax.ShapeDtypeStruct((H, PEER_NKEYS, T), BF16)
    oblk = pl.BlockSpec((H, PEER_NKEYS, tm), lambda i: (0, 0, i))
    return pl.pallas_call(
        _peer_topk_kernel,
        grid=(T // tm,),
        in_specs=[pl.BlockSpec((tm, D), lambda i: (i, 0)),
                  pl.BlockSpec(wq_heads.shape, lambda i: (0, 0, 0)),
                  pl.BlockSpec((H, 2, PEER_NKEYS, PEER_HALF), lambda i: (0, 0, 0, 0)),
                  pl.BlockSpec((_TRI_ROWS, wd), lambda i: (0, 0))],
        out_specs=[oblk, oblk, oblk, oblk],
        out_shape=[out, out, outb, outb],
        compiler_params=_params("parallel"),
        name="peer_topk",
    )(hn, wq_heads, subkeys, _tri_tables(wd))


def _gelu_tanh(x):
    c = math.sqrt(2.0 / math.pi)
    h = 0.5 * x
    return h + h * jnp.tanh(x * (c + (c * 0.044715) * (x * x)))


def _peer_dense_kernel(ni, hnT_ref, u_ref, vT_ref, l_ref, e1_ref, rb_ref, e2_ref, x_ref, o_ref,
                       acc_ref, act_ref, a_ref):
    j = pl.program_id(1)
    nk = PEER_NKEYS

    @pl.when(j == 0)
    def _():
        acc_ref[...] = jnp.zeros_like(acc_ref)

    tm = hnT_ref.shape[1]
    i0 = pl.multiple_of(j * ni, ni)
    zero = jnp.zeros((), BF16)
    a_ref[...] = _dot(u_ref[...], hnT_ref[...])
    for ii in range(ni):
        w = None
        for h in range(PEER_HEADS):
            lrow = jnp.broadcast_to(l_ref[h, pl.ds(i0 + ii, 1), :], (BF16_ROWS, tm)).astype(BF16)
            erow = jnp.broadcast_to(e1_ref[h, pl.ds(i0 + ii, 1), :], (BF16_ROWS, tm)).astype(BF16)
            wh = jnp.where(rb_ref[h] < lrow[None], e2_ref[h], zero) * erow[None]
            w = wh if w is None else w + wh
        a = a_ref[ii * nk:(ii + 1) * nk, :].astype(BF16)
        act_ref[ii * nk:(ii + 1) * nk, :] = _gelu_tanh(a) * w.reshape(nk, tm)
    acc_ref[...] += _dot(vT_ref[...], act_ref[...])

    @pl.when(j == pl.num_programs(1) - 1)
    def _():
        o_ref[...] = x_ref[...] + acc_ref[...].T


def _peer_dense(hnT, u_bf, vT_bf, tabs, x, tm=512, ni=16):
    D, T = hnT.shape
    NE = u_bf.shape[0]
    te = ni * PEER_NKEYS
    tm = min(tm, T)
    tab = pl.BlockSpec((PEER_HEADS, PEER_NKEYS, tm), lambda i, j: (0, 0, i))
    grp = PEER_NKEYS // BF16_ROWS
    tabb = pl.BlockSpec((PEER_HEADS, grp, BF16_ROWS, tm), lambda i, j: (0, 0, 0, i))
    lt, e1, rb, e2 = tabs
    rb, e2 = [t.reshape(PEER_HEADS, grp, BF16_ROWS, T) for t in (rb, e2)]
    return pl.pallas_call(
        functools.partial(_peer_dense_kernel, ni),
        grid=(T // tm, NE // te),
        in_specs=[pl.BlockSpec((D, tm), lambda i, j: (0, i)),
                  pl.BlockSpec((te, D), lambda i, j: (j, 0)),
                  pl.BlockSpec((D, te), lambda i, j: (0, j)),
                  tab, tab, tabb, tabb,
                  pl.BlockSpec((tm, D), lambda i, j: (i, 0))],
        out_specs=pl.BlockSpec((tm, D), lambda i, j: (i, 0)),
        out_shape=jax.ShapeDtypeStruct((T, D), F32),
        scratch_shapes=[pltpu.VMEM((D, tm), F32), pltpu.VMEM((te, tm), BF16), pltpu.VMEM((te, tm), F32)],
        compiler_params=_params("parallel", "arbitrary"),
        name="peer_experts",
    )(hnT, u_bf, vT_bf, lt, e1, rb, e2, x)


def kernel(x, norm1_g, w_in, gate_b, na_q_g, na_k_g, na_rpb, hy_conv_w, hy_conv_b, hy_w1, hy_b1, hy_freq,
           hy_w2, hy_b2, hy_w3, hy_bias, w_up_na, w_up_hy, w_out, norm2_g, peer_wq, peer_subkeys, peer_u,
           peer_v):
    B, S, D = x.shape
    assert B == 2, "the long convolution packs the two batches as one complex sequence"
    depth = w_in.shape[0]
    T = B * S
    xt = x.reshape(T, D)

    fc = _fft_constants(S)
    n1 = fc["n1"]
    pos_ext, dec_ext = _hy_pos_tables(S)
    head_sum = jnp.asarray(np.kron(np.eye(NA_HEADS), np.ones((NA_HEAD_DIM, NA_HEAD_DIM))), BF16)
    o_qk, o_v, o_hy = 2 * NA_WIDTH, 3 * NA_WIDTH, 3 * NA_WIDTH + 3 * HY_WIDTH
    tc = 2048
    ncol = FFT_N2 * HY_WIDTH

    for l in range(depth):
        (hn,) = _rmsnorm(xt, norm1_g[l])
        w = w_in[l].astype(BF16)
        qk_gain = jnp.concatenate([jnp.tile(na_q_g[l], NA_HEADS) * (NA_HEAD_DIM ** -0.5),
                                   jnp.tile(na_k_g[l], NA_HEADS)]).reshape(1, o_qk).astype(F32)
        qk, v, hy, gates = _proj_in(hn, w, qk_gain, head_sum, gate_b[l].reshape(1, 2 * D),
                                    (o_qk, o_v - o_qk, o_hy - o_v, w.shape[1] - o_hy))

        a_out = _na(qk, v, _na_bias_table(na_rpb[l]), B, S)

        z, x0 = _hy_prep(hy, hy_conv_w[l], hy_conv_b[l], S)
        w1p = jnp.pad(hy_w1[l], ((0, HY_FFN_HIDDEN - HY_POS_DIM), (0, 0)))
        g = _hy_filter(S, pos_ext, dec_ext, w1p, hy_b1[l], hy_freq[l], hy_w2[l], hy_b2[l], hy_w3[l])
        gr, gi = _dft_pair(fc["ga"], fc["gb"], g.reshape(n1, ncol), tc)
        z2 = z.reshape(n1, ncol)
        ar, ai = _dft_pair(fc["za"], fc["zb"], z2, tc)
        cube = lambda t: t.reshape(n1, FFT_N2, HY_WIDTH)
        br, bi = _spec_conv(cube(ar), cube(ai), cube(gr), cube(gi), fc)
        bias_t = jnp.tile(hy_bias[l], tc // HY_WIDTH).reshape(1, tc)
        b_out = _idft_out(fc["ya"], fc["yb"], br.reshape(n1, ncol), bi.reshape(n1, ncol),
                          z2, x0.reshape(n1, ncol), bias_t, tc).reshape(T, HY_WIDTH)

        xt = _merge_out(a_out, b_out, w_up_na[l].astype(BF16), w_up_hy[l].astype(BF16), gates,
                        w_out[l].astype(BF16), xt)

        hn2, hn2T = _rmsnorm(xt, norm2_g[l], transposed=True)
        wq_heads = peer_wq[l].reshape(D, PEER_HEADS, 2 * PEER_HALF).transpose(1, 0, 2).astype(BF16)
        tabs = _peer_topk(hn2, wq_heads, peer_subkeys[l])
        xt = _peer_dense(hn2T, peer_u[l].astype(BF16), peer_v[l].T.astype(BF16), tabs, xt)
    return xt.reshape(B, S, D)
```

```python
import functools
import math

import numpy as np
import jax
import jax.numpy as jnp
from jax import lax
from jax.experimental import pallas as pl
from jax.experimental.pallas import tpu as pltpu

F32 = jnp.float32
BF16 = jnp.bfloat16

GRID_W = 64
NA_HEADS = 8
NA_HEAD_DIM = 64
NA_WIDTH = NA_HEADS * NA_HEAD_DIM
NA_KR = 8
NA_KW = 16
HY_WIDTH = 512
HY_POS_BANDS = 16
HY_POS_DIM = 1 + 2 * HY_POS_BANDS
HY_FFN_HIDDEN = 64
HY_DECAY_TARGET = 1e-2
HY_FAST_PCT = 0.3
HY_SLOW_PCT = 1.5
PEER_HEADS = 8
PEER_NKEYS = 128
PEER_TOPK = 16
PEER_HALF = 128
EPS = 1e-6
NEG = -1e30

LANES = 128
VMEM_LIMIT_BYTES = 56 * 1024 * 1024
FFT_N2 = 128
BF16_ROWS = 16


def _params(*sem):
    return pltpu.CompilerParams(dimension_semantics=sem, vmem_limit_bytes=VMEM_LIMIT_BYTES)


def _split(x):
    hi = x.astype(BF16)
    lo = (x - hi.astype(F32)).astype(BF16)
    return hi, lo


def _dot(a, b):
    return jnp.dot(a, b, preferred_element_type=F32)


def _dot3(ah, al, bh, bl):
    return _dot(ah, bh) + _dot(ah, bl) + _dot(al, bh)


def _rmsnorm_kernel(x_ref, g_ref, o_ref, *maybe_ot_ref):
    x = x_ref[...]
    y = x * lax.rsqrt(jnp.mean(x * x, axis=-1, keepdims=True) + EPS)
    y = y * g_ref[...]
    o_ref[...] = y.astype(BF16)
    for ot_ref in maybe_ot_ref:
        ot_ref[...] = y.T.astype(BF16)


def _rmsnorm(x, g, transposed=False, tm=512):
    T, D = x.shape
    out_specs = [pl.BlockSpec((tm, D), lambda i: (i, 0))]
    out_shape = [jax.ShapeDtypeStruct((T, D), BF16)]
    if transposed:
        out_specs.append(pl.BlockSpec((D, tm), lambda i: (0, i)))
        out_shape.append(jax.ShapeDtypeStruct((D, T), BF16))
    return pl.pallas_call(
        _rmsnorm_kernel,
        grid=(T // tm,),
        in_specs=[pl.BlockSpec((tm, D), lambda i: (i, 0)),
                  pl.BlockSpec((1, D), lambda i: (0, 0))],
        out_specs=out_specs,
        out_shape=out_shape,
        compiler_params=_params("parallel"),
        name="rmsnorm",
    )(x, g.reshape(1, D))


def _epi_head_rmsnorm(acc, gain, head_sum):
    hi, lo = _split(acc * acc)
    ms = (_dot(hi, head_sum) + _dot(lo, head_sum)) * (1.0 / NA_HEAD_DIM)
    return acc * lax.rsqrt(ms + EPS) * gain


def _epi_gate(acc, bias):
    return jax.nn.sigmoid(acc + bias)


def _proj_in_kernel(jq, jv, jh, hn_ref, w_ref, gain_ref, hs_ref, gb_ref, qk_ref, v_ref, hy_ref, g_ref):
    j = pl.program_id(1)
    acc = _dot(hn_ref[...], w_ref[...])

    @pl.when(j < jq)
    def _():
        qk_ref[...] = _epi_head_rmsnorm(acc, gain_ref[...], hs_ref[...]).astype(qk_ref.dtype)

    @pl.when((j >= jq) & (j < jv))
    def _():
        v_ref[...] = acc.astype(v_ref.dtype)

    @pl.when((j >= jv) & (j < jh))
    def _():
        hy_ref[...] = acc

    @pl.when(j >= jh)
    def _():
        g_ref[...] = _epi_gate(acc, gb_ref[...]).astype(g_ref.dtype)


def _proj_in(hn, w, qk_gain, head_sum, gate_b, widths, tm=2048, tn=512):
    T, D = hn.shape
    tm = min(tm, T)
    wq, wv, wh, wg = widths
    assert all(x % tn == 0 for x in widths)
    jq, jv, jh, jn = wq // tn, (wq + wv) // tn, (wq + wv + wh) // tn, sum(widths) // tn
    rng = lambda lo, hi: (lambda i, j: (i, jnp.clip(j - lo, 0, hi - lo - 1)))
    crng = lambda lo, hi: (lambda i, j: (0, jnp.clip(j - lo, 0, hi - lo - 1)))
    blk = lambda f: pl.BlockSpec((tm, tn), f)
    return pl.pallas_call(
        functools.partial(_proj_in_kernel, jq, jv, jh),
        grid=(T // tm, jn),
        in_specs=[pl.BlockSpec((tm, D), lambda i, j: (i, 0)),
                  pl.BlockSpec((D, tn), lambda i, j: (0, j)),
                  pl.BlockSpec((1, tn), crng(0, jq)),
                  pl.BlockSpec(head_sum.shape, lambda i, j: (0, 0)),
                  pl.BlockSpec((1, tn), crng(jh, jn))],
        out_specs=[blk(rng(0, jq)), blk(rng(jq, jv)), blk(rng(jv, jh)), blk(rng(jh, jn))],
        out_shape=[jax.ShapeDtypeStruct((T, wq), BF16), jax.ShapeDtypeStruct((T, wv), BF16),
                   jax.ShapeDtypeStruct((T, wh), F32), jax.ShapeDtypeStruct((T, wg), BF16)],
        compiler_params=_params("parallel", "arbitrary"),
        name="proj_in",
    )(hn, w, qk_gain, head_sum, gate_b)


def _merge_out_kernel(a_ref, b_ref, wa_ref, wb_ref, g_ref, wo_ref, x_ref, o_ref):
    d = o_ref.shape[1]
    ya = _dot(a_ref[...], wa_ref[...])
    yb = _dot(b_ref[...], wb_ref[...])
    g = g_ref[...].astype(F32)
    merged = (g[:, :d] * ya + g[:, d:] * yb).astype(BF16)
    o_ref[...] = x_ref[...] + _dot(merged, wo_ref[...])


def _merge_out(a_out, b_out, w_na, w_hy, gates, w_o, x, tm=512):
    T, wa = a_out.shape
    D = w_na.shape[1]
    full = lambda a: pl.BlockSpec(a.shape, lambda i: (0, 0))
    return pl.pallas_call(
        _merge_out_kernel,
        grid=(T // tm,),
        in_specs=[pl.BlockSpec((tm, wa), lambda i: (i, 0)),
                  pl.BlockSpec((tm, b_out.shape[1]), lambda i: (i, 0)),
                  full(w_na), full(w_hy),
                  pl.BlockSpec((tm, 2 * D), lambda i: (i, 0)),
                  full(w_o),
                  pl.BlockSpec((tm, D), lambda i: (i, 0))],
        out_specs=pl.BlockSpec((tm, D), lambda i: (i, 0)),
        out_shape=jax.ShapeDtypeStruct((T, D), F32),
        compiler_params=_params("parallel"),
        name="merge_out",
    )(a_out, b_out, w_na, w_hy, gates, w_o, x)


def _na_kernel(rows, rb, q_ref, k_ref, v_ref, bias_ref, o_ref, s_ref, p_ref):
    blk = pl.program_id(2)
    lane = lax.broadcasted_iota(jnp.int32, (GRID_W, LANES), 1)
    nk = NA_KR * GRID_W
    starts = []
    for i in range(rb):
        r = blk * rb + i
        r0 = jnp.clip(r - NA_KR // 2, 0, rows - NA_KR)
        dr0 = r0 - r + (NA_KR - 1)
        start = pl.multiple_of(r0 * GRID_W, GRID_W)
        starts.append(start)
        qr = q_ref[i * GRID_W:(i + 1) * GRID_W, :]
        kw = k_ref[pl.ds(start, nk), :]
        for hh in range(2):
            in_head = (lane >= NA_HEAD_DIM) if hh else (lane < NA_HEAD_DIM)
            qm = jnp.where(in_head, qr, jnp.zeros_like(qr))
            s = lax.dot_general(qm, kw, (((1,), (1,)), ((), ())), preferred_element_type=F32)
            s_ref[(2 * i + hh) * GRID_W:(2 * i + hh + 1) * GRID_W, :] = s + bias_ref[hh, dr0]
    s = s_ref[...]
    p = jnp.exp(s - jnp.max(s, axis=-1, keepdims=True))
    p_ref[...] = (p * (1.0 / jnp.sum(p, axis=-1, keepdims=True))).astype(BF16)
    for i in range(rb):
        vw = v_ref[pl.ds(starts[i], nk), :]
        o0 = _dot(p_ref[(2 * i) * GRID_W:(2 * i + 1) * GRID_W, :], vw)
        o1 = _dot(p_ref[(2 * i + 1) * GRID_W:(2 * i + 2) * GRID_W, :], vw)
        o_ref[i * GRID_W:(i + 1) * GRID_W, :] = jnp.where(lane < NA_HEAD_DIM, o0, o1).astype(o_ref.dtype)


def _na_bias_table(rpb):
    c = np.arange(GRID_W)
    c0 = np.clip(c - NA_KW // 2, 0, GRID_W - NA_KW)
    col_in = (c[None, :] >= c0[:, None]) & (c[None, :] < c0[:, None] + NA_KW)
    dc_idx = np.clip(c[None, :] - c[:, None] + (NA_KW - 1), 0, 2 * NA_KW - 2)
    onehot = jnp.asarray((dc_idx[:, :, None] == np.arange(2 * NA_KW - 1)).astype(np.float32))
    rows = jnp.stack([rpb.astype(F32)[:, d:d + NA_KR] for d in range(NA_KR)], axis=1)
    b = jnp.einsum("hdjc,qkc->hdqjk", rows, onehot, precision=lax.Precision.HIGHEST)
    b = jnp.where(jnp.asarray(col_in)[None, None, :, None, :], b, NEG)
    return b.reshape(NA_HEADS, NA_KR, GRID_W, NA_KR * GRID_W)


def _na(qk, v, bias, B, S, rb=16):
    rows = S // GRID_W
    assert rows >= NA_KR and rows % rb == 0
    nblk = rows // rb
    tq = rb * GRID_W
    kofs = NA_WIDTH // LANES
    return pl.pallas_call(
        functools.partial(_na_kernel, rows, rb),
        grid=(B, NA_HEADS // 2, nblk),
        in_specs=[pl.BlockSpec((tq, LANES), lambda b, p, r: (b * nblk + r, p)),
                  pl.BlockSpec((S, LANES), lambda b, p, r: (b, kofs + p)),
                  pl.BlockSpec((S, LANES), lambda b, p, r: (b, p)),
                  pl.BlockSpec((2, NA_KR, GRID_W, NA_KR * GRID_W), lambda b, p, r: (p, 0, 0, 0))],
        out_specs=pl.BlockSpec((tq, LANES), lambda b, p, r: (b * nblk + r, p)),
        out_shape=jax.ShapeDtypeStruct((B * S, NA_WIDTH), BF16),
        scratch_shapes=[pltpu.VMEM((2 * tq, NA_KR * GRID_W), F32), pltpu.VMEM((2 * tq, NA_KR * GRID_W), BF16)],
        compiler_params=_params("parallel", "parallel", "parallel"),
        name="na_attention",
    )(qk, qk, v, bias)


def _hy_prep_kernel(S, hy_ref, prev_ref, next_ref, w_ref, b_ref, z_ref, x0_ref):
    tm, C = hy_ref.shape
    tok0 = pl.program_id(0) * tm
    h = hy_ref[...]
    row = lax.broadcasted_iota(jnp.int32, (tm, C), 0)
    prev_row = jnp.where(tok0 % S == 0, 0.0, prev_ref[7:8, :])
    next_row = jnp.where((tok0 + tm) % S == 0, 0.0, next_ref[0:1, :])
    up = jnp.where(row == 0, prev_row, pltpu.roll(h, 1, axis=0))
    dn = jnp.where(row == tm - 1, next_row, pltpu.roll(h, tm - 1, axis=0))
    u = up * w_ref[0:1, :] + h * w_ref[1:2, :] + dn * w_ref[2:3, :] + b_ref[...]
    c = C // 3
    x0_ref[...] = u[:, :c]
    z_ref[...] = u[:, 2 * c:] * u[:, c:2 * c]


def _hy_prep(hy, conv_w, conv_b, S, tm=512):
    T, C = hy.shape
    nb = tm // 8
    last = T // 8 - 1
    return pl.pallas_call(
        functools.partial(_hy_prep_kernel, S),
        grid=(T // tm,),
        in_specs=[pl.BlockSpec((tm, C), lambda i: (i, 0)),
                  pl.BlockSpec((8, C), lambda i: (jnp.maximum(i * nb - 1, 0), 0)),
                  pl.BlockSpec((8, C), lambda i: (jnp.minimum((i + 1) * nb, last), 0)),
                  pl.BlockSpec((3, C), lambda i: (0, 0)),
                  pl.BlockSpec((1, C), lambda i: (0, 0))],
        out_specs=[pl.BlockSpec((tm, C // 3), lambda i: (i, 0)),
                   pl.BlockSpec((tm, C // 3), lambda i: (i, 0))],
        out_shape=[jax.ShapeDtypeStruct((T, C // 3), F32), jax.ShapeDtypeStruct((T, C // 3), F32)],
        compiler_params=_params("parallel"),
        name="hyena_prep",
    )(hy, hy, hy, conv_w, conv_b.reshape(1, C))


def _hy_filter_kernel(L, pos_ref, dec_ref, w1_ref, b1_ref, fr_ref, w2_ref, b2_ref, w3_ref, g_ref):
    tr, C = g_ref.shape
    fr = fr_ref[...]
    w1h, w1l = _split(w1_ref[...])
    w2h, w2l = _split(w2_ref[...])
    w3h, w3l = _split(w3_ref[...])
    ph, plo = _split(pos_ref[...])
    h = jnp.sin(fr * (_dot3(ph, plo, w1h, w1l) + _dot(plo, w1l) + b1_ref[...]))
    hh, hl = _split(h)
    h = jnp.sin(fr * (_dot3(hh, hl, w2h, w2l) + _dot(hl, w2l) + b2_ref[...]))
    hh, hl = _split(h)
    h = _dot3(hh, hl, w3h, w3l) + _dot(hl, w3l)
    dec = jnp.exp(-(pos_ref[:, 0:1] * dec_ref[...]))
    hf = h[:, :C] * dec
    hb = h[:, C:] * dec
    n = pl.program_id(0) * tr + lax.broadcasted_iota(jnp.int32, (tr, C), 0)
    g = jnp.where(n < L, hf, hb)
    g = jnp.where(n == L, 0.0, g)
    g_ref[...] = jnp.where(n == 0, hf + hb, g)


def _hy_filter(L, pos_ext, dec_ext, w1p, b1, freq, w2, b2, w3, tr=1024):
    N = 2 * L
    tr = min(tr, N)
    H = HY_FFN_HIDDEN
    full = lambda shape: pl.BlockSpec(shape, lambda i: (0, 0))
    return pl.pallas_call(
        functools.partial(_hy_filter_kernel, L),
        grid=(N // tr,),
        in_specs=[pl.BlockSpec((tr, pos_ext.shape[1]), lambda i: (i, 0)),
                  full((1, HY_WIDTH)),
                  full(w1p.shape), full((1, H)), full((1, H)), full((H, H)), full((1, H)),
                  full((H, 2 * HY_WIDTH))],
        out_specs=pl.BlockSpec((tr, HY_WIDTH), lambda i: (i, 0)),
        out_shape=jax.ShapeDtypeStruct((N, HY_WIDTH), F32),
        compiler_params=_params("parallel"),
        name="hyena_filter",
    )(pos_ext, dec_ext, w1p, b1.reshape(1, H), freq.reshape(1, H), w2, b2.reshape(1, H), w3)


def _dft_pair_kernel(ma_ref, mb_ref, x_ref, or_ref, oi_ref):
    xh, xl = _split(x_ref[...])
    or_ref[...] = _dot3(ma_ref[0], ma_ref[1], xh, xl)
    oi_ref[...] = _dot3(mb_ref[0], mb_ref[1], xh, xl)


def _dft_pair(ma, mb, x, tc=2048):
    R = ma.shape[1]
    K, C = x.shape
    return pl.pallas_call(
        _dft_pair_kernel,
        grid=(C // tc,),
        in_specs=[pl.BlockSpec(ma.shape, lambda j: (0, 0, 0)),
                  pl.BlockSpec(mb.shape, lambda j: (0, 0, 0)),
                  pl.BlockSpec((K, tc), lambda j: (0, j))],
        out_specs=[pl.BlockSpec((R, tc), lambda j: (0, j)), pl.BlockSpec((R, tc), lambda j: (0, j))],
        out_shape=[jax.ShapeDtypeStruct((R, C), F32), jax.ShapeDtypeStruct((R, C), F32)],
        compiler_params=_params("parallel"),
        name="dft_outer",
    )(ma, mb, x)


def _spec_conv_kernel(ar_ref, ai_ref, fr_ref, fi_ref, twc_ref, tws_ref, wf_ref, wi_ref, br_ref, bi_ref):
    kb, n2, C = ar_ref.shape
    wfh, wfl, wih, wil = wf_ref[0], wf_ref[1], wi_ref[0], wi_ref[1]
    for k in range(kb):
        c = jnp.concatenate([twc_ref[k]] * (C // LANES), axis=1)
        s = jnp.concatenate([tws_ref[k]] * (C // LANES), axis=1)

        def inner_dft(re, im):
            h, l = _split(jnp.concatenate([re * c + im * s, im * c - re * s], axis=0))
            return _dot3(wfh, wfl, h, l)

        x = inner_dft(ar_ref[k], ai_ref[k])
        g = inner_dft(fr_ref[k], fi_ref[k])
        xr, xi, gr, gi = x[:n2], x[n2:], g[:n2], g[n2:]
        yh, yl = _split(jnp.concatenate([xr * gr - xi * gi, xr * gi + xi * gr], axis=0))
        b = _dot3(wih, wil, yh, yl)
        br, bi = b[:n2], b[n2:]
        br_ref[k] = br * c - bi * s
        bi_ref[k] = bi * c + br * s


def _spec_conv(ar, ai, fr, fi, fc, kb=4):
    n1, n2, C = ar.shape
    kb = min(kb, n1)
    blk = pl.BlockSpec((kb, n2, C), lambda k: (k, 0, 0))
    tblk = pl.BlockSpec((kb, n2, LANES), lambda k: (k, 0, 0))
    wblk = pl.BlockSpec((2, 2 * n2, 2 * n2), lambda k: (0, 0, 0))
    return pl.pallas_call(
        _spec_conv_kernel,
        grid=(n1 // kb,),
        in_specs=[blk, blk, blk, blk, tblk, tblk, wblk, wblk],
        out_specs=[blk, blk],
        out_shape=[jax.ShapeDtypeStruct((n1, n2, C), F32), jax.ShapeDtypeStruct((n1, n2, C), F32)],
        compiler_params=_params("parallel"),
        name="spectrum_conv",
    )(ar, ai, fr, fi, fc["twc"], fc["tws"], fc["wf"], fc["wi"])


def _idft_out_kernel(ma_ref, mb_ref, br_ref, bi_ref, z_ref, x0_ref, bias_ref, o_ref):
    brh, brl = _split(br_ref[...])
    bih, bil = _split(bi_ref[...])
    y = _dot3(ma_ref[0], ma_ref[1], brh, brl) + _dot3(mb_ref[0], mb_ref[1], bih, bil)
    o_ref[...] = ((y + z_ref[...] * bias_ref[...]) * x0_ref[...]).astype(o_ref.dtype)


def _idft_out(ma, mb, br, bi, z2, x02, bias_t, tc=2048):
    R = ma.shape[1]
    K, C = br.shape
    col = lambda rws: pl.BlockSpec((rws, tc), lambda j: (0, j))
    return pl.pallas_call(
        _idft_out_kernel,
        grid=(C // tc,),
        in_specs=[pl.BlockSpec(ma.shape, lambda j: (0, 0, 0)), pl.BlockSpec(mb.shape, lambda j: (0, 0, 0)),
                  col(K), col(K), col(R), col(R), pl.BlockSpec((1, tc), lambda j: (0, 0))],
        out_specs=col(R),
        out_shape=jax.ShapeDtypeStruct((R, C), BF16),
        compiler_params=_params("parallel"),
        name="idft_outer",
    )(ma, mb, br, bi, z2, x02, bias_t)


def _split_const(m):
    m32 = jnp.asarray(np.asarray(m, np.float32))
    hi, lo = _split(m32)
    return jnp.stack([hi, lo])


def _fft_constants(L):
    N = 2 * L
    n2 = FFT_N2
    n1 = N // n2
    h = n1 // 2
    k1 = np.arange(n1)[:, None].astype(np.float64)
    ang = 2.0 * np.pi * k1 * np.arange(n1)[None, :] / n1
    c, s = np.cos(ang), np.sin(ang)
    za = np.concatenate([c[:, :h], s[:, :h]], axis=1)
    zb = np.concatenate([-s[:, :h], c[:, :h]], axis=1)
    ga, gb = c, -s
    ya = np.concatenate([c[:h], s[:h]], axis=0)
    yb = np.concatenate([-s[:h], c[:h]], axis=0)
    kk = np.arange(n2)[:, None].astype(np.float64)
    nn = np.arange(n2)[None, :].astype(np.float64)
    base = 2.0 * np.pi * kk * nn / n2
    cb, sb = np.cos(base), np.sin(base)
    wf = np.block([[cb, sb], [-sb, cb]])
    wi = np.block([[cb.T, -sb.T], [sb.T, cb.T]]) / N
    tw = 2.0 * np.pi * np.arange(n1)[:, None] * nn / N
    rep = lambda m: jnp.broadcast_to(jnp.asarray(np.asarray(m, np.float32))[:, :, None], (n1, n2, LANES))
    sp = lambda m: _split_const(m)
    return dict(za=sp(za), zb=sp(zb), ga=sp(ga), gb=sp(gb), ya=sp(ya), yb=sp(yb), wf=sp(wf), wi=sp(wi),
                twc=rep(np.cos(tw)), tws=rep(np.sin(tw)), n1=n1)


def _hy_pos_tables(L):
    ext = lambda a: jnp.concatenate([a, a[0:1], jnp.flip(a[1:], axis=0)], axis=0)
    t = ext(jnp.linspace(0.0, 1.0, L, dtype=F32)[:, None])
    w = ext(2.0 * math.pi * jnp.arange(L, dtype=F32)[:, None] / L)
    f = jnp.linspace(1e-4, HY_POS_BANDS - 1, HY_POS_BANDS, dtype=F32)[None, :]
    z = jnp.concatenate([t, jnp.cos(f * w), -jnp.sin(f * w)], axis=-1)
    max_decay = math.log(HY_DECAY_TARGET) / HY_FAST_PCT
    min_decay = math.log(HY_DECAY_TARGET) / HY_SLOW_PCT
    deltas = jnp.linspace(min_decay, max_decay, HY_WIDTH, dtype=F32)
    zp = jnp.pad(z, ((0, 0), (0, HY_FFN_HIDDEN - HY_POS_DIM)))
    return zp, jnp.abs(deltas)[None, :]


_TRI_ROWS = 16 + 8 * 7 + 8


def _tri_tables(tm):
    flat = [b for b in range(16)]
    for a in range(1, 8):
        flat += [16 * a + b for b in range(8)]
    flat += [16 * a for a in range(8, 16)]
    return jnp.asarray(np.tile(np.asarray(flat, np.float32)[:, None], (1, tm)))


def _sort16_pairs():
    def merge(lo, hi, r):
        step = r * 2
        if step < hi - lo:
            yield from merge(lo, hi, step)
            yield from merge(lo + r, hi, step)
            yield from [(i, i + r) for i in range(lo + r, hi - r, step)]
        else:
            yield (lo, lo + r)

    def sort(lo, hi):
        if hi - lo >= 1:
            mid = lo + (hi - lo) // 2
            yield from sort(lo, mid)
            yield from sort(mid + 1, hi)
            yield from merge(lo, hi, 1)
    return list(sort(0, 15))


_SORT16 = _sort16_pairs()
_BITONIC16 = [(i, i + d) for d in (8, 4, 2, 1) for i in range(16) if not i & d]


def _cmpx(x, pairs):
    x = list(x)
    for i, j in pairs:
        x[i], x[j] = jnp.maximum(x[i], x[j]), jnp.minimum(x[i], x[j])
    return x


def _top16_sorted(s):
    n, tm = s.shape
    sub = n // PEER_TOPK
    x = _cmpx([s[r * sub:(r + 1) * sub] for r in range(PEER_TOPK)], _SORT16)
    shift = sub // 2
    while shift:
        y = [pltpu.roll(v, shift, axis=0) for v in x]
        x = _cmpx([jnp.maximum(x[i], y[PEER_TOPK - 1 - i]) for i in range(PEER_TOPK)], _BITONIC16)
        shift //= 2
    rowk = lax.broadcasted_iota(jnp.int32, (PEER_TOPK, tm), 0)
    vals = jnp.zeros((PEER_TOPK, tm), F32)
    for r in range(PEER_TOPK):
        vals = jnp.where(rowk == r, jnp.concatenate([x[r], x[r]], axis=0), vals)
    return vals


def _topk_rows(scores, k, exact):
    n, tm = scores[0].shape
    if not exact:
        return [(_top16_sorted(s), None) for s in scores]
    rowk = lax.broadcasted_iota(jnp.int32, (k, tm), 0)
    rowf = lax.broadcasted_iota(jnp.int32, (n, tm), 0).astype(F32)
    state = [(s, jnp.zeros((k, tm), F32), jnp.full((n, tm), float(k), F32)) for s in scores]
    for r in range(k):
        nxt = []
        for s, vals, rank in state:
            m = jnp.max(s, axis=0, keepdims=True)
            sel = rowf == jnp.min(jnp.where(s == m, rowf, float(n)), axis=0, keepdims=True)
            nxt.append((jnp.where(sel, -jnp.inf, s), jnp.where(rowk == r, m, vals),
                        jnp.where(sel, float(r), rank)))
        state = nxt
    return [(vals, rank) for _, vals, rank in state]


def _stair_cells(t1, t2, op):
    pieces = [op(t1[0:1], t2)]
    pieces += [op(t1[a:a + 1], t2[0:8]) for a in range(1, 8)]
    pieces += [op(t1[8:16], t2[0:1])]
    return jnp.concatenate(pieces, axis=0)


def _peer_tables(scores, flat, exact):
    K = PEER_TOPK
    tm = scores[0].shape[1]
    (sv1, rank1), (sv2, rank2) = _topk_rows(scores, K, exact)
    cand = _stair_cells(sv1, sv2, lambda x, y: x + y)
    if exact:
        row16 = lax.broadcasted_iota(jnp.int32, (K, tm), 0).astype(F32)
        length = jnp.zeros((K, tm), F32)
        for r in range(K):
            m = jnp.max(cand, axis=0, keepdims=True)
            f = jnp.min(jnp.where(cand == m, flat, 1e9), axis=0, keepdims=True)
            cand = jnp.where(flat == f, -jnp.inf, cand)
            length = jnp.where(row16 == jnp.floor(f * (1.0 / K)), length + 1.0, length)
        picked = cand == -jnp.inf
    else:
        for r in range(K):
            cand = jnp.where(cand == jnp.max(cand, axis=0, keepdims=True), -jnp.inf, cand)
        picked = cand == -jnp.inf
        cnt = jnp.where(picked, 1.0, 0.0)
        rows = [jnp.sum(cnt[0:16], axis=0, keepdims=True)]
        rows += [jnp.sum(cnt[8 + 8 * a:16 + 8 * a], axis=0, keepdims=True) for a in range(1, 8)]
        length = jnp.concatenate(rows + [cnt[72:80]], axis=0)
    s1, s2 = scores
    lfull = jnp.zeros_like(s1)
    if exact:
        ok = jnp.ones((1, tm), jnp.bool_)
        for r in range(K):
            lfull = jnp.where(rank1 == float(r), length[r:r + 1], lfull)
    else:
        rank2 = jnp.zeros_like(s2)
        for r in range(K):
            lfull = jnp.where(s1 == sv1[r:r + 1], length[r:r + 1], lfull)
            rank2 = jnp.where(sv2[r:r + 1] > s2, float(r + 1), rank2)
        reach = lambda s, sv: jnp.sum(jnp.where(s >= sv[K - 1:K], 1.0, 0.0), axis=0, keepdims=True)
        ok = (reach(s1, sv1) == float(K)) & (reach(s2, sv2) == float(K)) & \
             (jnp.sum(length, axis=0, keepdims=True) == float(K))
    es1 = jnp.exp(sv1 - sv1[0:1])
    es2 = jnp.exp(sv2 - sv2[0:1])
    ecand = _stair_cells(es1, es2, lambda x, y: x * y)
    z = jnp.sum(jnp.where(picked, ecand, 0.0), axis=0, keepdims=True)
    e1 = jnp.exp(scores[0] - sv1[0:1]) / z
    e2 = jnp.exp(scores[1] - sv2[0:1])
    return (lfull, e1, rank2, e2), jnp.where(ok, 1.0, 0.0)


def _peer_topk_kernel(hn_ref, wq_ref, sk_ref, flat_ref, l_ref, e1_ref, rb_ref, e2_ref):
    flat = flat_ref[...]

    def head(h, carry):
        q = _dot(hn_ref[...], wq_ref[h])
        scores = []
        for p in range(2):
            qh, ql = _split(q[:, p * PEER_HALF:(p + 1) * PEER_HALF])
            kh, kl = _split(sk_ref[h, p])
            nt = (((1,), (1,)), ((), ()))
            dg = lambda a, b: lax.dot_general(a, b, nt, preferred_element_type=F32)
            scores.append(dg(kh, qh) + dg(kh, ql) + dg(kl, qh))

        def write(tabs, cs):
            l_ref[h, :, cs], e1_ref[h, :, cs] = tabs[0], tabs[1]
            rb_ref[h, :, cs], e2_ref[h, :, cs] = tabs[2].astype(rb_ref.dtype), tabs[3].astype(e2_ref.dtype)

        wd = flat.shape[1]
        for c in range(q.shape[0] // wd):
            cs = slice(c * wd, (c + 1) * wd)
            sc = [s[:, cs] for s in scores]
            tabs, ok = _peer_tables(sc, flat, exact=False)
            write(tabs, cs)

            @pl.when(jnp.min(ok) < 0.5)
            def _():
                write(_peer_tables(sc, flat, exact=True)[0], cs)
        return carry

    lax.fori_loop(0, PEER_HEADS, head, 0)


def _peer_topk(hn, wq_heads, subkeys, tm=512, wd=256):
    T, D = hn.shape
    H = wq_heads.shape[0]
    tm = min(tm, T)
    wd = min(wd, tm)
    out = jax.ShapeDtypeStruct((H, PEER_NKEYS, T), F32)
    outb = jax.ShapeDtypeStruct((H, PEER_NKEYS, T), BF16)
    oblk = pl.BlockSpec((H, PEER_NKEYS, tm), lambda i: (0, 0, i))
    return pl.pallas_call(
        _peer_topk_kernel,
        grid=(T // tm,),
        in_specs=[pl.BlockSpec((tm, D), lambda i: (i, 0)),
                  pl.BlockSpec(wq_heads.shape, lambda i: (0, 0, 0)),
                  pl.BlockSpec((H, 2, PEER_NKEYS, PEER_HALF), lambda i: (0, 0, 0, 0)),
                  pl.BlockSpec((_TRI_ROWS, wd), lambda i: (0, 0))],
        out_specs=[oblk, oblk, oblk, oblk],
        out_shape=[out, out, outb, outb],
        compiler_params=_params("parallel"),
        name="peer_topk",
    )(hn, wq_heads, subkeys, _tri_tables(wd))


def _gelu_tanh(x):
    c = math.sqrt(2.0 / math.pi)
    h = 0.5 * x
    return h + h * jnp.tanh(x * (c + (c * 0.044715) * (x * x)))


def _peer_dense_kernel(ni, hnT_ref, u_ref, vT_ref, l_ref, e1_ref, rb_ref, e2_ref, x_ref, o_ref,
                       acc_ref, act_ref, a_ref):
    j = pl.program_id(1)
    nk = PEER_NKEYS

    @pl.when(j == 0)
    def _():
        acc_ref[...] = jnp.zeros_like(acc_ref)

    tm = hnT_ref.shape[1]
    i0 = pl.multiple_of(j * ni, ni)
    zero = jnp.zeros((), BF16)
    a_ref[...] = _dot(u_ref[...], hnT_ref[...])
    for ii in range(ni):
        w = None
        for h in range(PEER_HEADS):
            lrow = jnp.broadcast_to(l_ref[h, pl.ds(i0 + ii, 1), :], (BF16_ROWS, tm)).astype(BF16)
            erow = jnp.broadcast_to(e1_ref[h, pl.ds(i0 + ii, 1), :], (BF16_ROWS, tm)).astype(BF16)
            wh = jnp.where(rb_ref[h] < lrow[None], e2_ref[h], zero) * erow[None]
            w = wh if w is None else w + wh
        a = a_ref[ii * nk:(ii + 1) * nk, :].astype(BF16)
        act_ref[ii * nk:(ii + 1) * nk, :] = _gelu_tanh(a) * w.reshape(nk, tm)
    acc_ref[...] += _dot(vT_ref[...], act_ref[...])

    @pl.when(j == pl.num_programs(1) - 1)
    def _():
        o_ref[...] = x_ref[...] + acc_ref[...].T


def _peer_dense(hnT, u_bf, vT_bf, tabs, x, tm=512, ni=16):
    D, T = hnT.shape
    NE = u_bf.shape[0]
    te = ni * PEER_NKEYS
    tm = min(tm, T)
    tab = pl.BlockSpec((PEER_HEADS, PEER_NKEYS, tm), lambda i, j: (0, 0, i))
    grp = PEER_NKEYS // BF16_ROWS
    tabb = pl.BlockSpec((PEER_HEADS, grp, BF16_ROWS, tm), lambda i, j: (0, 0, 0, i))
    lt, e1, rb, e2 = tabs
    rb, e2 = [t.reshape(PEER_HEADS, grp, BF16_ROWS, T) for t in (rb, e2)]
    return pl.pallas_call(
        functools.partial(_peer_dense_kernel, ni),
        grid=(T // tm, NE // te),
        in_specs=[pl.BlockSpec((D, tm), lambda i, j: (0, i)),
                  pl.BlockSpec((te, D), lambda i, j: (j, 0)),
                  pl.BlockSpec((D, te), lambda i, j: (0, j)),
                  tab, tab, tabb, tabb,
                  pl.BlockSpec((tm, D), lambda i, j: (i, 0))],
        out_specs=pl.BlockSpec((tm, D), lambda i, j: (i, 0)),
        out_shape=jax.ShapeDtypeStruct((T, D), F32),
        scratch_shapes=[pltpu.VMEM((D, tm), F32), pltpu.VMEM((te, tm), BF16), pltpu.VMEM((te, tm), F32)],
        compiler_params=_params("parallel", "arbitrary"),
        name="peer_experts",
    )(hnT, u_bf, vT_bf, lt, e1, rb, e2, x)


def kernel(x, norm1_g, w_in, gate_b, na_q_g, na_k_g, na_rpb, hy_conv_w, hy_conv_b, hy_w1, hy_b1, hy_freq,
           hy_w2, hy_b2, hy_w3, hy_bias, w_up_na, w_up_hy, w_out, norm2_g, peer_wq, peer_subkeys, peer_u,
           peer_v):
    B, S, D = x.shape
    assert B == 2, "the long convolution packs the two batches as one complex sequence"
    depth = w_in.shape[0]
    T = B * S
    xt = x.reshape(T, D)

    fc = _fft_constants(S)
    n1 = fc["n1"]
    pos_ext, dec_ext = _hy_pos_tables(S)
    head_sum = jnp.asarray(np.kron(np.eye(NA_HEADS), np.ones((NA_HEAD_DIM, NA_HEAD_DIM))), BF16)
    o_qk, o_v, o_hy = 2 * NA_WIDTH, 3 * NA_WIDTH, 3 * NA_WIDTH + 3 * HY_WIDTH
    tc = 2048
    ncol = FFT_N2 * HY_WIDTH

    for l in range(depth):
        (hn,) = _rmsnorm(xt, norm1_g[l])
        w = w_in[l].astype(BF16)
        qk_gain = jnp.concatenate([jnp.tile(na_q_g[l], NA_HEADS) * (NA_HEAD_DIM ** -0.5),
                                   jnp.tile(na_k_g[l], NA_HEADS)]).reshape(1, o_qk).astype(F32)
        qk, v, hy, gates = _proj_in(hn, w, qk_gain, head_sum, gate_b[l].reshape(1, 2 * D),
                                    (o_qk, o_v - o_qk, o_hy - o_v, w.shape[1] - o_hy))

        a_out = _na(qk, v, _na_bias_table(na_rpb[l]), B, S)

        z, x0 = _hy_prep(hy, hy_conv_w[l], hy_conv_b[l], S)
        w1p = jnp.pad(hy_w1[l], ((0, HY_FFN_HIDDEN - HY_POS_DIM), (0, 0)))
        g = _hy_filter(S, pos_ext, dec_ext, w1p, hy_b1[l], hy_freq[l], hy_w2[l], hy_b2[l], hy_w3[l])
        gr, gi = _dft_pair(fc["ga"], fc["gb"], g.reshape(n1, ncol), tc)
        z2 = z.reshape(n1, ncol)
        ar, ai = _dft_pair(fc["za"], fc["zb"], z2, tc)
        cube = lambda t: t.reshape(n1, FFT_N2, HY_WIDTH)
        br, bi = _spec_conv(cube(ar), cube(ai), cube(gr), cube(gi), fc)
        bias_t = jnp.tile(hy_bias[l], tc // HY_WIDTH).reshape(1, tc)
        b_out = _idft_out(fc["ya"], fc["yb"], br.reshape(n1, ncol), bi.reshape(n1, ncol),
                          z2, x0.reshape(n1, ncol), bias_t, tc).reshape(T, HY_WIDTH)

        xt = _merge_out(a_out, b_out, w_up_na[l].astype(BF16), w_up_hy[l].astype(BF16), gates,
                        w_out[l].astype(BF16), xt)

        hn2, hn2T = _rmsnorm(xt, norm2_g[l], transposed=True)
        wq_heads = peer_wq[l].reshape(D, PEER_HEADS, 2 * PEER_HALF).transpose(1, 0, 2).astype(BF16)
        tabs = _peer_topk(hn2, wq_heads, peer_subkeys[l])
        xt = _peer_dense(hn2T, peer_u[l].astype(BF16), peer_v[l].T.astype(BF16), tabs, xt)
    return xt.reshape(B, S, D)
```

```python
import functools
import math

import numpy as np
import jax
import jax.numpy as jnp
from jax import lax
from jax.experimental import pallas as pl
from jax.experimental.pallas import tpu as pltpu

F32 = jnp.float32
BF16 = jnp.bfloat16

GRID_W = 64
NA_HEADS = 8
NA_HEAD_DIM = 64
NA_WIDTH = NA_HEADS * NA_HEAD_DIM
NA_KR = 8
NA_KW = 16
HY_WIDTH = 512
HY_POS_BANDS = 16
HY_POS_DIM = 1 + 2 * HY_POS_BANDS
HY_FFN_HIDDEN = 64
HY_DECAY_TARGET = 1e-2
HY_FAST_PCT = 0.3
HY_SLOW_PCT = 1.5
PEER_HEADS = 8
PEER_NKEYS = 128
PEER_TOPK = 16
PEER_HALF = 128
EPS = 1e-6
NEG = -1e30

LANES = 128
VMEM_LIMIT_BYTES = 56 * 1024 * 1024
FFT_N2 = 128
BF16_ROWS = 16


def _params(*sem):
    return pltpu.CompilerParams(dimension_semantics=sem, vmem_limit_bytes=VMEM_LIMIT_BYTES)


def _split(x):
    hi = x.astype(BF16)
    lo = (x - hi.astype(F32)).astype(BF16)
    return hi, lo


def _dot(a, b):
    return jnp.dot(a, b, preferred_element_type=F32)


def _dot3(ah, al, bh, bl):
    return _dot(ah, bh) + _dot(ah, bl) + _dot(al, bh)


def _rmsnorm_kernel(x_ref, g_ref, o_ref, *maybe_ot_ref):
    x = x_ref[...]
    y = x * lax.rsqrt(jnp.mean(x * x, axis=-1, keepdims=True) + EPS)
    y = y * g_ref[...]
    o_ref[...] = y.astype(BF16)
    for ot_ref in maybe_ot_ref:
        ot_ref[...] = y.T.astype(BF16)


def _rmsnorm(x, g, transposed=False, tm=1024):
    T, D = x.shape
    out_specs = [pl.BlockSpec((tm, D), lambda i: (i, 0))]
    out_shape = [jax.ShapeDtypeStruct((T, D), BF16)]
    if transposed:
        out_specs.append(pl.BlockSpec((D, tm), lambda i: (0, i)))
        out_shape.append(jax.ShapeDtypeStruct((D, T), BF16))
    return pl.pallas_call(
        _rmsnorm_kernel,
        grid=(T // tm,),
        in_specs=[pl.BlockSpec((tm, D), lambda i: (i, 0)),
                  pl.BlockSpec((1, D), lambda i: (0, 0))],
        out_specs=out_specs,
        out_shape=out_shape,
        compiler_params=_params("parallel"),
        name="rmsnorm",
    )(x, g.reshape(1, D))


def _epi_head_rmsnorm(acc, gain, head_sum):
    hi, lo = _split(acc * acc)
    ms = (_dot(hi, head_sum) + _dot(lo, head_sum)) * (1.0 / NA_HEAD_DIM)
    return acc * lax.rsqrt(ms + EPS) * gain


def _epi_gate(acc, bias):
    return jax.nn.sigmoid(acc + bias)


def _proj_in_kernel(jq, jv, jh, hn_ref, w_ref, gain_ref, hs_ref, gb_ref, qk_ref, v_ref, hy_ref, g_ref):
    j = pl.program_id(1)
    acc = _dot(hn_ref[...], w_ref[...])

    @pl.when(j < jq)
    def _():
        qk_ref[...] = _epi_head_rmsnorm(acc, gain_ref[...], hs_ref[...]).astype(qk_ref.dtype)

    @pl.when((j >= jq) & (j < jv))
    def _():
        v_ref[...] = acc.astype(v_ref.dtype)

    @pl.when((j >= jv) & (j < jh))
    def _():
        hy_ref[...] = acc

    @pl.when(j >= jh)
    def _():
        g_ref[...] = _epi_gate(acc, gb_ref[...]).astype(g_ref.dtype)


def _proj_in(hn, w, qk_gain, head_sum, gate_b, widths, tm=2048, tn=512):
    T, D = hn.shape
    tm = min(tm, T)
    wq, wv, wh, wg = widths
    assert all(x % tn == 0 for x in widths)
    jq, jv, jh, jn = wq // tn, (wq + wv) // tn, (wq + wv + wh) // tn, sum(widths) // tn
    rng = lambda lo, hi: (lambda i, j: (i, jnp.clip(j - lo, 0, hi - lo - 1)))
    crng = lambda lo, hi: (lambda i, j: (0, jnp.clip(j - lo, 0, hi - lo - 1)))
    blk = lambda f: pl.BlockSpec((tm, tn), f)
    return pl.pallas_call(
        functools.partial(_proj_in_kernel, jq, jv, jh),
        grid=(T // tm, jn),
        in_specs=[pl.BlockSpec((tm, D), lambda i, j: (i, 0)),
                  pl.BlockSpec((D, tn), lambda i, j: (0, j)),
                  pl.BlockSpec((1, tn), crng(0, jq)),
                  pl.BlockSpec(head_sum.shape, lambda i, j: (0, 0)),
                  pl.BlockSpec((1, tn), crng(jh, jn))],
        out_specs=[blk(rng(0, jq)), blk(rng(jq, jv)), blk(rng(jv, jh)), blk(rng(jh, jn))],
        out_shape=[jax.ShapeDtypeStruct((T, wq), BF16), jax.ShapeDtypeStruct((T, wv), BF16),
                   jax.ShapeDtypeStruct((T, wh), F32), jax.ShapeDtypeStruct((T, wg), BF16)],
        compiler_params=_params("parallel", "arbitrary"),
        name="proj_in",
    )(hn, w, qk_gain, head_sum, gate_b)


def _merge_out_kernel(a_ref, b_ref, wa_ref, wb_ref, g_ref, wo_ref, x_ref, o_ref):
    d = o_ref.shape[1]
    ya = _dot(a_ref[...], wa_ref[...])
    yb = _dot(b_ref[...], wb_ref[...])
    g = g_ref[...].astype(F32)
    merged = (g[:, :d] * ya + g[:, d:] * yb).astype(BF16)
    o_ref[...] = x_ref[...] + _dot(merged, wo_ref[...])


def _merge_out(a_out, b_out, w_na, w_hy, gates, w_o, x, tm=512):
    T, wa = a_out.shape
    D = w_na.shape[1]
    full = lambda a: pl.BlockSpec(a.shape, lambda i: (0, 0))
    return pl.pallas_call(
        _merge_out_kernel,
        grid=(T // tm,),
        in_specs=[pl.BlockSpec((tm, wa), lambda i: (i, 0)),
                  pl.BlockSpec((tm, b_out.shape[1]), lambda i: (i, 0)),
                  full(w_na), full(w_hy),
                  pl.BlockSpec((tm, 2 * D), lambda i: (i, 0)),
                  full(w_o),
                  pl.BlockSpec((tm, D), lambda i: (i, 0))],
        out_specs=pl.BlockSpec((tm, D), lambda i: (i, 0)),
        out_shape=jax.ShapeDtypeStruct((T, D), F32),
        compiler_params=_params("parallel"),
        name="merge_out",
    )(a_out, b_out, w_na, w_hy, gates, w_o, x)


def _na_kernel(rows, rb, q_ref, k_ref, v_ref, bias_ref, o_ref, s_ref, p_ref):
    blk = pl.program_id(2)
    lane = lax.broadcasted_iota(jnp.int32, (GRID_W, LANES), 1)
    nk = NA_KR * GRID_W
    starts = []
    for i in range(rb):
        r = blk * rb + i
        r0 = jnp.clip(r - NA_KR // 2, 0, rows - NA_KR)
        dr0 = r0 - r + (NA_KR - 1)
        start = pl.multiple_of(r0 * GRID_W, GRID_W)
        starts.append(start)
        qr = q_ref[i * GRID_W:(i + 1) * GRID_W, :]
        kw = k_ref[pl.ds(start, nk), :]
        for hh in range(2):
            in_head = (lane >= NA_HEAD_DIM) if hh else (lane < NA_HEAD_DIM)
            qm = jnp.where(in_head, qr, jnp.zeros_like(qr))
            s = lax.dot_general(qm, kw, (((1,), (1,)), ((), ())), preferred_element_type=F32)
            s_ref[(2 * i + hh) * GRID_W:(2 * i + hh + 1) * GRID_W, :] = s + bias_ref[hh, dr0]
    s = s_ref[...]
    p = jnp.exp(s - jnp.max(s, axis=-1, keepdims=True))
    p_ref[...] = (p * (1.0 / jnp.sum(p, axis=-1, keepdims=True))).astype(BF16)
    for i in range(rb):
        vw = v_ref[pl.ds(starts[i], nk), :]
        o0 = _dot(p_ref[(2 * i) * GRID_W:(2 * i + 1) * GRID_W, :], vw)
        o1 = _dot(p_ref[(2 * i + 1) * GRID_W:(2 * i + 2) * GRID_W, :], vw)
        o_ref[i * GRID_W:(i + 1) * GRID_W, :] = jnp.where(lane < NA_HEAD_DIM, o0, o1).astype(o_ref.dtype)


def _na_bias_table(rpb):
    c = np.arange(GRID_W)
    c0 = np.clip(c - NA_KW // 2, 0, GRID_W - NA_KW)
    col_in = (c[None, :] >= c0[:, None]) & (c[None, :] < c0[:, None] + NA_KW)
    dc_idx = np.clip(c[None, :] - c[:, None] + (NA_KW - 1), 0, 2 * NA_KW - 2)
    onehot = jnp.asarray((dc_idx[:, :, None] == np.arange(2 * NA_KW - 1)).astype(np.float32))
    rows = jnp.stack([rpb.astype(F32)[:, d:d + NA_KR] for d in range(NA_KR)], axis=1)
    b = jnp.einsum("hdjc,qkc->hdqjk", rows, onehot, precision=lax.Precision.HIGHEST)
    b = jnp.where(jnp.asarray(col_in)[None, None, :, None, :], b, NEG)
    return b.reshape(NA_HEADS, NA_KR, GRID_W, NA_KR * GRID_W)


def _na(qk, v, bias, B, S, rb=16):
    rows = S // GRID_W
    assert rows >= NA_KR and rows % rb == 0
    nblk = rows // rb
    tq = rb * GRID_W
    kofs = NA_WIDTH // LANES
    return pl.pallas_call(
        functools.partial(_na_kernel, rows, rb),
        grid=(B, NA_HEADS // 2, nblk),
        in_specs=[pl.BlockSpec((tq, LANES), lambda b, p, r: (b * nblk + r, p)),
                  pl.BlockSpec((S, LANES), lambda b, p, r: (b, kofs + p)),
                  pl.BlockSpec((S, LANES), lambda b, p, r: (b, p)),
                  pl.BlockSpec((2, NA_KR, GRID_W, NA_KR * GRID_W), lambda b, p, r: (p, 0, 0, 0))],
        out_specs=pl.BlockSpec((tq, LANES), lambda b, p, r: (b * nblk + r, p)),
        out_shape=jax.ShapeDtypeStruct((B * S, NA_WIDTH), BF16),
        scratch_shapes=[pltpu.VMEM((2 * tq, NA_KR * GRID_W), F32), pltpu.VMEM((2 * tq, NA_KR * GRID_W), BF16)],
        compiler_params=_params("parallel", "parallel", "parallel"),
        name="na_attention",
    )(qk, qk, v, bias)


def _hy_prep_kernel(S, hy_ref, prev_ref, next_ref, w_ref, b_ref, z_ref, x0_ref):
    tm, C = hy_ref.shape
    tok0 = pl.program_id(0) * tm
    h = hy_ref[...]
    row = lax.broadcasted_iota(jnp.int32, (tm, C), 0)
    prev_row = jnp.where(tok0 % S == 0, 0.0, prev_ref[7:8, :])
    next_row = jnp.where((tok0 + tm) % S == 0, 0.0, next_ref[0:1, :])
    up = jnp.where(row == 0, prev_row, pltpu.roll(h, 1, axis=0))
    dn = jnp.where(row == tm - 1, next_row, pltpu.roll(h, tm - 1, axis=0))
    u = up * w_ref[0:1, :] + h * w_ref[1:2, :] + dn * w_ref[2:3, :] + b_ref[...]
    c = C // 3
    x0_ref[...] = u[:, :c]
    z_ref[...] = u[:, 2 * c:] * u[:, c:2 * c]


def _hy_prep(hy, conv_w, conv_b, S, tm=1024):
    T, C = hy.shape
    nb = tm // 8
    last = T // 8 - 1
    return pl.pallas_call(
        functools.partial(_hy_prep_kernel, S),
        grid=(T // tm,),
        in_specs=[pl.BlockSpec((tm, C), lambda i: (i, 0)),
                  pl.BlockSpec((8, C), lambda i: (jnp.maximum(i * nb - 1, 0), 0)),
                  pl.BlockSpec((8, C), lambda i: (jnp.minimum((i + 1) * nb, last), 0)),
                  pl.BlockSpec((3, C), lambda i: (0, 0)),
                  pl.BlockSpec((1, C), lambda i: (0, 0))],
        out_specs=[pl.BlockSpec((tm, C // 3), lambda i: (i, 0)),
                   pl.BlockSpec((tm, C // 3), lambda i: (i, 0))],
        out_shape=[jax.ShapeDtypeStruct((T, C // 3), F32), jax.ShapeDtypeStruct((T, C // 3), F32)],
        compiler_params=_params("parallel"),
        name="hyena_prep",
    )(hy, hy, hy, conv_w, conv_b.reshape(1, C))


def _hy_filter_kernel(L, pos_ref, dec_ref, w1_ref, b1_ref, fr_ref, w2_ref, b2_ref, w3_ref, g_ref):
    tr, C = g_ref.shape
    fr = fr_ref[...]
    w1h, w1l = _split(w1_ref[...])
    w2h, w2l = _split(w2_ref[...])
    w3h, w3l = _split(w3_ref[...])
    ph, plo = _split(pos_ref[...])
    h = jnp.sin(fr * (_dot3(ph, plo, w1h, w1l) + _dot(plo, w1l) + b1_ref[...]))
    hh, hl = _split(h)
    h = jnp.sin(fr * (_dot3(hh, hl, w2h, w2l) + _dot(hl, w2l) + b2_ref[...]))
    hh, hl = _split(h)
    h = _dot3(hh, hl, w3h, w3l) + _dot(hl, w3l)
    dec = jnp.exp(-(pos_ref[:, 0:1] * dec_ref[...]))
    hf = h[:, :C] * dec
    hb = h[:, C:] * dec
    n = pl.program_id(0) * tr + lax.broadcasted_iota(jnp.int32, (tr, C), 0)
    g = jnp.where(n < L, hf, hb)
    g = jnp.where(n == L, 0.0, g)
    g_ref[...] = jnp.where(n == 0, hf + hb, g)


def _hy_filter(L, pos_ext, dec_ext, w1p, b1, freq, w2, b2, w3, tr=1024):
    N = 2 * L
    tr = min(tr, N)
    H = HY_FFN_HIDDEN
    full = lambda shape: pl.BlockSpec(shape, lambda i: (0, 0))
    return pl.pallas_call(
        functools.partial(_hy_filter_kernel, L),
        grid=(N // tr,),
        in_specs=[pl.BlockSpec((tr, pos_ext.shape[1]), lambda i: (i, 0)),
                  full((1, HY_WIDTH)),
                  full(w1p.shape), full((1, H)), full((1, H)), full((H, H)), full((1, H)),
                  full((H, 2 * HY_WIDTH))],
        out_specs=pl.BlockSpec((tr, HY_WIDTH), lambda i: (i, 0)),
        out_shape=jax.ShapeDtypeStruct((N, HY_WIDTH), F32),
        compiler_params=_params("parallel"),
        name="hyena_filter",
    )(pos_ext, dec_ext, w1p, b1.reshape(1, H), freq.reshape(1, H), w2, b2.reshape(1, H), w3)


def _dft_pair_kernel(ma_ref, mb_ref, x_ref, or_ref, oi_ref):
    xh, xl = _split(x_ref[...])
    or_ref[...] = _dot3(ma_ref[0], ma_ref[1], xh, xl)
    oi_ref[...] = _dot3(mb_ref[0], mb_ref[1], xh, xl)


def _dft_pair(ma, mb, x, tc=2048):
    R = ma.shape[1]
    K, C = x.shape
    return pl.pallas_call(
        _dft_pair_kernel,
        grid=(C // tc,),
        in_specs=[pl.BlockSpec(ma.shape, lambda j: (0, 0, 0)),
                  pl.BlockSpec(mb.shape, lambda j: (0, 0, 0)),
                  pl.BlockSpec((K, tc), lambda j: (0, j))],
        out_specs=[pl.BlockSpec((R, tc), lambda j: (0, j)), pl.BlockSpec((R, tc), lambda j: (0, j))],
        out_shape=[jax.ShapeDtypeStruct((R, C), F32), jax.ShapeDtypeStruct((R, C), F32)],
        compiler_params=_params("parallel"),
        name="dft_outer",
    )(ma, mb, x)


def _spec_conv_kernel(ar_ref, ai_ref, fr_ref, fi_ref, twc_ref, tws_ref, wf_ref, wi_ref, br_ref, bi_ref):
    kb, n2, C = ar_ref.shape
    wfh, wfl, wih, wil = wf_ref[0], wf_ref[1], wi_ref[0], wi_ref[1]
    for k in range(kb):
        c = jnp.concatenate([twc_ref[k]] * (C // LANES), axis=1)
        s = jnp.concatenate([tws_ref[k]] * (C // LANES), axis=1)

        def inner_dft(re, im):
            h, l = _split(jnp.concatenate([re * c + im * s, im * c - re * s], axis=0))
            return _dot3(wfh, wfl, h, l)

        x = inner_dft(ar_ref[k], ai_ref[k])
        g = inner_dft(fr_ref[k], fi_ref[k])
        xr, xi, gr, gi = x[:n2], x[n2:], g[:n2], g[n2:]
        yh, yl = _split(jnp.concatenate([xr * gr - xi * gi, xr * gi + xi * gr], axis=0))
        b = _dot3(wih, wil, yh, yl)
        br, bi = b[:n2], b[n2:]
        br_ref[k] = br * c - bi * s
        bi_ref[k] = bi * c + br * s


def _spec_conv(ar, ai, fr, fi, fc, kb=8):
    n1, n2, C = ar.shape
    kb = min(kb, n1)
    blk = pl.BlockSpec((kb, n2, C), lambda k: (k, 0, 0))
    tblk = pl.BlockSpec((kb, n2, LANES), lambda k: (k, 0, 0))
    wblk = pl.BlockSpec((2, 2 * n2, 2 * n2), lambda k: (0, 0, 0))
    return pl.pallas_call(
        _spec_conv_kernel,
        grid=(n1 // kb,),
        in_specs=[blk, blk, blk, blk, tblk, tblk, wblk, wblk],
        out_specs=[blk, blk],
        out_shape=[jax.ShapeDtypeStruct((n1, n2, C), F32), jax.ShapeDtypeStruct((n1, n2, C), F32)],
        compiler_params=_params("parallel"),
        name="spectrum_conv",
    )(ar, ai, fr, fi, fc["twc"], fc["tws"], fc["wf"], fc["wi"])


def _idft_out_kernel(ma_ref, mb_ref, br_ref, bi_ref, z_ref, x0_ref, bias_ref, o_ref):
    brh, brl = _split(br_ref[...])
    bih, bil = _split(bi_ref[...])
    y = _dot3(ma_ref[0], ma_ref[1], brh, brl) + _dot3(mb_ref[0], mb_ref[1], bih, bil)
    o_ref[...] = ((y + z_ref[...] * bias_ref[...]) * x0_ref[...]).astype(o_ref.dtype)


def _idft_out(ma, mb, br, bi, z2, x02, bias_t, tc=2048):
    R = ma.shape[1]
    K, C = br.shape
    col = lambda rws: pl.BlockSpec((rws, tc), lambda j: (0, j))
    return pl.pallas_call(
        _idft_out_kernel,
        grid=(C // tc,),
        in_specs=[pl.BlockSpec(ma.shape, lambda j: (0, 0, 0)), pl.BlockSpec(mb.shape, lambda j: (0, 0, 0)),
                  col(K), col(K), col(R), col(R), pl.BlockSpec((1, tc), lambda j: (0, 0))],
        out_specs=col(R),
        out_shape=jax.ShapeDtypeStruct((R, C), BF16),
        compiler_params=_params("parallel"),
        name="idft_outer",
    )(ma, mb, br, bi, z2, x02, bias_t)


def _split_const(m):
    m32 = jnp.asarray(np.asarray(m, np.float32))
    hi, lo = _split(m32)
    return jnp.stack([hi, lo])


def _fft_constants(L):
    N = 2 * L
    n2 = FFT_N2
    n1 = N // n2
    h = n1 // 2
    k1 = np.arange(n1)[:, None].astype(np.float64)
    ang = 2.0 * np.pi * k1 * np.arange(n1)[None, :] / n1
    c, s = np.cos(ang), np.sin(ang)
    za = np.concatenate([c[:, :h], s[:, :h]], axis=1)
    zb = np.concatenate([-s[:, :h], c[:, :h]], axis=1)
    ga, gb = c, -s
    ya = np.concatenate([c[:h], s[:h]], axis=0)
    yb = np.concatenate([-s[:h], c[:h]], axis=0)
    kk = np.arange(n2)[:, None].astype(np.float64)
    nn = np.arange(n2)[None, :].astype(np.float64)
    base = 2.0 * np.pi * kk * nn / n2
    cb, sb = np.cos(base), np.sin(base)
    wf = np.block([[cb, sb], [-sb, cb]])
    wi = np.block([[cb.T, -sb.T], [sb.T, cb.T]]) / N
    tw = 2.0 * np.pi * np.arange(n1)[:, None] * nn / N
    rep = lambda m: jnp.broadcast_to(jnp.asarray(np.asarray(m, np.float32))[:, :, None], (n1, n2, LANES))
    sp = lambda m: _split_const(m)
    return dict(za=sp(za), zb=sp(zb), ga=sp(ga), gb=sp(gb), ya=sp(ya), yb=sp(yb), wf=sp(wf), wi=sp(wi),
                twc=rep(np.cos(tw)), tws=rep(np.sin(tw)), n1=n1)


def _hy_pos_tables(L):
    ext = lambda a: jnp.concatenate([a, a[0:1], jnp.flip(a[1:], axis=0)], axis=0)
    t = ext(jnp.linspace(0.0, 1.0, L, dtype=F32)[:, None])
    w = ext(2.0 * math.pi * jnp.arange(L, dtype=F32)[:, None] / L)
    f = jnp.linspace(1e-4, HY_POS_BANDS - 1, HY_POS_BANDS, dtype=F32)[None, :]
    z = jnp.concatenate([t, jnp.cos(f * w), -jnp.sin(f * w)], axis=-1)
    max_decay = math.log(HY_DECAY_TARGET) / HY_FAST_PCT
    min_decay = math.log(HY_DECAY_TARGET) / HY_SLOW_PCT
    deltas = jnp.linspace(min_decay, max_decay, HY_WIDTH, dtype=F32)
    zp = jnp.pad(z, ((0, 0), (0, HY_FFN_HIDDEN - HY_POS_DIM)))
    return zp, jnp.abs(deltas)[None, :]


_TRI_ROWS = 16 + 8 * 7 + 8


def _tri_tables(tm):
    flat = [b for b in range(16)]
    for a in range(1, 8):
        flat += [16 * a + b for b in range(8)]
    flat += [16 * a for a in range(8, 16)]
    return jnp.asarray(np.tile(np.asarray(flat, np.float32)[:, None], (1, tm)))


def _sort16_pairs():
    def merge(lo, hi, r):
        step = r * 2
        if step < hi - lo:
            yield from merge(lo, hi, step)
            yield from merge(lo + r, hi, step)
            yield from [(i, i + r) for i in range(lo + r, hi - r, step)]
        else:
            yield (lo, lo + r)

    def sort(lo, hi):
        if hi - lo >= 1:
            mid = lo + (hi - lo) // 2
            yield from sort(lo, mid)
            yield from sort(mid + 1, hi)
            yield from merge(lo, hi, 1)
    return list(sort(0, 15))


_SORT16 = _sort16_pairs()
_BITONIC16 = [(i, i + d) for d in (8, 4, 2, 1) for i in range(16) if not i & d]


def _cmpx(x, pairs):
    x = list(x)
    for i, j in pairs:
        x[i], x[j] = jnp.maximum(x[i], x[j]), jnp.minimum(x[i], x[j])
    return x


def _top16_sorted(s):
    n, tm = s.shape
    sub = n // PEER_TOPK
    x = _cmpx([s[r * sub:(r + 1) * sub] for r in range(PEER_TOPK)], _SORT16)
    shift = sub // 2
    while shift:
        y = [pltpu.roll(v, shift, axis=0) for v in x]
        x = _cmpx([jnp.maximum(x[i], y[PEER_TOPK - 1 - i]) for i in range(PEER_TOPK)], _BITONIC16)
        shift //= 2
    rowk = lax.broadcasted_iota(jnp.int32, (PEER_TOPK, tm), 0)
    vals = jnp.zeros((PEER_TOPK, tm), F32)
    for r in range(PEER_TOPK):
        vals = jnp.where(rowk == r, jnp.concatenate([x[r], x[r]], axis=0), vals)
    return vals


def _topk_rows(scores, k, exact):
    n, tm = scores[0].shape
    if not exact:
        return [(_top16_sorted(s), None) for s in scores]
    rowk = lax.broadcasted_iota(jnp.int32, (k, tm), 0)
    rowf = lax.broadcasted_iota(jnp.int32, (n, tm), 0).astype(F32)
    state = [(s, jnp.zeros((k, tm), F32), jnp.full((n, tm), float(k), F32)) for s in scores]
    for r in range(k):
        nxt = []
        for s, vals, rank in state:
            m = jnp.max(s, axis=0, keepdims=True)
            sel = rowf == jnp.min(jnp.where(s == m, rowf, float(n)), axis=0, keepdims=True)
            nxt.append((jnp.where(sel, -jnp.inf, s), jnp.where(rowk == r, m, vals),
                        jnp.where(sel, float(r), rank)))
        state = nxt
    return [(vals, rank) for _, vals, rank in state]


def _stair_cells(t1, t2, op):
    pieces = [op(t1[0:1], t2)]
    pieces += [op(t1[a:a + 1], t2[0:8]) for a in range(1, 8)]
    pieces += [op(t1[8:16], t2[0:1])]
    return jnp.concatenate(pieces, axis=0)


def _peer_tables(scores, flat, exact):
    K = PEER_TOPK
    tm = scores[0].shape[1]
    (sv1, rank1), (sv2, rank2) = _topk_rows(scores, K, exact)
    cand = _stair_cells(sv1, sv2, lambda x, y: x + y)
    if exact:
        row16 = lax.broadcasted_iota(jnp.int32, (K, tm), 0).astype(F32)
        length = jnp.zeros((K, tm), F32)
        for r in range(K):
            m = jnp.max(cand, axis=0, keepdims=True)
            f = jnp.min(jnp.where(cand == m, flat, 1e9), axis=0, keepdims=True)
            cand = jnp.where(flat == f, -jnp.inf, cand)
            length = jnp.where(row16 == jnp.floor(f * (1.0 / K)), length + 1.0, length)
        picked = cand == -jnp.inf
    else:
        for r in range(K):
            cand = jnp.where(cand == jnp.max(cand, axis=0, keepdims=True), -jnp.inf, cand)
        picked = cand == -jnp.inf
        cnt = jnp.where(picked, 1.0, 0.0)
        rows = [jnp.sum(cnt[0:16], axis=0, keepdims=True)]
        rows += [jnp.sum(cnt[8 + 8 * a:16 + 8 * a], axis=0, keepdims=True) for a in range(1, 8)]
        length = jnp.concatenate(rows + [cnt[72:80]], axis=0)
    s1, s2 = scores
    lfull = jnp.zeros_like(s1)
    if exact:
        ok = jnp.ones((1, tm), jnp.bool_)
        for r in range(K):
            lfull = jnp.where(rank1 == float(r), length[r:r + 1], lfull)
    else:
        rank2 = jnp.zeros_like(s2)
        for r in range(K):
            lfull = jnp.where(s1 == sv1[r:r + 1], length[r:r + 1], lfull)
            rank2 = jnp.where(sv2[r:r + 1] > s2, float(r + 1), rank2)
        reach = lambda s, sv: jnp.sum(jnp.where(s >= sv[K - 1:K], 1.0, 0.0), axis=0, keepdims=True)
        ok = (reach(s1, sv1) == float(K)) & (reach(s2, sv2) == float(K)) & \
             (jnp.sum(length, axis=0, keepdims=True) == float(K))
    es1 = jnp.exp(sv1 - sv1[0:1])
    es2 = jnp.exp(sv2 - sv2[0:1])
    ecand = _stair_cells(es1, es2, lambda x, y: x * y)
    z = jnp.sum(jnp.where(picked, ecand, 0.0), axis=0, keepdims=True)
    e1 = jnp.exp(scores[0] - sv1[0:1]) / z
    e2 = jnp.exp(scores[1] - sv2[0:1])
    return (lfull, e1, rank2, e2), jnp.where(ok, 1.0, 0.0)


def _peer_topk_kernel(hn_ref, wq_ref, sk_ref, flat_ref, l_ref, e1_ref, rb_ref, e2_ref):
    flat = flat_ref[...]

    def head(h, carry):
        q = _dot(hn_ref[...], wq_ref[h])
        scores = []
        for p in range(2):
            qh, ql = _split(q[:, p * PEER_HALF:(p + 1) * PEER_HALF])
            kh, kl = _split(sk_ref[h, p])
            nt = (((1,), (1,)), ((), ()))
            dg = lambda a, b: lax.dot_general(a, b, nt, preferred_element_type=F32)
            scores.append(dg(kh, qh) + dg(kh, ql) + dg(kl, qh))

        def write(tabs, cs):
            l_ref[h, :, cs], e1_ref[h, :, cs] = tabs[0], tabs[1]
            rb_ref[h, :, cs], e2_ref[h, :, cs] = tabs[2].astype(rb_ref.dtype), tabs[3].astype(e2_ref.dtype)

        wd = flat.shape[1]
        for c in range(q.shape[0] // wd):
            cs = slice(c * wd, (c + 1) * wd)
            sc = [s[:, cs] for s in scores]
            tabs, ok = _peer_tables(sc, flat, exact=False)
            write(tabs, cs)

            @pl.when(jnp.min(ok) < 0.5)
            def _():
                write(_peer_tables(sc, flat, exact=True)[0], cs)
        return carry

    lax.fori_loop(0, PEER_HEADS, head, 0)


def _peer_topk(hn, wq_heads, subkeys, tm=512, wd=256):
    T, D = hn.shape
    H = wq_heads.shape[0]
    tm = min(tm, T)
    wd = min(wd, tm)
    out = jax.ShapeDtypeStruct((H, PEER_NKEYS, T), F32)
    outb = jax.ShapeDtypeStruct((H, PEER_NKEYS, T), BF16)
    oblk = pl.BlockSpec((H, PEER_NKEYS, tm), lambda i: (0, 0, i))
    return pl.pallas_call(
        _peer_topk_kernel,
        grid=(T // tm,),
        in_specs=[pl.BlockSpec((tm, D), lambda i: (i, 0)),
                  pl.BlockSpec(wq_heads.shape, lambda i: (0, 0, 0)),
                  pl.BlockSpec((H, 2, PEER_NKEYS, PEER_HALF), lambda i: (0, 0, 0, 0)),
                  pl.BlockSpec((_TRI_ROWS, wd), lambda i: (0, 0))],
        out_specs=[oblk, oblk, oblk, oblk],
        out_shape=[out, out, outb, outb],
        compiler_params=_params("parallel"),
        name="peer_topk",
    )(hn, wq_heads, subkeys, _tri_tables(wd))


def _gelu_tanh(x):
    c = math.sqrt(2.0 / math.pi)
    h = 0.5 * x
    return h + h * jnp.tanh(x * (c + (c * 0.044715) * (x * x)))


def _peer_dense_kernel(ni, hnT_ref, u_ref, vT_ref, l_ref, e1_ref, rb_ref, e2_ref, x_ref, o_ref,
                       acc_ref, act_ref, a_ref):
    j = pl.program_id(1)
    nk = PEER_NKEYS

    @pl.when(j == 0)
    def _():
        acc_ref[...] = jnp.zeros_like(acc_ref)

    tm = hnT_ref.shape[1]
    i0 = pl.multiple_of(j * ni, ni)
    zero = jnp.zeros((), BF16)
    a_ref[...] = _dot(u_ref[...], hnT_ref[...])
    for ii in range(ni):
        w = None
        for h in range(PEER_HEADS):
            lrow = jnp.broadcast_to(l_ref[h, pl.ds(i0 + ii, 1), :], (BF16_ROWS, tm)).astype(BF16)
            erow = jnp.broadcast_to(e1_ref[h, pl.ds(i0 + ii, 1), :], (BF16_ROWS, tm)).astype(BF16)
            wh = jnp.where(rb_ref[h] < lrow[None], e2_ref[h], zero) * erow[None]
            w = wh if w is None else w + wh
        a = a_ref[ii * nk:(ii + 1) * nk, :].astype(BF16)
        act_ref[ii * nk:(ii + 1) * nk, :] = _gelu_tanh(a) * w.reshape(nk, tm)
    acc_ref[...] += _dot(vT_ref[...], act_ref[...])

    @pl.when(j == pl.num_programs(1) - 1)
    def _():
        o_ref[...] = x_ref[...] + acc_ref[...].T


def _peer_dense(hnT, u_bf, vT_bf, tabs, x, tm=512, ni=16):
    D, T = hnT.shape
    NE = u_bf.shape[0]
    te = ni * PEER_NKEYS
    tm = min(tm, T)
    tab = pl.BlockSpec((PEER_HEADS, PEER_NKEYS, tm), lambda i, j: (0, 0, i))
    grp = PEER_NKEYS // BF16_ROWS
    tabb = pl.BlockSpec((PEER_HEADS, grp, BF16_ROWS, tm), lambda i, j: (0, 0, 0, i))
    lt, e1, rb, e2 = tabs
    rb, e2 = [t.reshape(PEER_HEADS, grp, BF16_ROWS, T) for t in (rb, e2)]
    return pl.pallas_call(
        functools.partial(_peer_dense_kernel, ni),
        grid=(T // tm, NE // te),
        in_specs=[pl.BlockSpec((D, tm), lambda i, j: (0, i)),
                  pl.BlockSpec((te, D), lambda i, j: (j, 0)),
                  pl.BlockSpec((D, te), lambda i, j: (0, j)),
                  tab, tab, tabb, tabb,
                  pl.BlockSpec((tm, D), lambda i, j: (i, 0))],
        out_specs=pl.BlockSpec((tm, D), lambda i, j: (i, 0)),
        out_shape=jax.ShapeDtypeStruct((T, D), F32),
        scratch_shapes=[pltpu.VMEM((D, tm), F32), pltpu.VMEM((te, tm), BF16), pltpu.VMEM((te, tm), F32)],
        compiler_params=_params("parallel", "arbitrary"),
        name="peer_experts",
    )(hnT, u_bf, vT_bf, lt, e1, rb, e2, x)


def kernel(x, norm1_g, w_in, gate_b, na_q_g, na_k_g, na_rpb, hy_conv_w, hy_conv_b, hy_w1, hy_b1, hy_freq,
           hy_w2, hy_b2, hy_w3, hy_bias, w_up_na, w_up_hy, w_out, norm2_g, peer_wq, peer_subkeys, peer_u,
           peer_v):
    B, S, D = x.shape
    assert B == 2, "the long convolution packs the two batches as one complex sequence"
    depth = w_in.shape[0]
    T = B * S
    xt = x.reshape(T, D)

    fc = _fft_constants(S)
    n1 = fc["n1"]
    pos_ext, dec_ext = _hy_pos_tables(S)
    head_sum = jnp.asarray(np.kron(np.eye(NA_HEADS), np.ones((NA_HEAD_DIM, NA_HEAD_DIM))), BF16)
    o_qk, o_v, o_hy = 2 * NA_WIDTH, 3 * NA_WIDTH, 3 * NA_WIDTH + 3 * HY_WIDTH
    tc = 4096
    ncol = FFT_N2 * HY_WIDTH

    for l in range(depth):
        (hn,) = _rmsnorm(xt, norm1_g[l])
        w = w_in[l].astype(BF16)
        qk_gain = jnp.concatenate([jnp.tile(na_q_g[l], NA_HEADS) * (NA_HEAD_DIM ** -0.5),
                                   jnp.tile(na_k_g[l], NA_HEADS)]).reshape(1, o_qk).astype(F32)
        qk, v, hy, gates = _proj_in(hn, w, qk_gain, head_sum, gate_b[l].reshape(1, 2 * D),
                                    (o_qk, o_v - o_qk, o_hy - o_v, w.shape[1] - o_hy))

        a_out = _na(qk, v, _na_bias_table(na_rpb[l]), B, S)

        z, x0 = _hy_prep(hy, hy_conv_w[l], hy_conv_b[l], S)
        w1p = jnp.pad(hy_w1[l], ((0, HY_FFN_HIDDEN - HY_POS_DIM), (0, 0)))
        g = _hy_filter(S, pos_ext, dec_ext, w1p, hy_b1[l], hy_freq[l], hy_w2[l], hy_b2[l], hy_w3[l])
        gr, gi = _dft_pair(fc["ga"], fc["gb"], g.reshape(n1, ncol), tc)
        z2 = z.reshape(n1, ncol)
        ar, ai = _dft_pair(fc["za"], fc["zb"], z2, tc)
        cube = lambda t: t.reshape(n1, FFT_N2, HY_WIDTH)
        br, bi = _spec_conv(cube(ar), cube(ai), cube(gr), cube(gi), fc)
        bias_t = jnp.tile(hy_bias[l], tc // HY_WIDTH).reshape(1, tc)
        b_out = _idft_out(fc["ya"], fc["yb"], br.reshape(n1, ncol), bi.reshape(n1, ncol),
                          z2, x0.reshape(n1, ncol), bias_t, tc).reshape(T, HY_WIDTH)

        xt = _merge_out(a_out, b_out, w_up_na[l].astype(BF16), w_up_hy[l].astype(BF16), gates,
                        w_out[l].astype(BF16), xt)

        hn2, hn2T = _rmsnorm(xt, norm2_g[l], transposed=True)
        wq_heads = peer_wq[l].reshape(D, PEER_HEADS, 2 * PEER_HALF).transpose(1, 0, 2).astype(BF16)
        tabs = _peer_topk(hn2, wq_heads, peer_subkeys[l])
        xt = _peer_dense(hn2T, peer_u[l].astype(BF16), peer_v[l].T.astype(BF16), tabs, xt)
    return xt.reshape(B, S, D)
```

```python
import functools
import math

import numpy as np
import jax
import jax.numpy as jnp
from jax import lax
from jax.experimental import pallas as pl
from jax.experimental.pallas import tpu as pltpu

F32 = jnp.float32
BF16 = jnp.bfloat16

GRID_W = 64
NA_HEADS = 8
NA_HEAD_DIM = 64
NA_WIDTH = NA_HEADS * NA_HEAD_DIM
NA_KR = 8
NA_KW = 16
HY_WIDTH = 512
HY_POS_BANDS = 16
HY_POS_DIM = 1 + 2 * HY_POS_BANDS
HY_FFN_HIDDEN = 64
HY_DECAY_TARGET = 1e-2
HY_FAST_PCT = 0.3
HY_SLOW_PCT = 1.5
PEER_HEADS = 8
PEER_NKEYS = 128
PEER_TOPK = 16
PEER_HALF = 128
EPS = 1e-6
NEG = -1e30

LANES = 128
VMEM_LIMIT_BYTES = 56 * 1024 * 1024
FFT_N2 = 128
BF16_ROWS = 16


def _params(*sem):
    return pltpu.CompilerParams(dimension_semantics=sem, vmem_limit_bytes=VMEM_LIMIT_BYTES)


def _split(x):
    hi = x.astype(BF16)
    lo = (x - hi.astype(F32)).astype(BF16)
    return hi, lo


def _dot(a, b):
    return jnp.dot(a, b, preferred_element_type=F32)


def _dot3(ah, al, bh, bl):
    return _dot(ah, bh) + _dot(ah, bl) + _dot(al, bh)


def _rmsnorm_t_kernel(x_ref, g_ref, ot_ref):
    x = x_ref[...]
    y = x * lax.rsqrt(jnp.mean(x * x, axis=-1, keepdims=True) + EPS)
    ot_ref[...] = (y * g_ref[...]).T.astype(BF16)


def _rmsnorm_t(x, g, tm=1024):
    T, D = x.shape
    return pl.pallas_call(
        _rmsnorm_t_kernel,
        grid=(T // tm,),
        in_specs=[pl.BlockSpec((tm, D), lambda i: (i, 0)),
                  pl.BlockSpec((1, D), lambda i: (0, 0))],
        out_specs=pl.BlockSpec((D, tm), lambda i: (0, i)),
        out_shape=jax.ShapeDtypeStruct((D, T), BF16),
        compiler_params=_params("parallel"),
        name="rmsnorm_t",
    )(x, g.reshape(1, D))


def _epi_head_rmsnorm(acc, gain, head_sum):
    hi, lo = _split(acc * acc)
    ms = (_dot(hi, head_sum) + _dot(lo, head_sum)) * (1.0 / NA_HEAD_DIM)
    return acc * lax.rsqrt(ms + EPS) * gain


def _epi_gate(acc, bias):
    return jax.nn.sigmoid(acc + bias)


def _proj_in_kernel(jq, jv, jh, x_ref, n1g_ref, w_ref, gain_ref, hs_ref, gb_ref,
                    qk_ref, v_ref, hy_ref, g_ref, hn_ref):
    j = pl.program_id(1)

    @pl.when(j == 0)
    def _():
        x = x_ref[...]
        y = x * lax.rsqrt(jnp.mean(x * x, axis=-1, keepdims=True) + EPS)
        hn_ref[...] = (y * n1g_ref[...]).astype(BF16)

    acc = _dot(hn_ref[...], w_ref[...].astype(BF16))

    @pl.when(j < jq)
    def _():
        qk_ref[...] = _epi_head_rmsnorm(acc, gain_ref[...], hs_ref[...]).astype(qk_ref.dtype)

    @pl.when((j >= jq) & (j < jv))
    def _():
        v_ref[...] = acc.astype(v_ref.dtype)

    @pl.when((j >= jv) & (j < jh))
    def _():
        hy_ref[...] = acc

    @pl.when(j >= jh)
    def _():
        g_ref[...] = _epi_gate(acc, gb_ref[...]).astype(g_ref.dtype)


def _proj_in(x, norm_g, w_all, layer, qk_gain, head_sum, gate_b, widths, tm=2048, tn=512):
    T, D = x.shape
    tm = min(tm, T)
    wq, wv, wh, wg = widths
    assert all(x % tn == 0 for x in widths)
    jq, jv, jh, jn = wq // tn, (wq + wv) // tn, (wq + wv + wh) // tn, sum(widths) // tn
    rng = lambda lo, hi: (lambda i, j: (i, jnp.clip(j - lo, 0, hi - lo - 1)))
    crng = lambda lo, hi: (lambda i, j: (0, jnp.clip(j - lo, 0, hi - lo - 1)))
    blk = lambda f: pl.BlockSpec((tm, tn), f)
    return pl.pallas_call(
        functools.partial(_proj_in_kernel, jq, jv, jh),
        grid=(T // tm, jn),
        in_specs=[pl.BlockSpec((tm, D), lambda i, j: (i, 0)),
                  pl.BlockSpec((1, D), lambda i, j: (0, 0)),
                  pl.BlockSpec((None, D, tn), lambda i, j: (layer, 0, j)),
                  pl.BlockSpec((1, tn), crng(0, jq)),
                  pl.BlockSpec(head_sum.shape, lambda i, j: (0, 0)),
                  pl.BlockSpec((1, tn), crng(jh, jn))],
        out_specs=[blk(rng(0, jq)), blk(rng(jq, jv)), blk(rng(jv, jh)), blk(rng(jh, jn))],
        out_shape=[jax.ShapeDtypeStruct((T, wq), BF16), jax.ShapeDtypeStruct((T, wv), BF16),
                   jax.ShapeDtypeStruct((T, wh), F32), jax.ShapeDtypeStruct((T, wg), BF16)],
        scratch_shapes=[pltpu.VMEM((tm, D), BF16)],
        compiler_params=_params("parallel", "arbitrary"),
        name="proj_in",
    )(x, norm_g.reshape(1, D), w_all, qk_gain, head_sum, gate_b)


def _merge_out_kernel(a_ref, b_ref, wa_ref, wb_ref, g_ref, wo_ref, x_ref, o_ref):
    d = o_ref.shape[1]
    ya = _dot(a_ref[...], wa_ref[...].astype(BF16))
    yb = _dot(b_ref[...], wb_ref[...].astype(BF16))
    g = g_ref[...].astype(F32)
    merged = (g[:, :d] * ya + g[:, d:] * yb).astype(BF16)
    o_ref[...] = x_ref[...] + _dot(merged, wo_ref[...].astype(BF16))


def _merge_out(a_out, b_out, w_na, w_hy, gates, w_o, layer, x, tm=512):
    T, wa = a_out.shape
    D = w_na.shape[2]
    full = lambda a: pl.BlockSpec((None,) + a.shape[1:], lambda i: (layer, 0, 0))
    return pl.pallas_call(
        _merge_out_kernel,
        grid=(T // tm,),
        in_specs=[pl.BlockSpec((tm, wa), lambda i: (i, 0)),
                  pl.BlockSpec((tm, b_out.shape[1]), lambda i: (i, 0)),
                  full(w_na), full(w_hy),
                  pl.BlockSpec((tm, 2 * D), lambda i: (i, 0)),
                  full(w_o),
                  pl.BlockSpec((tm, D), lambda i: (i, 0))],
        out_specs=pl.BlockSpec((tm, D), lambda i: (i, 0)),
        out_shape=jax.ShapeDtypeStruct((T, D), F32),
        compiler_params=_params("parallel"),
        name="merge_out",
    )(a_out, b_out, w_na, w_hy, gates, w_o, x)


def _na_kernel(rows, rb, q_ref, k_ref, v_ref, bias_ref, o_ref, s_ref, p_ref):
    blk = pl.program_id(2)
    lane = lax.broadcasted_iota(jnp.int32, (GRID_W, LANES), 1)
    nk = NA_KR * GRID_W
    starts = []
    for i in range(rb):
        r = blk * rb + i
        r0 = jnp.clip(r - NA_KR // 2, 0, rows - NA_KR)
        dr0 = r0 - r + (NA_KR - 1)
        start = pl.multiple_of(r0 * GRID_W, GRID_W)
        starts.append(start)
        qr = q_ref[i * GRID_W:(i + 1) * GRID_W, :]
        kw = k_ref[pl.ds(start, nk), :]
        for hh in range(2):
            in_head = (lane >= NA_HEAD_DIM) if hh else (lane < NA_HEAD_DIM)
            qm = jnp.where(in_head, qr, jnp.zeros_like(qr))
            s = lax.dot_general(qm, kw, (((1,), (1,)), ((), ())), preferred_element_type=F32)
            s_ref[(2 * i + hh) * GRID_W:(2 * i + hh + 1) * GRID_W, :] = s + bias_ref[hh, dr0]
    s = s_ref[...]
    p = jnp.exp(s - jnp.max(s, axis=-1, keepdims=True))
    p_ref[...] = (p * (1.0 / jnp.sum(p, axis=-1, keepdims=True))).astype(BF16)
    for i in range(rb):
        vw = v_ref[pl.ds(starts[i], nk), :]
        o0 = _dot(p_ref[(2 * i) * GRID_W:(2 * i + 1) * GRID_W, :], vw)
        o1 = _dot(p_ref[(2 * i + 1) * GRID_W:(2 * i + 2) * GRID_W, :], vw)
        o_ref[i * GRID_W:(i + 1) * GRID_W, :] = jnp.where(lane < NA_HEAD_DIM, o0, o1).astype(o_ref.dtype)


def _na_bias_table(rpb):
    c = np.arange(GRID_W)
    c0 = np.clip(c - NA_KW // 2, 0, GRID_W - NA_KW)
    col_in = (c[None, :] >= c0[:, None]) & (c[None, :] < c0[:, None] + NA_KW)
    dc_idx = np.clip(c[None, :] - c[:, None] + (NA_KW - 1), 0, 2 * NA_KW - 2)
    onehot = jnp.asarray((dc_idx[:, :, None] == np.arange(2 * NA_KW - 1)).astype(np.float32))
    rows = jnp.stack([rpb.astype(F32)[:, d:d + NA_KR] for d in range(NA_KR)], axis=1)
    b = jnp.einsum("hdjc,qkc->hdqjk", rows, onehot, precision=lax.Precision.HIGHEST)
    b = jnp.where(jnp.asarray(col_in)[None, None, :, None, :], b, NEG)
    return b.reshape(NA_HEADS, NA_KR, GRID_W, NA_KR * GRID_W)


def _na(qk, v, bias, B, S, rb=16):
    rows = S // GRID_W
    assert rows >= NA_KR and rows % rb == 0
    nblk = rows // rb
    tq = rb * GRID_W
    kofs = NA_WIDTH // LANES
    return pl.pallas_call(
        functools.partial(_na_kernel, rows, rb),
        grid=(B, NA_HEADS // 2, nblk),
        in_specs=[pl.BlockSpec((tq, LANES), lambda b, p, r: (b * nblk + r, p)),
                  pl.BlockSpec((S, LANES), lambda b, p, r: (b, kofs + p)),
                  pl.BlockSpec((S, LANES), lambda b, p, r: (b, p)),
                  pl.BlockSpec((2, NA_KR, GRID_W, NA_KR * GRID_W), lambda b, p, r: (p, 0, 0, 0))],
        out_specs=pl.BlockSpec((tq, LANES), lambda b, p, r: (b * nblk + r, p)),
        out_shape=jax.ShapeDtypeStruct((B * S, NA_WIDTH), BF16),
        scratch_shapes=[pltpu.VMEM((2 * tq, NA_KR * GRID_W), F32), pltpu.VMEM((2 * tq, NA_KR * GRID_W), BF16)],
        compiler_params=_params("parallel", "parallel", "parallel"),
        name="na_attention",
    )(qk, qk, v, bias)


def _hy_prep_kernel(S, hy_ref, prev_ref, next_ref, w_ref, b_ref, z_ref, x0_ref):
    tm, C = hy_ref.shape
    tok0 = pl.program_id(0) * tm
    h = hy_ref[...]
    row = lax.broadcasted_iota(jnp.int32, (tm, C), 0)
    prev_row = jnp.where(tok0 % S == 0, 0.0, prev_ref[7:8, :])
    next_row = jnp.where((tok0 + tm) % S == 0, 0.0, next_ref[0:1, :])
    up = jnp.where(row == 0, prev_row, pltpu.roll(h, 1, axis=0))
    dn = jnp.where(row == tm - 1, next_row, pltpu.roll(h, tm - 1, axis=0))
    u = up * w_ref[0:1, :] + h * w_ref[1:2, :] + dn * w_ref[2:3, :] + b_ref[...]
    c = C // 3
    x0_ref[...] = u[:, :c]
    z_ref[...] = u[:, 2 * c:] * u[:, c:2 * c]


def _hy_prep(hy, conv_w, conv_b, S, tm=1024):
    T, C = hy.shape
    nb = tm // 8
    last = T // 8 - 1
    return pl.pallas_call(
        functools.partial(_hy_prep_kernel, S),
        grid=(T // tm,),
        in_specs=[pl.BlockSpec((tm, C), lambda i: (i, 0)),
                  pl.BlockSpec((8, C), lambda i: (jnp.maximum(i * nb - 1, 0), 0)),
                  pl.BlockSpec((8, C), lambda i: (jnp.minimum((i + 1) * nb, last), 0)),
                  pl.BlockSpec((3, C), lambda i: (0, 0)),
                  pl.BlockSpec((1, C), lambda i: (0, 0))],
        out_specs=[pl.BlockSpec((tm, C // 3), lambda i: (i, 0)),
                   pl.BlockSpec((tm, C // 3), lambda i: (i, 0))],
        out_shape=[jax.ShapeDtypeStruct((T, C // 3), F32), jax.ShapeDtypeStruct((T, C // 3), F32)],
        compiler_params=_params("parallel"),
        name="hyena_prep",
    )(hy, hy, hy, conv_w, conv_b.reshape(1, C))


def _hy_filter_kernel(L, pos_ref, dec_ref, w1_ref, b1_ref, fr_ref, w2_ref, b2_ref, w3_ref, g_ref):
    tr, C = g_ref.shape
    hr = tr // 2
    fr = fr_ref[...]
    w1h, w1l = _split(w1_ref[...])
    w2h, w2l = _split(w2_ref[...])
    w3h, w3l = _split(w3_ref[...])
    ph, plo = _split(jnp.concatenate([pos_ref[0:hr, :], pos_ref[hr:tr, :]], axis=1))
    h = jnp.sin(fr * (_dot3(ph, plo, w1h, w1l) + _dot(plo, w1l) + b1_ref[...]))
    hh, hl = _split(h)
    h = jnp.sin(fr * (_dot3(hh, hl, w2h, w2l) + _dot(hl, w2l) + b2_ref[...]))
    hh, hl = _split(h)
    h = _dot3(hh, hl, w3h, w3l) + _dot(hl, w3l)
    for half in range(2):
        rows = slice(half * hr, (half + 1) * hr)
        dec = jnp.exp(-(pos_ref[rows, 0:1] * dec_ref[...]))
        hf = h[:, (2 * half) * C:(2 * half + 1) * C] * dec
        hb = h[:, (2 * half + 1) * C:(2 * half + 2) * C] * dec
        n = pl.program_id(0) * tr + half * hr + lax.broadcasted_iota(jnp.int32, (hr, C), 0)
        g = jnp.where(n < L, hf, hb)
        g = jnp.where(n == L, 0.0, g)
        g_ref[rows, :] = jnp.where(n == 0, hf + hb, g)


def _hy_filter(L, pos_ext, dec_ext, w1p, b1, freq, w2, b2, w3, tr=1024):
    N = 2 * L
    tr = min(tr, N)
    full = lambda a: pl.BlockSpec(a.shape, lambda i: (0, 0))
    twice = lambda w: jnp.kron(jnp.eye(2, dtype=w.dtype), w)
    row2 = lambda b: jnp.tile(b, 2).reshape(1, -1)
    ws = (twice(w1p), row2(b1), row2(freq), twice(w2), row2(b2), twice(w3))
    return pl.pallas_call(
        functools.partial(_hy_filter_kernel, L),
        grid=(N // tr,),
        in_specs=[pl.BlockSpec((tr, pos_ext.shape[1]), lambda i: (i, 0)), full(dec_ext)]
                 + [full(w) for w in ws],
        out_specs=pl.BlockSpec((tr, HY_WIDTH), lambda i: (i, 0)),
        out_shape=jax.ShapeDtypeStruct((N, HY_WIDTH), F32),
        compiler_params=_params("parallel"),
        name="hyena_filter",
    )(pos_ext, dec_ext, *ws)


def _dft_pair_kernel(ma_ref, mb_ref, x_ref, or_ref, oi_ref):
    xh, xl = _split(x_ref[...])
    or_ref[...] = _dot3(ma_ref[0], ma_ref[1], xh, xl)
    oi_ref[...] = _dot3(mb_ref[0], mb_ref[1], xh, xl)


def _dft_pair(ma, mb, x, tc=2048):
    R = ma.shape[1]
    K, C = x.shape
    return pl.pallas_call(
        _dft_pair_kernel,
        grid=(C // tc,),
        in_specs=[pl.BlockSpec(ma.shape, lambda j: (0, 0, 0)),
                  pl.BlockSpec(mb.shape, lambda j: (0, 0, 0)),
                  pl.BlockSpec((K, tc), lambda j: (0, j))],
        out_specs=[pl.BlockSpec((R, tc), lambda j: (0, j)), pl.BlockSpec((R, tc), lambda j: (0, j))],
        out_shape=[jax.ShapeDtypeStruct((R, C), F32), jax.ShapeDtypeStruct((R, C), F32)],
        compiler_params=_params("parallel"),
        name="dft_outer",
    )(ma, mb, x)


def _spec_conv_kernel(ar_ref, ai_ref, fr_ref, fi_ref, twc_ref, tws_ref, wf_ref, wi_ref, br_ref, bi_ref):
    kb, n2, C = ar_ref.shape
    wfh, wfl, wih, wil = wf_ref[0], wf_ref[1], wi_ref[0], wi_ref[1]
    for k in range(kb):
        c = jnp.concatenate([twc_ref[k]] * (C // LANES), axis=1)
        s = jnp.concatenate([tws_ref[k]] * (C // LANES), axis=1)

        def inner_dft(re, im):
            h, l = _split(jnp.concatenate([re * c + im * s, im * c - re * s], axis=0))
            return _dot3(wfh, wfl, h, l)

        x = inner_dft(ar_ref[k], ai_ref[k])
        g = inner_dft(fr_ref[k], fi_ref[k])
        xr, xi, gr, gi = x[:n2], x[n2:], g[:n2], g[n2:]
        yh, yl = _split(jnp.concatenate([xr * gr - xi * gi, xr * gi + xi * gr], axis=0))
        b = _dot3(wih, wil, yh, yl)
        br, bi = b[:n2], b[n2:]
        br_ref[k] = br * c - bi * s
        bi_ref[k] = bi * c + br * s


def _spec_conv(ar, ai, fr, fi, fc, kb=8):
    n1, n2, C = ar.shape
    kb = min(kb, n1)
    blk = pl.BlockSpec((kb, n2, C), lambda k: (k, 0, 0))
    tblk = pl.BlockSpec((kb, n2, LANES), lambda k: (k, 0, 0))
    wblk = pl.BlockSpec((2, 2 * n2, 2 * n2), lambda k: (0, 0, 0))
    return pl.pallas_call(
        _spec_conv_kernel,
        grid=(n1 // kb,),
        in_specs=[blk, blk, blk, blk, tblk, tblk, wblk, wblk],
        out_specs=[blk, blk],
        out_shape=[jax.ShapeDtypeStruct((n1, n2, C), F32), jax.ShapeDtypeStruct((n1, n2, C), F32)],
        compiler_params=_params("parallel"),
        name="spectrum_conv",
    )(ar, ai, fr, fi, fc["twc"], fc["tws"], fc["wf"], fc["wi"])


def _idft_out_kernel(ma_ref, mb_ref, br_ref, bi_ref, z_ref, x0_ref, bias_ref, o_ref):
    brh, brl = _split(br_ref[...])
    bih, bil = _split(bi_ref[...])
    y = _dot3(ma_ref[0], ma_ref[1], brh, brl) + _dot3(mb_ref[0], mb_ref[1], bih, bil)
    o_ref[...] = ((y + z_ref[...] * bias_ref[...]) * x0_ref[...]).astype(o_ref.dtype)


def _idft_out(ma, mb, br, bi, z2, x02, bias_t, tc=2048):
    R = ma.shape[1]
    K, C = br.shape
    col = lambda rws: pl.BlockSpec((rws, tc), lambda j: (0, j))
    return pl.pallas_call(
        _idft_out_kernel,
        grid=(C // tc,),
        in_specs=[pl.BlockSpec(ma.shape, lambda j: (0, 0, 0)), pl.BlockSpec(mb.shape, lambda j: (0, 0, 0)),
                  col(K), col(K), col(R), col(R), pl.BlockSpec((1, tc), lambda j: (0, 0))],
        out_specs=col(R),
        out_shape=jax.ShapeDtypeStruct((R, C), BF16),
        compiler_params=_params("parallel"),
        name="idft_outer",
    )(ma, mb, br, bi, z2, x02, bias_t)


def _split_const(m):
    m32 = jnp.asarray(np.asarray(m, np.float32))
    hi, lo = _split(m32)
    return jnp.stack([hi, lo])


def _fft_constants(L):
    N = 2 * L
    n2 = FFT_N2
    n1 = N // n2
    h = n1 // 2
    k1 = np.arange(n1)[:, None].astype(np.float64)
    ang = 2.0 * np.pi * k1 * np.arange(n1)[None, :] / n1
    c, s = np.cos(ang), np.sin(ang)
    za = np.concatenate([c[:, :h], s[:, :h]], axis=1)
    zb = np.concatenate([-s[:, :h], c[:, :h]], axis=1)
    ga, gb = c, -s
    ya = np.concatenate([c[:h], s[:h]], axis=0)
    yb = np.concatenate([-s[:h], c[:h]], axis=0)
    kk = np.arange(n2)[:, None].astype(np.float64)
    nn = np.arange(n2)[None, :].astype(np.float64)
    base = 2.0 * np.pi * kk * nn / n2
    cb, sb = np.cos(base), np.sin(base)
    wf = np.block([[cb, sb], [-sb, cb]])
    wi = np.block([[cb.T, -sb.T], [sb.T, cb.T]]) / N
    tw = 2.0 * np.pi * np.arange(n1)[:, None] * nn / N
    rep = lambda m: jnp.broadcast_to(jnp.asarray(np.asarray(m, np.float32))[:, :, None], (n1, n2, LANES))
    sp = lambda m: _split_const(m)
    return dict(za=sp(za), zb=sp(zb), ga=sp(ga), gb=sp(gb), ya=sp(ya), yb=sp(yb), wf=sp(wf), wi=sp(wi),
                twc=rep(np.cos(tw)), tws=rep(np.sin(tw)), n1=n1)


def _hy_pos_tables(L):
    ext = lambda a: jnp.concatenate([a, a[0:1], jnp.flip(a[1:], axis=0)], axis=0)
    t = ext(jnp.linspace(0.0, 1.0, L, dtype=F32)[:, None])
    w = ext(2.0 * math.pi * jnp.arange(L, dtype=F32)[:, None] / L)
    f = jnp.linspace(1e-4, HY_POS_BANDS - 1, HY_POS_BANDS, dtype=F32)[None, :]
    z = jnp.concatenate([t, jnp.cos(f * w), -jnp.sin(f * w)], axis=-1)
    max_decay = math.log(HY_DECAY_TARGET) / HY_FAST_PCT
    min_decay = math.log(HY_DECAY_TARGET) / HY_SLOW_PCT
    deltas = jnp.linspace(min_decay, max_decay, HY_WIDTH, dtype=F32)
    zp = jnp.pad(z, ((0, 0), (0, HY_FFN_HIDDEN - HY_POS_DIM)))
    return zp, jnp.abs(deltas)[None, :]


_TRI_ROWS = 16 + 8 * 7 + 8


def _tri_tables(tm):
    flat = [b for b in range(16)]
    for a in range(1, 8):
        flat += [16 * a + b for b in range(8)]
    flat += [16 * a for a in range(8, 16)]
    return jnp.asarray(np.tile(np.asarray(flat, np.float32)[:, None], (1, tm)))


def _sort16_pairs():
    def merge(lo, hi, r):
        step = r * 2
        if step < hi - lo:
            yield from merge(lo, hi, step)
            yield from merge(lo + r, hi, step)
            yield from [(i, i + r) for i in range(lo + r, hi - r, step)]
        else:
            yield (lo, lo + r)

    def sort(lo, hi):
        if hi - lo >= 1:
            mid = lo + (hi - lo) // 2
            yield from sort(lo, mid)
            yield from sort(mid + 1, hi)
            yield from merge(lo, hi, 1)
    return list(sort(0, 15))


_SORT16 = _sort16_pairs()
_BITONIC16 = [(i, i + d) for d in (8, 4, 2, 1) for i in range(16) if not i & d]


def _cmpx(x, pairs):
    x = list(x)
    for i, j in pairs:
        x[i], x[j] = jnp.maximum(x[i], x[j]), jnp.minimum(x[i], x[j])
    return x


def _top16_sorted(s):
    n, tm = s.shape
    sub = n // PEER_TOPK
    x = _cmpx([s[r * sub:(r + 1) * sub] for r in range(PEER_TOPK)], _SORT16)
    shift = sub // 2
    while shift:
        y = [pltpu.roll(v, shift, axis=0) for v in x]
        x = _cmpx([jnp.maximum(x[i], y[PEER_TOPK - 1 - i]) for i in range(PEER_TOPK)], _BITONIC16)
        shift //= 2
    rowk = lax.broadcasted_iota(jnp.int32, (PEER_TOPK, tm), 0)
    vals = jnp.zeros((PEER_TOPK, tm), F32)
    for r in range(PEER_TOPK):
        vals = jnp.where(rowk == r, jnp.concatenate([x[r], x[r]], axis=0), vals)
    return vals


def _topk_rows(scores, k, exact):
    n, tm = scores[0].shape
    if not exact:
        return [(_top16_sorted(s), None) for s in scores]
    rowk = lax.broadcasted_iota(jnp.int32, (k, tm), 0)
    rowf = lax.broadcasted_iota(jnp.int32, (n, tm), 0).astype(F32)
    state = [(s, jnp.zeros((k, tm), F32), jnp.full((n, tm), float(k), F32)) for s in scores]
    for r in range(k):
        nxt = []
        for s, vals, rank in state:
            m = jnp.max(s, axis=0, keepdims=True)
            sel = rowf == jnp.min(jnp.where(s == m, rowf, float(n)), axis=0, keepdims=True)
            nxt.append((jnp.where(sel, -jnp.inf, s), jnp.where(rowk == r, m, vals),
                        jnp.where(sel, float(r), rank)))
        state = nxt
    return [(vals, rank) for _, vals, rank in state]


def _stair_cells(t1, t2, op):
    pieces = [op(t1[0:1], t2)]
    pieces += [op(t1[a:a + 1], t2[0:8]) for a in range(1, 8)]
    pieces += [op(t1[8:16], t2[0:1])]
    return jnp.concatenate(pieces, axis=0)


def _peer_tables(scores, flat, exact):
    K = PEER_TOPK
    tm = scores[0].shape[1]
    (sv1, rank1), (sv2, rank2) = _topk_rows(scores, K, exact)
    cand = _stair_cells(sv1, sv2, lambda x, y: x + y)
    if exact:
        row16 = lax.broadcasted_iota(jnp.int32, (K, tm), 0).astype(F32)
        length = jnp.zeros((K, tm), F32)
        for r in range(K):
            m = jnp.max(cand, axis=0, keepdims=True)
            f = jnp.min(jnp.where(cand == m, flat, 1e9), axis=0, keepdims=True)
            cand = jnp.where(flat == f, -jnp.inf, cand)
            length = jnp.where(row16 == jnp.floor(f * (1.0 / K)), length + 1.0, length)
        picked = cand == -jnp.inf
    else:
        for r in range(K):
            cand = jnp.where(cand == jnp.max(cand, axis=0, keepdims=True), -jnp.inf, cand)
        picked = cand == -jnp.inf
        cnt = jnp.where(picked, 1.0, 0.0)
        rows = [jnp.sum(cnt[0:16], axis=0, keepdims=True)]
        rows += [jnp.sum(cnt[8 + 8 * a:16 + 8 * a], axis=0, keepdims=True) for a in range(1, 8)]
        length = jnp.concatenate(rows + [cnt[72:80]], axis=0)
    s1, s2 = scores
    lfull = jnp.zeros_like(s1)
    if exact:
        ok = jnp.ones((1, tm), jnp.bool_)
        for r in range(K):
            lfull = jnp.where(rank1 == float(r), length[r:r + 1], lfull)
    else:
        rank2 = jnp.zeros_like(s2)
        for r in range(K):
            lfull = jnp.where(s1 == sv1[r:r + 1], length[r:r + 1], lfull)
            rank2 = jnp.where(sv2[r:r + 1] > s2, float(r + 1), rank2)
        reach = lambda s, sv: jnp.sum(jnp.where(s >= sv[K - 1:K], 1.0, 0.0), axis=0, keepdims=True)
        ok = (reach(s1, sv1) == float(K)) & (reach(s2, sv2) == float(K)) & \
             (jnp.sum(length, axis=0, keepdims=True) == float(K))
    es1 = jnp.exp(sv1 - sv1[0:1])
    es2 = jnp.exp(sv2 - sv2[0:1])
    ecand = _stair_cells(es1, es2, lambda x, y: x * y)
    z = jnp.sum(jnp.where(picked, ecand, 0.0), axis=0, keepdims=True)
    e1 = jnp.exp(scores[0] - sv1[0:1]) / z
    e2 = jnp.exp(scores[1] - sv2[0:1])
    return (lfull, e1, rank2, e2), jnp.where(ok, 1.0, 0.0)


def _peer_topk_kernel(hnT_ref, wq_ref, sk_ref, flat_ref, l_ref, e1_ref, rb_ref, e2_ref):
    flat = flat_ref[...]

    def head(h, carry):
        qT = _dot(wq_ref[h], hnT_ref[...])
        scores = []
        for p in range(2):
            qh, ql = _split(qT[p * PEER_HALF:(p + 1) * PEER_HALF])
            kh, kl = _split(sk_ref[h, p])
            scores.append(_dot3(kh, kl, qh, ql))

        def write(tabs, cs):
            l_ref[h, :, cs], e1_ref[h, :, cs] = tabs[0], tabs[1]
            rb_ref[h, :, cs], e2_ref[h, :, cs] = tabs[2].astype(rb_ref.dtype), tabs[3].astype(e2_ref.dtype)

        wd = flat.shape[1]
        for c in range(qT.shape[1] // wd):
            cs = slice(c * wd, (c + 1) * wd)
            sc = [s[:, cs] for s in scores]
            tabs, ok = _peer_tables(sc, flat, exact=False)
            write(tabs, cs)

            @pl.when(jnp.min(ok) < 0.5)
            def _():
                write(_peer_tables(sc, flat, exact=True)[0], cs)
        return carry

    lax.fori_loop(0, PEER_HEADS, head, 0)


def _peer_topk(hnT, wq_heads, subkeys, tm=512, wd=256):
    D, T = hnT.shape
    H = wq_heads.shape[0]
    tm = min(tm, T)
    wd = min(wd, tm)
    out = jax.ShapeDtypeStruct((H, PEER_NKEYS, T), F32)
    outb = jax.ShapeDtypeStruct((H, PEER_NKEYS, T), BF16)
    oblk = pl.BlockSpec((H, PEER_NKEYS, tm), lambda i: (0, 0, i))
    return pl.pallas_call(
        _peer_topk_kernel,
        grid=(T // tm,),
        in_specs=[pl.BlockSpec((D, tm), lambda i: (0, i)),
                  pl.BlockSpec(wq_heads.shape, lambda i: (0, 0, 0)),
                  pl.BlockSpec((H, 2, PEER_NKEYS, PEER_HALF), lambda i: (0, 0, 0, 0)),
                  pl.BlockSpec((_TRI_ROWS, wd), lambda i: (0, 0))],
        out_specs=[oblk, oblk, oblk, oblk],
        out_shape=[out, out, outb, outb],
        compiler_params=_params("parallel"),
        name="peer_topk",
    )(hnT, wq_heads, subkeys, _tri_tables(wd))


def _gelu_tanh(x):
    c = math.sqrt(2.0 / math.pi)
    h = 0.5 * x
    return h + h * jnp.tanh(x * (c + (c * 0.044715) * (x * x)))


def _peer_dense_kernel(ni, hnT_ref, u_ref, vT_ref, l_ref, e1_ref, rb_ref, e2_ref, x_ref, o_ref,
                       acc_ref, act_ref, a_ref):
    j = pl.program_id(1)
    nk = PEER_NKEYS

    @pl.when(j == 0)
    def _():
        acc_ref[...] = jnp.zeros_like(acc_ref)

    tm = hnT_ref.shape[1]
    i0 = pl.multiple_of(j * ni, ni)
    zero = jnp.zeros((), BF16)
    a_ref[...] = _dot(u_ref[...], hnT_ref[...])
    for ii in range(ni):
        w = None
        for h in range(PEER_HEADS):
            lrow = jnp.broadcast_to(l_ref[h, pl.ds(i0 + ii, 1), :], (BF16_ROWS, tm)).astype(BF16)
            erow = jnp.broadcast_to(e1_ref[h, pl.ds(i0 + ii, 1), :], (BF16_ROWS, tm)).astype(BF16)
            wh = jnp.where(rb_ref[h] < lrow[None], e2_ref[h], zero) * erow[None]
            w = wh if w is None else w + wh
        a = a_ref[ii * nk:(ii + 1) * nk, :].astype(BF16)
        act_ref[ii * nk:(ii + 1) * nk, :] = _gelu_tanh(a) * w.reshape(nk, tm)
    acc_ref[...] += _dot(vT_ref[...], act_ref[...])

    @pl.when(j == pl.num_programs(1) - 1)
    def _():
        o_ref[...] = x_ref[...] + acc_ref[...].T


def _peer_dense(hnT, u_bf, vT_bf, tabs, x, tm=512, ni=16):
    D, T = hnT.shape
    NE = u_bf.shape[0]
    te = ni * PEER_NKEYS
    tm = min(tm, T)
    tab = pl.BlockSpec((PEER_HEADS, PEER_NKEYS, tm), lambda i, j: (0, 0, i))
    grp = PEER_NKEYS // BF16_ROWS
    tabb = pl.BlockSpec((PEER_HEADS, grp, BF16_ROWS, tm), lambda i, j: (0, 0, 0, i))
    lt, e1, rb, e2 = tabs
    rb, e2 = [t.reshape(PEER_HEADS, grp, BF16_ROWS, T) for t in (rb, e2)]
    return pl.pallas_call(
        functools.partial(_peer_dense_kernel, ni),
        grid=(T // tm, NE // te),
        in_specs=[pl.BlockSpec((D, tm), lambda i, j: (0, i)),
                  pl.BlockSpec((te, D), lambda i, j: (j, 0)),
                  pl.BlockSpec((D, te), lambda i, j: (0, j)),
                  tab, tab, tabb, tabb,
                  pl.BlockSpec((tm, D), lambda i, j: (i, 0))],
        out_specs=pl.BlockSpec((tm, D), lambda i, j: (i, 0)),
        out_shape=jax.ShapeDtypeStruct((T, D), F32),
        scratch_shapes=[pltpu.VMEM((D, tm), F32), pltpu.VMEM((te, tm), BF16), pltpu.VMEM((te, tm), F32)],
        compiler_params=_params("parallel", "arbitrary"),
        name="peer_experts",
    )(hnT, u_bf, vT_bf, lt, e1, rb, e2, x)


def kernel(x, norm1_g, w_in, gate_b, na_q_g, na_k_g, na_rpb, hy_conv_w, hy_conv_b, hy_w1, hy_b1, hy_freq,
           hy_w2, hy_b2, hy_w3, hy_bias, w_up_na, w_up_hy, w_out, norm2_g, peer_wq, peer_subkeys, peer_u,
           peer_v):
    B, S, D = x.shape
    assert B == 2, "the long convolution packs the two batches as one complex sequence"
    depth = w_in.shape[0]
    T = B * S
    xt = x.reshape(T, D)

    fc = _fft_constants(S)
    n1 = fc["n1"]
    pos_ext, dec_ext = _hy_pos_tables(S)
    head_sum = jnp.asarray(np.kron(np.eye(NA_HEADS), np.ones((NA_HEAD_DIM, NA_HEAD_DIM))), BF16)
    o_qk, o_v, o_hy = 2 * NA_WIDTH, 3 * NA_WIDTH, 3 * NA_WIDTH + 3 * HY_WIDTH
    tc = 4096
    ncol = FFT_N2 * HY_WIDTH

    for l in range(depth):
        qk_gain = jnp.concatenate([jnp.tile(na_q_g[l], NA_HEADS) * (NA_HEAD_DIM ** -0.5),
                                   jnp.tile(na_k_g[l], NA_HEADS)]).reshape(1, o_qk).astype(F32)
        qk, v, hy, gates = _proj_in(xt, norm1_g[l], w_in, l, qk_gain, head_sum, gate_b[l].reshape(1, 2 * D),
                                    (o_qk, o_v - o_qk, o_hy - o_v, w_in.shape[2] - o_hy))

        a_out = _na(qk, v, _na_bias_table(na_rpb[l]), B, S)

        z, x0 = _hy_prep(hy, hy_conv_w[l], hy_conv_b[l], S)
        w1p = jnp.pad(hy_w1[l], ((0, HY_FFN_HIDDEN - HY_POS_DIM), (0, 0)))
        g = _hy_filter(S, pos_ext, dec_ext, w1p, hy_b1[l], hy_freq[l], hy_w2[l], hy_b2[l], hy_w3[l])
        gr, gi = _dft_pair(fc["ga"], fc["gb"], g.reshape(n1, ncol), tc)
        z2 = z.reshape(n1, ncol)
        ar, ai = _dft_pair(fc["za"], fc["zb"], z2, tc)
        cube = lambda t: t.reshape(n1, FFT_N2, HY_WIDTH)
        br, bi = _spec_conv(cube(ar), cube(ai), cube(gr), cube(gi), fc)
        bias_t = jnp.tile(hy_bias[l], tc // HY_WIDTH).reshape(1, tc)
        b_out = _idft_out(fc["ya"], fc["yb"], br.reshape(n1, ncol), bi.reshape(n1, ncol),
                          z2, x0.reshape(n1, ncol), bias_t, tc).reshape(T, HY_WIDTH)

        xt = _merge_out(a_out, b_out, w_up_na, w_up_hy, gates, w_out, l, xt)

        hn2T = _rmsnorm_t(xt, norm2_g[l])
        wq_heads = peer_wq[l].reshape(D, PEER_HEADS, 2 * PEER_HALF).transpose(1, 2, 0).astype(BF16)
        tabs = _peer_topk(hn2T, wq_heads, peer_subkeys[l])
        xt = _peer_dense(hn2T, peer_u[l].astype(BF16), peer_v[l].T.astype(BF16), tabs, xt)
    return xt.reshape(B, S, D)
```

```python
import functools
import math

import numpy as np
import jax
import jax.numpy as jnp
from jax import lax
from jax.experimental import pallas as pl
from jax.experimental.pallas import tpu as pltpu

F32 = jnp.float32
BF16 = jnp.bfloat16

GRID_W = 64
NA_HEADS = 8
NA_HEAD_DIM = 64
NA_WIDTH = NA_HEADS * NA_HEAD_DIM
NA_KR = 8
NA_KW = 16
HY_WIDTH = 512
HY_POS_BANDS = 16
HY_POS_DIM = 1 + 2 * HY_POS_BANDS
HY_FFN_HIDDEN = 64
HY_DECAY_TARGET = 1e-2
HY_FAST_PCT = 0.3
HY_SLOW_PCT = 1.5
PEER_HEADS = 8
PEER_NKEYS = 128
PEER_TOPK = 16
PEER_HALF = 128
EPS = 1e-6
NEG = -1e30

LANES = 128
VMEM_LIMIT_BYTES = 56 * 1024 * 1024
FFT_N2 = 128
BF16_ROWS = 16


def _params(*sem):
    return pltpu.CompilerParams(dimension_semantics=sem, vmem_limit_bytes=VMEM_LIMIT_BYTES)


def _split(x):
    hi = x.astype(BF16)
    lo = (x - hi.astype(F32)).astype(BF16)
    return hi, lo


def _dot(a, b):
    return jnp.dot(a, b, preferred_element_type=F32)


def _dot3(ah, al, bh, bl):
    return _dot(ah, bh) + _dot(ah, bl) + _dot(al, bh)


def _rmsnorm_t_kernel(x_ref, g_ref, ot_ref):
    x = x_ref[...]
    y = x * lax.rsqrt(jnp.mean(x * x, axis=-1, keepdims=True) + EPS)
    ot_ref[...] = (y * g_ref[...]).T.astype(BF16)


def _rmsnorm_t(x, g, tm=1024):
    T, D = x.shape
    return pl.pallas_call(
        _rmsnorm_t_kernel,
        grid=(T // tm,),
        in_specs=[pl.BlockSpec((tm, D), lambda i: (i, 0)),
                  pl.BlockSpec((1, D), lambda i: (0, 0))],
        out_specs=pl.BlockSpec((D, tm), lambda i: (0, i)),
        out_shape=jax.ShapeDtypeStruct((D, T), BF16),
        compiler_params=_params("parallel"),
        name="rmsnorm_t",
    )(x, g.reshape(1, D))


def _epi_head_rmsnorm(acc, gain, head_sum):
    hi, lo = _split(acc * acc)
    ms = (_dot(hi, head_sum) + _dot(lo, head_sum)) * (1.0 / NA_HEAD_DIM)
    return acc * lax.rsqrt(ms + EPS) * gain


def _epi_gate(acc, bias):
    return jax.nn.sigmoid(acc + bias)


def _proj_in_kernel(jq, jv, jh, x_ref, n1g_ref, w_ref, gain_ref, hs_ref, gb_ref,
                    qk_ref, v_ref, hy_ref, g_ref, hn_ref):
    j = pl.program_id(1)

    @pl.when(j == 0)
    def _():
        x = x_ref[...]
        y = x * lax.rsqrt(jnp.mean(x * x, axis=-1, keepdims=True) + EPS)
        hn_ref[...] = (y * n1g_ref[...]).astype(BF16)

    acc = _dot(hn_ref[...], w_ref[...].astype(BF16))

    @pl.when(j < jq)
    def _():
        qk_ref[...] = _epi_head_rmsnorm(acc, gain_ref[...], hs_ref[...]).astype(qk_ref.dtype)

    @pl.when((j >= jq) & (j < jv))
    def _():
        v_ref[...] = acc.astype(v_ref.dtype)

    @pl.when((j >= jv) & (j < jh))
    def _():
        hy_ref[...] = acc

    @pl.when(j >= jh)
    def _():
        g_ref[...] = _epi_gate(acc, gb_ref[...]).astype(g_ref.dtype)


def _proj_in(x, norm_g, w_all, layer, qk_gain, head_sum, gate_b, widths, tm=2048, tn=512):
    T, D = x.shape
    tm = min(tm, T)
    wq, wv, wh, wg = widths
    assert all(x % tn == 0 for x in widths)
    jq, jv, jh, jn = wq // tn, (wq + wv) // tn, (wq + wv + wh) // tn, sum(widths) // tn
    rng = lambda lo, hi: (lambda i, j: (i, jnp.clip(j - lo, 0, hi - lo - 1)))
    crng = lambda lo, hi: (lambda i, j: (0, jnp.clip(j - lo, 0, hi - lo - 1)))
    blk = lambda f: pl.BlockSpec((tm, tn), f)
    return pl.pallas_call(
        functools.partial(_proj_in_kernel, jq, jv, jh),
        grid=(T // tm, jn),
        in_specs=[pl.BlockSpec((tm, D), lambda i, j: (i, 0)),
                  pl.BlockSpec((1, D), lambda i, j: (0, 0)),
                  pl.BlockSpec((None, D, tn), lambda i, j: (layer, 0, j)),
                  pl.BlockSpec((1, tn), crng(0, jq)),
                  pl.BlockSpec(head_sum.shape, lambda i, j: (0, 0)),
                  pl.BlockSpec((1, tn), crng(jh, jn))],
        out_specs=[blk(rng(0, jq)), blk(rng(jq, jv)), blk(rng(jv, jh)), blk(rng(jh, jn))],
        out_shape=[jax.ShapeDtypeStruct((T, wq), BF16), jax.ShapeDtypeStruct((T, wv), BF16),
                   jax.ShapeDtypeStruct((T, wh), F32), jax.ShapeDtypeStruct((T, wg), BF16)],
        scratch_shapes=[pltpu.VMEM((tm, D), BF16)],
        compiler_params=_params("parallel", "arbitrary"),
        name="proj_in",
    )(x, norm_g.reshape(1, D), w_all, qk_gain, head_sum, gate_b)


def _merge_out_kernel(a_ref, b_ref, wa_ref, wb_ref, g_ref, wo_ref, x_ref, o_ref):
    d = o_ref.shape[1]
    ya = _dot(a_ref[...], wa_ref[...].astype(BF16))
    yb = _dot(b_ref[...], wb_ref[...].astype(BF16))
    g = g_ref[...].astype(F32)
    merged = (g[:, :d] * ya + g[:, d:] * yb).astype(BF16)
    o_ref[...] = x_ref[...] + _dot(merged, wo_ref[...].astype(BF16))


def _merge_out(a_out, b_out, w_na, w_hy, gates, w_o, layer, x, tm=1024):
    T, wa = a_out.shape
    D = w_na.shape[2]
    full = lambda a: pl.BlockSpec((None,) + a.shape[1:], lambda i: (layer, 0, 0))
    return pl.pallas_call(
        _merge_out_kernel,
        grid=(T // tm,),
        in_specs=[pl.BlockSpec((tm, wa), lambda i: (i, 0)),
                  pl.BlockSpec((tm, b_out.shape[1]), lambda i: (i, 0)),
                  full(w_na), full(w_hy),
                  pl.BlockSpec((tm, 2 * D), lambda i: (i, 0)),
                  full(w_o),
                  pl.BlockSpec((tm, D), lambda i: (i, 0))],
        out_specs=pl.BlockSpec((tm, D), lambda i: (i, 0)),
        out_shape=jax.ShapeDtypeStruct((T, D), F32),
        compiler_params=_params("parallel"),
        name="merge_out",
    )(a_out, b_out, w_na, w_hy, gates, w_o, x)


def _na_kernel(rows, rb, q_ref, k_ref, v_ref, bias_ref, o_ref, s_ref, p_ref):
    blk = pl.program_id(2)
    lane = lax.broadcasted_iota(jnp.int32, (GRID_W, LANES), 1)
    nk = NA_KR * GRID_W
    starts = []
    for i in range(rb):
        r = blk * rb + i
        r0 = jnp.clip(r - NA_KR // 2, 0, rows - NA_KR)
        dr0 = r0 - r + (NA_KR - 1)
        start = pl.multiple_of(r0 * GRID_W, GRID_W)
        starts.append(start)
        qr = q_ref[i * GRID_W:(i + 1) * GRID_W, :]
        kw = k_ref[pl.ds(start, nk), :]
        for hh in range(2):
            in_head = (lane >= NA_HEAD_DIM) if hh else (lane < NA_HEAD_DIM)
            qm = jnp.where(in_head, qr, jnp.zeros_like(qr))
            s = lax.dot_general(qm, kw, (((1,), (1,)), ((), ())), preferred_element_type=F32)
            s_ref[(2 * i + hh) * GRID_W:(2 * i + hh + 1) * GRID_W, :] = s + bias_ref[hh, dr0]
    s = s_ref[...]
    p = jnp.exp(s - jnp.max(s, axis=-1, keepdims=True))
    p_ref[...] = (p * (1.0 / jnp.sum(p, axis=-1, keepdims=True))).astype(BF16)
    for i in range(rb):
        vw = v_ref[pl.ds(starts[i], nk), :]
        o0 = _dot(p_ref[(2 * i) * GRID_W:(2 * i + 1) * GRID_W, :], vw)
        o1 = _dot(p_ref[(2 * i + 1) * GRID_W:(2 * i + 2) * GRID_W, :], vw)
        o_ref[i * GRID_W:(i + 1) * GRID_W, :] = jnp.where(lane < NA_HEAD_DIM, o0, o1).astype(o_ref.dtype)


def _na_bias_table(rpb):
    c = np.arange(GRID_W)
    c0 = np.clip(c - NA_KW // 2, 0, GRID_W - NA_KW)
    col_in = (c[None, :] >= c0[:, None]) & (c[None, :] < c0[:, None] + NA_KW)
    dc_idx = np.clip(c[None, :] - c[:, None] + (NA_KW - 1), 0, 2 * NA_KW - 2)
    onehot = jnp.asarray((dc_idx[:, :, None] == np.arange(2 * NA_KW - 1)).astype(np.float32))
    rows = jnp.stack([rpb.astype(F32)[:, d:d + NA_KR] for d in range(NA_KR)], axis=1)
    b = jnp.einsum("hdjc,qkc->hdqjk", rows, onehot, precision=lax.Precision.HIGHEST)
    b = jnp.where(jnp.asarray(col_in)[None, None, :, None, :], b, NEG)
    return b.reshape(NA_HEADS, NA_KR, GRID_W, NA_KR * GRID_W)


def _na(qk, v, bias, B, S, rb=16):
    rows = S // GRID_W
    assert rows >= NA_KR and rows % rb == 0
    nblk = rows // rb
    tq = rb * GRID_W
    kofs = NA_WIDTH // LANES
    return pl.pallas_call(
        functools.partial(_na_kernel, rows, rb),
        grid=(B, NA_HEADS // 2, nblk),
        in_specs=[pl.BlockSpec((tq, LANES), lambda b, p, r: (b * nblk + r, p)),
                  pl.BlockSpec((S, LANES), lambda b, p, r: (b, kofs + p)),
                  pl.BlockSpec((S, LANES), lambda b, p, r: (b, p)),
                  pl.BlockSpec((2, NA_KR, GRID_W, NA_KR * GRID_W), lambda b, p, r: (p, 0, 0, 0))],
        out_specs=pl.BlockSpec((tq, LANES), lambda b, p, r: (b * nblk + r, p)),
        out_shape=jax.ShapeDtypeStruct((B * S, NA_WIDTH), BF16),
        scratch_shapes=[pltpu.VMEM((2 * tq, NA_KR * GRID_W), F32), pltpu.VMEM((2 * tq, NA_KR * GRID_W), BF16)],
        compiler_params=_params("parallel", "parallel", "parallel"),
        name="na_attention",
    )(qk, qk, v, bias)


def _hy_prep_kernel(S, hy_ref, prev_ref, next_ref, w_ref, b_ref, z_ref, x0_ref):
    tm, C = hy_ref.shape
    tok0 = pl.program_id(0) * tm
    h = hy_ref[...]
    row = lax.broadcasted_iota(jnp.int32, (tm, C), 0)
    prev_row = jnp.where(tok0 % S == 0, 0.0, prev_ref[7:8, :])
    next_row = jnp.where((tok0 + tm) % S == 0, 0.0, next_ref[0:1, :])
    up = jnp.where(row == 0, prev_row, pltpu.roll(h, 1, axis=0))
    dn = jnp.where(row == tm - 1, next_row, pltpu.roll(h, tm - 1, axis=0))
    u = up * w_ref[0:1, :] + h * w_ref[1:2, :] + dn * w_ref[2:3, :] + b_ref[...]
    c = C // 3
    x0_ref[...] = u[:, :c]
    z_ref[...] = u[:, 2 * c:] * u[:, c:2 * c]


def _hy_prep(hy, conv_w, conv_b, S, tm=1024):
    T, C = hy.shape
    nb = tm // 8
    last = T // 8 - 1
    return pl.pallas_call(
        functools.partial(_hy_prep_kernel, S),
        grid=(T // tm,),
        in_specs=[pl.BlockSpec((tm, C), lambda i: (i, 0)),
                  pl.BlockSpec((8, C), lambda i: (jnp.maximum(i * nb - 1, 0), 0)),
                  pl.BlockSpec((8, C), lambda i: (jnp.minimum((i + 1) * nb, last), 0)),
                  pl.BlockSpec((3, C), lambda i: (0, 0)),
                  pl.BlockSpec((1, C), lambda i: (0, 0))],
        out_specs=[pl.BlockSpec((tm, C // 3), lambda i: (i, 0)),
                   pl.BlockSpec((tm, C // 3), lambda i: (i, 0))],
        out_shape=[jax.ShapeDtypeStruct((T, C // 3), F32), jax.ShapeDtypeStruct((T, C // 3), F32)],
        compiler_params=_params("parallel"),
        name="hyena_prep",
    )(hy, hy, hy, conv_w, conv_b.reshape(1, C))


def _hy_filter_kernel(L, pos_ref, dec_ref, w1_ref, b1_ref, fr_ref, w2_ref, b2_ref, w3_ref, g_ref):
    tr, C = g_ref.shape
    hr = tr // 2
    fr = fr_ref[...]
    w1h, w1l = _split(w1_ref[...])
    w2h, w2l = _split(w2_ref[...])
    w3h, w3l = _split(w3_ref[...])
    ph, plo = _split(jnp.concatenate([pos_ref[0:hr, :], pos_ref[hr:tr, :]], axis=1))
    h = jnp.sin(fr * (_dot3(ph, plo, w1h, w1l) + _dot(plo, w1l) + b1_ref[...]))
    hh, hl = _split(h)
    h = jnp.sin(fr * (_dot3(hh, hl, w2h, w2l) + _dot(hl, w2l) + b2_ref[...]))
    hh, hl = _split(h)
    h = _dot3(hh, hl, w3h, w3l) + _dot(hl, w3l)
    for half in range(2):
        rows = slice(half * hr, (half + 1) * hr)
        dec = jnp.exp(-(pos_ref[rows, 0:1] * dec_ref[...]))
        hf = h[:, (2 * half) * C:(2 * half + 1) * C] * dec
        hb = h[:, (2 * half + 1) * C:(2 * half + 2) * C] * dec
        n = pl.program_id(0) * tr + half * hr + lax.broadcasted_iota(jnp.int32, (hr, C), 0)
        g = jnp.where(n < L, hf, hb)
        g = jnp.where(n == L, 0.0, g)
        g_ref[rows, :] = jnp.where(n == 0, hf + hb, g)


def _hy_filter(L, pos_ext, dec_ext, w1p, b1, freq, w2, b2, w3, tr=2048):
    N = 2 * L
    tr = min(tr, N)
    full = lambda a: pl.BlockSpec(a.shape, lambda i: (0, 0))
    twice = lambda w: jnp.kron(jnp.eye(2, dtype=w.dtype), w)
    row2 = lambda b: jnp.tile(b, 2).reshape(1, -1)
    ws = (twice(w1p), row2(b1), row2(freq), twice(w2), row2(b2), twice(w3))
    return pl.pallas_call(
        functools.partial(_hy_filter_kernel, L),
        grid=(N // tr,),
        in_specs=[pl.BlockSpec((tr, pos_ext.shape[1]), lambda i: (i, 0)), full(dec_ext)]
                 + [full(w) for w in ws],
        out_specs=pl.BlockSpec((tr, HY_WIDTH), lambda i: (i, 0)),
        out_shape=jax.ShapeDtypeStruct((N, HY_WIDTH), F32),
        compiler_params=_params("parallel"),
        name="hyena_filter",
    )(pos_ext, dec_ext, *ws)


def _dft_pair_kernel(ma_ref, mb_ref, x_ref, or_ref, oi_ref):
    xh, xl = _split(x_ref[...])
    or_ref[...] = _dot3(ma_ref[0], ma_ref[1], xh, xl)
    oi_ref[...] = _dot3(mb_ref[0], mb_ref[1], xh, xl)


def _dft_pair(ma, mb, x, tc=2048):
    R = ma.shape[1]
    K, C = x.shape
    return pl.pallas_call(
        _dft_pair_kernel,
        grid=(C // tc,),
        in_specs=[pl.BlockSpec(ma.shape, lambda j: (0, 0, 0)),
                  pl.BlockSpec(mb.shape, lambda j: (0, 0, 0)),
                  pl.BlockSpec((K, tc), lambda j: (0, j))],
        out_specs=[pl.BlockSpec((R, tc), lambda j: (0, j)), pl.BlockSpec((R, tc), lambda j: (0, j))],
        out_shape=[jax.ShapeDtypeStruct((R, C), F32), jax.ShapeDtypeStruct((R, C), F32)],
        compiler_params=_params("parallel"),
        name="dft_outer",
    )(ma, mb, x)


def _spec_conv_kernel(ar_ref, ai_ref, fr_ref, fi_ref, twc_ref, tws_ref, wf_ref, wi_ref, br_ref, bi_ref):
    kb, n2, C = ar_ref.shape
    wfh, wfl, wih, wil = wf_ref[0], wf_ref[1], wi_ref[0], wi_ref[1]
    for k in range(kb):
        c = jnp.concatenate([twc_ref[k]] * (C // LANES), axis=1)
        s = jnp.concatenate([tws_ref[k]] * (C // LANES), axis=1)

        def inner_dft(re, im):
            h, l = _split(jnp.concatenate([re * c + im * s, im * c - re * s], axis=0))
            return _dot3(wfh, wfl, h, l)

        x = inner_dft(ar_ref[k], ai_ref[k])
        g = inner_dft(fr_ref[k], fi_ref[k])
        xr, xi, gr, gi = x[:n2], x[n2:], g[:n2], g[n2:]
        yh, yl = _split(jnp.concatenate([xr * gr - xi * gi, xr * gi + xi * gr], axis=0))
        b = _dot3(wih, wil, yh, yl)
        br, bi = b[:n2], b[n2:]
        br_ref[k] = br * c - bi * s
        bi_ref[k] = bi * c + br * s


def _spec_conv(ar, ai, fr, fi, fc, kb=8):
    n1, n2, C = ar.shape
    kb = min(kb, n1)
    blk = pl.BlockSpec((kb, n2, C), lambda k: (k, 0, 0))
    tblk = pl.BlockSpec((kb, n2, LANES), lambda k: (k, 0, 0))
    wblk = pl.BlockSpec((2, 2 * n2, 2 * n2), lambda k: (0, 0, 0))
    return pl.pallas_call(
        _spec_conv_kernel,
        grid=(n1 // kb,),
        in_specs=[blk, blk, blk, blk, tblk, tblk, wblk, wblk],
        out_specs=[blk, blk],
        out_shape=[jax.ShapeDtypeStruct((n1, n2, C), F32), jax.ShapeDtypeStruct((n1, n2, C), F32)],
        compiler_params=_params("parallel"),
        name="spectrum_conv",
    )(ar, ai, fr, fi, fc["twc"], fc["tws"], fc["wf"], fc["wi"])


def _idft_out_kernel(ma_ref, mb_ref, br_ref, bi_ref, z_ref, x0_ref, bias_ref, o_ref):
    brh, brl = _split(br_ref[...])
    bih, bil = _split(bi_ref[...])
    y = _dot3(ma_ref[0], ma_ref[1], brh, brl) + _dot3(mb_ref[0], mb_ref[1], bih, bil)
    o_ref[...] = ((y + z_ref[...] * bias_ref[...]) * x0_ref[...]).astype(o_ref.dtype)


def _idft_out(ma, mb, br, bi, z2, x02, bias_t, tc=2048):
    R = ma.shape[1]
    K, C = br.shape
    col = lambda rws: pl.BlockSpec((rws, tc), lambda j: (0, j))
    return pl.pallas_call(
        _idft_out_kernel,
        grid=(C // tc,),
        in_specs=[pl.BlockSpec(ma.shape, lambda j: (0, 0, 0)), pl.BlockSpec(mb.shape, lambda j: (0, 0, 0)),
                  col(K), col(K), col(R), col(R), pl.BlockSpec((1, tc), lambda j: (0, 0))],
        out_specs=col(R),
        out_shape=jax.ShapeDtypeStruct((R, C), BF16),
        compiler_params=_params("parallel"),
        name="idft_outer",
    )(ma, mb, br, bi, z2, x02, bias_t)


def _split_const(m):
    m32 = jnp.asarray(np.asarray(m, np.float32))
    hi, lo = _split(m32)
    return jnp.stack([hi, lo])


def _fft_constants(L):
    N = 2 * L
    n2 = FFT_N2
    n1 = N // n2
    h = n1 // 2
    k1 = np.arange(n1)[:, None].astype(np.float64)
    ang = 2.0 * np.pi * k1 * np.arange(n1)[None, :] / n1
    c, s = np.cos(ang), np.sin(ang)
    za = np.concatenate([c[:, :h], s[:, :h]], axis=1)
    zb = np.concatenate([-s[:, :h], c[:, :h]], axis=1)
    ga, gb = c, -s
    ya = np.concatenate([c[:h], s[:h]], axis=0)
    yb = np.concatenate([-s[:h], c[:h]], axis=0)
    kk = np.arange(n2)[:, None].astype(np.float64)
    nn = np.arange(n2)[None, :].astype(np.float64)
    base = 2.0 * np.pi * kk * nn / n2
    cb, sb = np.cos(base), np.sin(base)
    wf = np.block([[cb, sb], [-sb, cb]])
    wi = np.block([[cb.T, -sb.T], [sb.T, cb.T]]) / N
    tw = 2.0 * np.pi * np.arange(n1)[:, None] * nn / N
    rep = lambda m: jnp.broadcast_to(jnp.asarray(np.asarray(m, np.float32))[:, :, None], (n1, n2, LANES))
    sp = lambda m: _split_const(m)
    return dict(za=sp(za), zb=sp(zb), ga=sp(ga), gb=sp(gb), ya=sp(ya), yb=sp(yb), wf=sp(wf), wi=sp(wi),
                twc=rep(np.cos(tw)), tws=rep(np.sin(tw)), n1=n1)


def _hy_pos_tables(L):
    ext = lambda a: jnp.concatenate([a, a[0:1], jnp.flip(a[1:], axis=0)], axis=0)
    t = ext(jnp.linspace(0.0, 1.0, L, dtype=F32)[:, None])
    w = ext(2.0 * math.pi * jnp.arange(L, dtype=F32)[:, None] / L)
    f = jnp.linspace(1e-4, HY_POS_BANDS - 1, HY_POS_BANDS, dtype=F32)[None, :]
    z = jnp.concatenate([t, jnp.cos(f * w), -jnp.sin(f * w)], axis=-1)
    max_decay = math.log(HY_DECAY_TARGET) / HY_FAST_PCT
    min_decay = math.log(HY_DECAY_TARGET) / HY_SLOW_PCT
    deltas = jnp.linspace(min_decay, max_decay, HY_WIDTH, dtype=F32)
    zp = jnp.pad(z, ((0, 0), (0, HY_FFN_HIDDEN - HY_POS_DIM)))
    return zp, jnp.abs(deltas)[None, :]


_TRI_ROWS = 16 + 8 * 7 + 8


def _tri_tables(tm):
    flat = [b for b in range(16)]
    for a in range(1, 8):
        flat += [16 * a + b for b in range(8)]
    flat += [16 * a for a in range(8, 16)]
    return jnp.asarray(np.tile(np.asarray(flat, np.float32)[:, None], (1, tm)))


def _sort16_pairs():
    def merge(lo, hi, r):
        step = r * 2
        if step < hi - lo:
            yield from merge(lo, hi, step)
            yield from merge(lo + r, hi, step)
            yield from [(i, i + r) for i in range(lo + r, hi - r, step)]
        else:
            yield (lo, lo + r)

    def sort(lo, hi):
        if hi - lo >= 1:
            mid = lo + (hi - lo) // 2
            yield from sort(lo, mid)
            yield from sort(mid + 1, hi)
            yield from merge(lo, hi, 1)
    return list(sort(0, 15))


_SORT16 = _sort16_pairs()
_BITONIC16 = [(i, i + d) for d in (8, 4, 2, 1) for i in range(16) if not i & d]


def _cmpx(x, pairs):
    x = list(x)
    for i, j in pairs:
        x[i], x[j] = jnp.maximum(x[i], x[j]), jnp.minimum(x[i], x[j])
    return x


def _top16_sorted(s):
    n, tm = s.shape
    sub = n // PEER_TOPK
    x = _cmpx([s[r * sub:(r + 1) * sub] for r in range(PEER_TOPK)], _SORT16)
    shift = sub // 2
    while shift:
        y = [pltpu.roll(v, shift, axis=0) for v in x]
        x = _cmpx([jnp.maximum(x[i], y[PEER_TOPK - 1 - i]) for i in range(PEER_TOPK)], _BITONIC16)
        shift //= 2
    rowk = lax.broadcasted_iota(jnp.int32, (PEER_TOPK, tm), 0)
    vals = jnp.zeros((PEER_TOPK, tm), F32)
    for r in range(PEER_TOPK):
        vals = jnp.where(rowk == r, jnp.concatenate([x[r], x[r]], axis=0), vals)
    return vals


def _topk_rows(scores, k, exact):
    n, tm = scores[0].shape
    if not exact:
        return [(_top16_sorted(s), None) for s in scores]
    rowk = lax.broadcasted_iota(jnp.int32, (k, tm), 0)
    rowf = lax.broadcasted_iota(jnp.int32, (n, tm), 0).astype(F32)
    state = [(s, jnp.zeros((k, tm), F32), jnp.full((n, tm), float(k), F32)) for s in scores]
    for r in range(k):
        nxt = []
        for s, vals, rank in state:
            m = jnp.max(s, axis=0, keepdims=True)
            sel = rowf == jnp.min(jnp.where(s == m, rowf, float(n)), axis=0, keepdims=True)
            nxt.append((jnp.where(sel, -jnp.inf, s), jnp.where(rowk == r, m, vals),
                        jnp.where(sel, float(r), rank)))
        state = nxt
    return [(vals, rank) for _, vals, rank in state]


def _stair_cells(t1, t2, op):
    pieces = [op(t1[0:1], t2)]
    pieces += [op(t1[a:a + 1], t2[0:8]) for a in range(1, 8)]
    pieces += [op(t1[8:16], t2[0:1])]
    return jnp.concatenate(pieces, axis=0)


def _peer_tables(scores, flat, exact):
    K = PEER_TOPK
    tm = scores[0].shape[1]
    (sv1, rank1), (sv2, rank2) = _topk_rows(scores, K, exact)
    cand = _stair_cells(sv1, sv2, lambda x, y: x + y)
    if exact:
        row16 = lax.broadcasted_iota(jnp.int32, (K, tm), 0).astype(F32)
        length = jnp.zeros((K, tm), F32)
        for r in range(K):
            m = jnp.max(cand, axis=0, keepdims=True)
            f = jnp.min(jnp.where(cand == m, flat, 1e9), axis=0, keepdims=True)
            cand = jnp.where(flat == f, -jnp.inf, cand)
            length = jnp.where(row16 == jnp.floor(f * (1.0 / K)), length + 1.0, length)
        picked = cand == -jnp.inf
    else:
        for r in range(K):
            cand = jnp.where(cand == jnp.max(cand, axis=0, keepdims=True), -jnp.inf, cand)
        picked = cand == -jnp.inf
        cnt = jnp.where(picked, 1.0, 0.0)
        rows = [jnp.sum(cnt[0:16], axis=0, keepdims=True)]
        rows += [jnp.sum(cnt[8 + 8 * a:16 + 8 * a], axis=0, keepdims=True) for a in range(1, 8)]
        length = jnp.concatenate(rows + [cnt[72:80]], axis=0)
    s1, s2 = scores
    lfull = jnp.zeros_like(s1)
    if exact:
        ok = jnp.ones((1, tm), jnp.bool_)
        for r in range(K):
            lfull = jnp.where(rank1 == float(r), length[r:r + 1], lfull)
    else:
        rank2 = jnp.zeros_like(s2)
        for r in range(K):
            lfull = jnp.where(s1 == sv1[r:r + 1], length[r:r + 1], lfull)
            rank2 = jnp.where(sv2[r:r + 1] > s2, float(r + 1), rank2)
        reach = lambda s, sv: jnp.sum(jnp.where(s >= sv[K - 1:K], 1.0, 0.0), axis=0, keepdims=True)
        ok = (reach(s1, sv1) == float(K)) & (reach(s2, sv2) == float(K)) & \
             (jnp.sum(length, axis=0, keepdims=True) == float(K))
    es1 = jnp.exp(sv1 - sv1[0:1])
    es2 = jnp.exp(sv2 - sv2[0:1])
    ecand = _stair_cells(es1, es2, lambda x, y: x * y)
    z = jnp.sum(jnp.where(picked, ecand, 0.0), axis=0, keepdims=True)
    e1 = jnp.exp(scores[0] - sv1[0:1]) / z
    e2 = jnp.exp(scores[1] - sv2[0:1])
    return (lfull, e1, rank2, e2), jnp.where(ok, 1.0, 0.0)


def _peer_topk_kernel(hnT_ref, wq_ref, sk_ref, flat_ref, l_ref, e1_ref, rb_ref, e2_ref):
    flat = flat_ref[...]

    def head(h, carry):
        qT = _dot(wq_ref[h], hnT_ref[...])
        scores = []
        for p in range(2):
            qh, ql = _split(qT[p * PEER_HALF:(p + 1) * PEER_HALF])
            kh, kl = _split(sk_ref[h, p])
            scores.append(_dot3(kh, kl, qh, ql))

        def write(tabs, cs):
            l_ref[h, :, cs], e1_ref[h, :, cs] = tabs[0], tabs[1]
            rb_ref[h, :, cs], e2_ref[h, :, cs] = tabs[2].astype(rb_ref.dtype), tabs[3].astype(e2_ref.dtype)

        wd = flat.shape[1]
        for c in range(qT.shape[1] // wd):
            cs = slice(c * wd, (c + 1) * wd)
            sc = [s[:, cs] for s in scores]
            tabs, ok = _peer_tables(sc, flat, exact=False)
            write(tabs, cs)

            @pl.when(jnp.min(ok) < 0.5)
            def _():
                write(_peer_tables(sc, flat, exact=True)[0], cs)
        return carry

    lax.fori_loop(0, PEER_HEADS, head, 0)


def _peer_topk(hnT, wq_heads, subkeys, tm=512, wd=256):
    D, T = hnT.shape
    H = wq_heads.shape[0]
    tm = min(tm, T)
    wd = min(wd, tm)
    out = jax.ShapeDtypeStruct((H, PEER_NKEYS, T), F32)
    outb = jax.ShapeDtypeStruct((H, PEER_NKEYS, T), BF16)
    oblk = pl.BlockSpec((H, PEER_NKEYS, tm), lambda i: (0, 0, i))
    return pl.pallas_call(
        _peer_topk_kernel,
        grid=(T // tm,),
        in_specs=[pl.BlockSpec((D, tm), lambda i: (0, i)),
                  pl.BlockSpec(wq_heads.shape, lambda i: (0, 0, 0)),
                  pl.BlockSpec((H, 2, PEER_NKEYS, PEER_HALF), lambda i: (0, 0, 0, 0)),
                  pl.BlockSpec((_TRI_ROWS, wd), lambda i: (0, 0))],
        out_specs=[oblk, oblk, oblk, oblk],
        out_shape=[out, out, outb, outb],
        compiler_params=_params("parallel"),
        name="peer_topk",
    )(hnT, wq_heads, subkeys, _tri_tables(wd))


def _gelu_tanh(x):
    c = math.sqrt(2.0 / math.pi)
    h = 0.5 * x
    return h + h * jnp.tanh(x * (c + (c * 0.044715) * (x * x)))


def _peer_dense_kernel(ni, hnT_ref, u_ref, vT_ref, l_ref, e1_ref, rb_ref, e2_ref, x_ref, o_ref,
                       acc_ref, act_ref, a_ref):
    j = pl.program_id(1)
    nk = PEER_NKEYS

    @pl.when(j == 0)
    def _():
        acc_ref[...] = jnp.zeros_like(acc_ref)

    tm = hnT_ref.shape[1]
    i0 = pl.multiple_of(j * ni, ni)
    zero = jnp.zeros((), BF16)
    a_ref[...] = _dot(u_ref[...], hnT_ref[...])
    for ii in range(ni):
        w = None
        for h in range(PEER_HEADS):
            lrow = jnp.broadcast_to(l_ref[h, pl.ds(i0 + ii, 1), :], (BF16_ROWS, tm)).astype(BF16)
            erow = jnp.broadcast_to(e1_ref[h, pl.ds(i0 + ii, 1), :], (BF16_ROWS, tm)).astype(BF16)
            wh = jnp.where(rb_ref[h] < lrow[None], e2_ref[h], zero) * erow[None]
            w = wh if w is None else w + wh
        a = a_ref[ii * nk:(ii + 1) * nk, :].astype(BF16)
        act_ref[ii * nk:(ii + 1) * nk, :] = _gelu_tanh(a) * w.reshape(nk, tm)
    acc_ref[...] += _dot(vT_ref[...], act_ref[...])

    @pl.when(j == pl.num_programs(1) - 1)
    def _():
        o_ref[...] = x_ref[...] + acc_ref[...].T


def _peer_dense(hnT, u_bf, vT_bf, tabs, x, tm=512, ni=16):
    D, T = hnT.shape
    NE = u_bf.shape[0]
    te = ni * PEER_NKEYS
    tm = min(tm, T)
    tab = pl.BlockSpec((PEER_HEADS, PEER_NKEYS, tm), lambda i, j: (0, 0, i))
    grp = PEER_NKEYS // BF16_ROWS
    tabb = pl.BlockSpec((PEER_HEADS, grp, BF16_ROWS, tm), lambda i, j: (0, 0, 0, i))
    lt, e1, rb, e2 = tabs
    rb, e2 = [t.reshape(PEER_HEADS, grp, BF16_ROWS, T) for t in (rb, e2)]
    return pl.pallas_call(
        functools.partial(_peer_dense_kernel, ni),
        grid=(T // tm, NE // te),
        in_specs=[pl.BlockSpec((D, tm), lambda i, j: (0, i)),
                  pl.BlockSpec((te, D), lambda i, j: (j, 0)),
                  pl.BlockSpec((D, te), lambda i, j: (0, j)),
                  tab, tab, tabb, tabb,
                  pl.BlockSpec((tm, D), lambda i, j: (i, 0))],
        out_specs=pl.BlockSpec((tm, D), lambda i, j: (i, 0)),
        out_shape=jax.ShapeDtypeStruct((T, D), F32),
        scratch_shapes=[pltpu.VMEM((D, tm), F32), pltpu.VMEM((te, tm), BF16), pltpu.VMEM((te, tm), F32)],
        compiler_params=_params("parallel", "arbitrary"),
        name="peer_experts",
    )(hnT, u_bf, vT_bf, lt, e1, rb, e2, x)


def kernel(x, norm1_g, w_in, gate_b, na_q_g, na_k_g, na_rpb, hy_conv_w, hy_conv_b, hy_w1, hy_b1, hy_freq,
           hy_w2, hy_b2, hy_w3, hy_bias, w_up_na, w_up_hy, w_out, norm2_g, peer_wq, peer_subkeys, peer_u,
           peer_v):
    B, S, D = x.shape
    assert B == 2, "the long convolution packs the two batches as one complex sequence"
    depth = w_in.shape[0]
    T = B * S
    xt = x.reshape(T, D)

    fc = _fft_constants(S)
    n1 = fc["n1"]
    pos_ext, dec_ext = _hy_pos_tables(S)
    head_sum = jnp.asarray(np.kron(np.eye(NA_HEADS), np.ones((NA_HEAD_DIM, NA_HEAD_DIM))), BF16)
    o_qk, o_v, o_hy = 2 * NA_WIDTH, 3 * NA_WIDTH, 3 * NA_WIDTH + 3 * HY_WIDTH
    tc = 4096
    ncol = FFT_N2 * HY_WIDTH

    for l in range(depth):
        qk_gain = jnp.concatenate([jnp.tile(na_q_g[l], NA_HEADS) * (NA_HEAD_DIM ** -0.5),
                                   jnp.tile(na_k_g[l], NA_HEADS)]).reshape(1, o_qk).astype(F32)
        qk, v, hy, gates = _proj_in(xt, norm1_g[l], w_in, l, qk_gain, head_sum, gate_b[l].reshape(1, 2 * D),
                                    (o_qk, o_v - o_qk, o_hy - o_v, w_in.shape[2] - o_hy))

        a_out = _na(qk, v, _na_bias_table(na_rpb[l]), B, S)

        z, x0 = _hy_prep(hy, hy_conv_w[l], hy_conv_b[l], S)
        w1p = jnp.pad(hy_w1[l], ((0, HY_FFN_HIDDEN - HY_POS_DIM), (0, 0)))
        g = _hy_filter(S, pos_ext, dec_ext, w1p, hy_b1[l], hy_freq[l], hy_w2[l], hy_b2[l], hy_w3[l])
        gr, gi = _dft_pair(fc["ga"], fc["gb"], g.reshape(n1, ncol), tc)
        z2 = z.reshape(n1, ncol)
        ar, ai = _dft_pair(fc["za"], fc["zb"], z2, tc)
        cube = lambda t: t.reshape(n1, FFT_N2, HY_WIDTH)
        br, bi = _spec_conv(cube(ar), cube(ai), cube(gr), cube(gi), fc)
        bias_t = jnp.tile(hy_bias[l], tc // HY_WIDTH).reshape(1, tc)
        b_out = _idft_out(fc["ya"], fc["yb"], br.reshape(n1, ncol), bi.reshape(n1, ncol),
                          z2, x0.reshape(n1, ncol), bias_t, tc).reshape(T, HY_WIDTH)

        xt = _merge_out(a_out, b_out, w_up_na, w_up_hy, gates, w_out, l, xt)

        hn2T = _rmsnorm_t(xt, norm2_g[l])
        wq_heads = peer_wq[l].reshape(D, PEER_HEADS, 2 * PEER_HALF).transpose(1, 2, 0).astype(BF16)
        tabs = _peer_topk(hn2T, wq_heads, peer_subkeys[l])
        xt = _peer_dense(hn2T, peer_u[l].astype(BF16), peer_v[l].T.astype(BF16), tabs, xt)
    return xt.reshape(B, S, D)
```

```python
import functools
import math

import numpy as np
import jax
import jax.numpy as jnp
from jax import lax
from jax.experimental import pallas as pl
from jax.experimental.pallas import tpu as pltpu

F32 = jnp.float32
BF16 = jnp.bfloat16

GRID_W = 64
NA_HEADS = 8
NA_HEAD_DIM = 64
NA_WIDTH = NA_HEADS * NA_HEAD_DIM
NA_KR = 8
NA_KW = 16
HY_WIDTH = 512
HY_POS_BANDS = 16
HY_POS_DIM = 1 + 2 * HY_POS_BANDS
HY_FFN_HIDDEN = 64
HY_DECAY_TARGET = 1e-2
HY_FAST_PCT = 0.3
HY_SLOW_PCT = 1.5
PEER_HEADS = 8
PEER_NKEYS = 128
PEER_TOPK = 16
PEER_HALF = 128
EPS = 1e-6
NEG = -1e30

LANES = 128
VMEM_LIMIT_BYTES = 56 * 1024 * 1024
FFT_N2 = 128
BF16_ROWS = 16


def _params(*sem):
    return pltpu.CompilerParams(dimension_semantics=sem, vmem_limit_bytes=VMEM_LIMIT_BYTES)


def _split(x):
    hi = x.astype(BF16)
    lo = (x - hi.astype(F32)).astype(BF16)
    return hi, lo


def _dot(a, b):
    return jnp.dot(a, b, preferred_element_type=F32)


def _dot3(ah, al, bh, bl):
    return _dot(ah, bh) + _dot(ah, bl) + _dot(al, bh)


def _rmsnorm_t_kernel(x_ref, g_ref, ot_ref):
    x = x_ref[...]
    y = x * lax.rsqrt(jnp.mean(x * x, axis=-1, keepdims=True) + EPS)
    ot_ref[...] = (y * g_ref[...]).T.astype(BF16)


def _rmsnorm_t(x, g, tm=1024):
    T, D = x.shape
    return pl.pallas_call(
        _rmsnorm_t_kernel,
        grid=(T // tm,),
        in_specs=[pl.BlockSpec((tm, D), lambda i: (i, 0)),
                  pl.BlockSpec((1, D), lambda i: (0, 0))],
        out_specs=pl.BlockSpec((D, tm), lambda i: (0, i)),
        out_shape=jax.ShapeDtypeStruct((D, T), BF16),
        compiler_params=_params("parallel"),
        name="rmsnorm_t",
    )(x, g.reshape(1, D))


def _epi_head_rmsnorm(acc, gain, head_sum):
    hi, lo = _split(acc * acc)
    ms = (_dot(hi, head_sum) + _dot(lo, head_sum)) * (1.0 / NA_HEAD_DIM)
    return acc * lax.rsqrt(ms + EPS) * gain


def _epi_gate(acc, bias):
    return jax.nn.sigmoid(acc + bias)


def _proj_in_kernel(jq, jv, jh, x_ref, n1g_ref, w_ref, gain_ref, hs_ref, gb_ref,
                    qk_ref, v_ref, hy_ref, g_ref, hn_ref):
    j = pl.program_id(1)

    @pl.when(j == 0)
    def _():
        x = x_ref[...]
        y = x * lax.rsqrt(jnp.mean(x * x, axis=-1, keepdims=True) + EPS)
        hn_ref[...] = (y * n1g_ref[...]).astype(BF16)

    acc = _dot(hn_ref[...], w_ref[...].astype(BF16))

    @pl.when(j < jq)
    def _():
        qk_ref[...] = _epi_head_rmsnorm(acc, gain_ref[...], hs_ref[...]).astype(qk_ref.dtype)

    @pl.when((j >= jq) & (j < jv))
    def _():
        v_ref[...] = acc.astype(v_ref.dtype)

    @pl.when((j >= jv) & (j < jh))
    def _():
        hy_ref[...] = acc

    @pl.when(j >= jh)
    def _():
        g_ref[...] = _epi_gate(acc, gb_ref[...]).astype(g_ref.dtype)


def _proj_in(x, norm_g, w_all, layer, qk_gain, head_sum, gate_b, widths, tm=2048, tn=512):
    T, D = x.shape
    tm = min(tm, T)
    wq, wv, wh, wg = widths
    assert all(x % tn == 0 for x in widths)
    jq, jv, jh, jn = wq // tn, (wq + wv) // tn, (wq + wv + wh) // tn, sum(widths) // tn
    rng = lambda lo, hi: (lambda i, j: (i, jnp.clip(j - lo, 0, hi - lo - 1)))
    crng = lambda lo, hi: (lambda i, j: (0, jnp.clip(j - lo, 0, hi - lo - 1)))
    blk = lambda f: pl.BlockSpec((tm, tn), f)
    return pl.pallas_call(
        functools.partial(_proj_in_kernel, jq, jv, jh),
        grid=(T // tm, jn),
        in_specs=[pl.BlockSpec((tm, D), lambda i, j: (i, 0)),
                  pl.BlockSpec((1, D), lambda i, j: (0, 0)),
                  pl.BlockSpec((None, D, tn), lambda i, j: (layer, 0, j)),
                  pl.BlockSpec((1, tn), crng(0, jq)),
                  pl.BlockSpec(head_sum.shape, lambda i, j: (0, 0)),
                  pl.BlockSpec((1, tn), crng(jh, jn))],
        out_specs=[blk(rng(0, jq)), blk(rng(jq, jv)), blk(rng(jv, jh)), blk(rng(jh, jn))],
        out_shape=[jax.ShapeDtypeStruct((T, wq), BF16), jax.ShapeDtypeStruct((T, wv), BF16),
                   jax.ShapeDtypeStruct((T, wh), F32), jax.ShapeDtypeStruct((T, wg), BF16)],
        scratch_shapes=[pltpu.VMEM((tm, D), BF16)],
        compiler_params=_params("parallel", "arbitrary"),
        name="proj_in",
    )(x, norm_g.reshape(1, D), w_all, qk_gain, head_sum, gate_b)


def _merge_out_kernel(a_ref, b_ref, wa_ref, wb_ref, g_ref, wo_ref, x_ref, o_ref):
    d = o_ref.shape[1]
    ya = _dot(a_ref[...], wa_ref[...].astype(BF16))
    yb = _dot(b_ref[...], wb_ref[...].astype(BF16))
    g = g_ref[...].astype(F32)
    merged = (g[:, :d] * ya + g[:, d:] * yb).astype(BF16)
    o_ref[...] = x_ref[...] + _dot(merged, wo_ref[...].astype(BF16))


def _merge_out(a_out, b_out, w_na, w_hy, gates, w_o, layer, x, tm=512):
    T, wa = a_out.shape
    D = w_na.shape[2]
    full = lambda a: pl.BlockSpec((None,) + a.shape[1:], lambda i: (layer, 0, 0))
    return pl.pallas_call(
        _merge_out_kernel,
        grid=(T // tm,),
        in_specs=[pl.BlockSpec((tm, wa), lambda i: (i, 0)),
                  pl.BlockSpec((tm, b_out.shape[1]), lambda i: (i, 0)),
                  full(w_na), full(w_hy),
                  pl.BlockSpec((tm, 2 * D), lambda i: (i, 0)),
                  full(w_o),
                  pl.BlockSpec((tm, D), lambda i: (i, 0))],
        out_specs=pl.BlockSpec((tm, D), lambda i: (i, 0)),
        out_shape=jax.ShapeDtypeStruct((T, D), F32),
        compiler_params=_params("parallel"),
        name="merge_out",
    )(a_out, b_out, w_na, w_hy, gates, w_o, x)


def _na_kernel(rows, rb, q_ref, k_ref, v_ref, bias_ref, o_ref, s_ref, p_ref):
    blk = pl.program_id(2)
    lane = lax.broadcasted_iota(jnp.int32, (GRID_W, LANES), 1)
    nk = NA_KR * GRID_W
    starts = []
    for i in range(rb):
        r = blk * rb + i
        r0 = jnp.clip(r - NA_KR // 2, 0, rows - NA_KR)
        dr0 = r0 - r + (NA_KR - 1)
        start = pl.multiple_of(r0 * GRID_W, GRID_W)
        starts.append(start)
        qr = q_ref[i * GRID_W:(i + 1) * GRID_W, :]
        kw = k_ref[pl.ds(start, nk), :]
        for hh in range(2):
            in_head = (lane >= NA_HEAD_DIM) if hh else (lane < NA_HEAD_DIM)
            qm = jnp.where(in_head, qr, jnp.zeros_like(qr))
            s = lax.dot_general(qm, kw, (((1,), (1,)), ((), ())), preferred_element_type=F32)
            s_ref[(2 * i + hh) * GRID_W:(2 * i + hh + 1) * GRID_W, :] = s + bias_ref[hh, dr0]
    s = s_ref[...]
    p = jnp.exp(s - jnp.max(s, axis=-1, keepdims=True))
    p_ref[...] = (p * (1.0 / jnp.sum(p, axis=-1, keepdims=True))).astype(BF16)
    for i in range(rb):
        vw = v_ref[pl.ds(starts[i], nk), :]
        o0 = _dot(p_ref[(2 * i) * GRID_W:(2 * i + 1) * GRID_W, :], vw)
        o1 = _dot(p_ref[(2 * i + 1) * GRID_W:(2 * i + 2) * GRID_W, :], vw)
        o_ref[i * GRID_W:(i + 1) * GRID_W, :] = jnp.where(lane < NA_HEAD_DIM, o0, o1).astype(o_ref.dtype)


def _na_bias_table(rpb):
    c = np.arange(GRID_W)
    c0 = np.clip(c - NA_KW // 2, 0, GRID_W - NA_KW)
    col_in = (c[None, :] >= c0[:, None]) & (c[None, :] < c0[:, None] + NA_KW)
    dc_idx = np.clip(c[None, :] - c[:, None] + (NA_KW - 1), 0, 2 * NA_KW - 2)
    onehot = jnp.asarray((dc_idx[:, :, None] == np.arange(2 * NA_KW - 1)).astype(np.float32))
    rows = jnp.stack([rpb.astype(F32)[:, d:d + NA_KR] for d in range(NA_KR)], axis=1)
    b = jnp.einsum("hdjc,qkc->hdqjk", rows, onehot, precision=lax.Precision.HIGHEST)
    b = jnp.where(jnp.asarray(col_in)[None, None, :, None, :], b, NEG)
    return b.reshape(NA_HEADS, NA_KR, GRID_W, NA_KR * GRID_W)


def _na(qk, v, bias, B, S, rb=16):
    rows = S // GRID_W
    assert rows >= NA_KR and rows % rb == 0
    nblk = rows // rb
    tq = rb * GRID_W
    kofs = NA_WIDTH // LANES
    return pl.pallas_call(
        functools.partial(_na_kernel, rows, rb),
        grid=(B, NA_HEADS // 2, nblk),
        in_specs=[pl.BlockSpec((tq, LANES), lambda b, p, r: (b * nblk + r, p)),
                  pl.BlockSpec((S, LANES), lambda b, p, r: (b, kofs + p)),
                  pl.BlockSpec((S, LANES), lambda b, p, r: (b, p)),
                  pl.BlockSpec((2, NA_KR, GRID_W, NA_KR * GRID_W), lambda b, p, r: (p, 0, 0, 0))],
        out_specs=pl.BlockSpec((tq, LANES), lambda b, p, r: (b * nblk + r, p)),
        out_shape=jax.ShapeDtypeStruct((B * S, NA_WIDTH), BF16),
        scratch_shapes=[pltpu.VMEM((2 * tq, NA_KR * GRID_W), F32), pltpu.VMEM((2 * tq, NA_KR * GRID_W), BF16)],
        compiler_params=_params("parallel", "parallel", "parallel"),
        name="na_attention",
    )(qk, qk, v, bias)


def _hy_prep_kernel(S, hy_ref, prev_ref, next_ref, w_ref, b_ref, z_ref, x0_ref):
    tm, C = hy_ref.shape
    tok0 = pl.program_id(0) * tm
    h = hy_ref[...]
    row = lax.broadcasted_iota(jnp.int32, (tm, C), 0)
    prev_row = jnp.where(tok0 % S == 0, 0.0, prev_ref[7:8, :])
    next_row = jnp.where((tok0 + tm) % S == 0, 0.0, next_ref[0:1, :])
    up = jnp.where(row == 0, prev_row, pltpu.roll(h, 1, axis=0))
    dn = jnp.where(row == tm - 1, next_row, pltpu.roll(h, tm - 1, axis=0))
    u = up * w_ref[0:1, :] + h * w_ref[1:2, :] + dn * w_ref[2:3, :] + b_ref[...]
    c = C // 3
    x0_ref[...] = u[:, :c]
    z_ref[...] = u[:, 2 * c:] * u[:, c:2 * c]


def _hy_prep(hy, conv_w, conv_b, S, tm=1024):
    T, C = hy.shape
    nb = tm // 8
    last = T // 8 - 1
    return pl.pallas_call(
        functools.partial(_hy_prep_kernel, S),
        grid=(T // tm,),
        in_specs=[pl.BlockSpec((tm, C), lambda i: (i, 0)),
                  pl.BlockSpec((8, C), lambda i: (jnp.maximum(i * nb - 1, 0), 0)),
                  pl.BlockSpec((8, C), lambda i: (jnp.minimum((i + 1) * nb, last), 0)),
                  pl.BlockSpec((3, C), lambda i: (0, 0)),
                  pl.BlockSpec((1, C), lambda i: (0, 0))],
        out_specs=[pl.BlockSpec((tm, C // 3), lambda i: (i, 0)),
                   pl.BlockSpec((tm, C // 3), lambda i: (i, 0))],
        out_shape=[jax.ShapeDtypeStruct((T, C // 3), F32), jax.ShapeDtypeStruct((T, C // 3), F32)],
        compiler_params=_params("parallel"),
        name="hyena_prep",
    )(hy, hy, hy, conv_w, conv_b.reshape(1, C))


def _hy_filter_kernel(L, pos_ref, dec_ref, w1_ref, b1_ref, fr_ref, w2_ref, b2_ref, w3_ref, g_ref):
    tr, C = g_ref.shape
    hr = tr // 2
    fr = fr_ref[...]
    w1h, w1l = _split(w1_ref[...])
    w2h, w2l = _split(w2_ref[...])
    w3h, w3l = _split(w3_ref[...])
    ph, plo = _split(jnp.concatenate([pos_ref[0:hr, :], pos_ref[hr:tr, :]], axis=1))
    h = jnp.sin(fr * (_dot3(ph, plo, w1h, w1l) + _dot(plo, w1l) + b1_ref[...]))
    hh, hl = _split(h)
    h = jnp.sin(fr * (_dot3(hh, hl, w2h, w2l) + _dot(hl, w2l) + b2_ref[...]))
    hh, hl = _split(h)
    h = _dot3(hh, hl, w3h, w3l) + _dot(hl, w3l)
    for half in range(2):
        rows = slice(half * hr, (half + 1) * hr)
        dec = jnp.exp(-(pos_ref[rows, 0:1] * dec_ref[...]))
        hf = h[:, (2 * half) * C:(2 * half + 1) * C] * dec
        hb = h[:, (2 * half + 1) * C:(2 * half + 2) * C] * dec
        n = pl.program_id(0) * tr + half * hr + lax.broadcasted_iota(jnp.int32, (hr, C), 0)
        g = jnp.where(n < L, hf, hb)
        g = jnp.where(n == L, 0.0, g)
        g_ref[rows, :] = jnp.where(n == 0, hf + hb, g)


def _hy_filter(L, pos_ext, dec_ext, w1p, b1, freq, w2, b2, w3, tr=1024):
    N = 2 * L
    tr = min(tr, N)
    full = lambda a: pl.BlockSpec(a.shape, lambda i: (0, 0))
    twice = lambda w: jnp.kron(jnp.eye(2, dtype=w.dtype), w)
    row2 = lambda b: jnp.tile(b, 2).reshape(1, -1)
    ws = (twice(w1p), row2(b1), row2(freq), twice(w2), row2(b2), twice(w3))
    return pl.pallas_call(
        functools.partial(_hy_filter_kernel, L),
        grid=(N // tr,),
        in_specs=[pl.BlockSpec((tr, pos_ext.shape[1]), lambda i: (i, 0)), full(dec_ext)]
                 + [full(w) for w in ws],
        out_specs=pl.BlockSpec((tr, HY_WIDTH), lambda i: (i, 0)),
        out_shape=jax.ShapeDtypeStruct((N, HY_WIDTH), F32),
        compiler_params=_params("parallel"),
        name="hyena_filter",
    )(pos_ext, dec_ext, *ws)


def _dft_pair_kernel(ma_ref, mb_ref, x_ref, or_ref, oi_ref):
    xh, xl = _split(x_ref[...])
    or_ref[...] = _dot3(ma_ref[0], ma_ref[1], xh, xl)
    oi_ref[...] = _dot3(mb_ref[0], mb_ref[1], xh, xl)


def _dft_pair(ma, mb, x, tc=2048):
    R = ma.shape[1]
    K, C = x.shape
    return pl.pallas_call(
        _dft_pair_kernel,
        grid=(C // tc,),
        in_specs=[pl.BlockSpec(ma.shape, lambda j: (0, 0, 0)),
                  pl.BlockSpec(mb.shape, lambda j: (0, 0, 0)),
                  pl.BlockSpec((K, tc), lambda j: (0, j))],
        out_specs=[pl.BlockSpec((R, tc), lambda j: (0, j)), pl.BlockSpec((R, tc), lambda j: (0, j))],
        out_shape=[jax.ShapeDtypeStruct((R, C), F32), jax.ShapeDtypeStruct((R, C), F32)],
        compiler_params=_params("parallel"),
        name="dft_outer",
    )(ma, mb, x)


def _spec_conv_kernel(ar_ref, ai_ref, fr_ref, fi_ref, twc_ref, tws_ref, wf_ref, wi_ref, br_ref, bi_ref):
    kb, n2, C = ar_ref.shape
    wfh, wfl, wih, wil = wf_ref[0], wf_ref[1], wi_ref[0], wi_ref[1]
    for k in range(kb):
        c = jnp.concatenate([twc_ref[k]] * (C // LANES), axis=1)
        s = jnp.concatenate([tws_ref[k]] * (C // LANES), axis=1)

        def inner_dft(re, im):
            h, l = _split(jnp.concatenate([re * c + im * s, im * c - re * s], axis=0))
            return _dot3(wfh, wfl, h, l)

        x = inner_dft(ar_ref[k], ai_ref[k])
        g = inner_dft(fr_ref[k], fi_ref[k])
        xr, xi, gr, gi = x[:n2], x[n2:], g[:n2], g[n2:]
        yh, yl = _split(jnp.concatenate([xr * gr - xi * gi, xr * gi + xi * gr], axis=0))
        b = _dot3(wih, wil, yh, yl)
        br, bi = b[:n2], b[n2:]
        br_ref[k] = br * c - bi * s
        bi_ref[k] = bi * c + br * s


def _spec_conv(ar, ai, fr, fi, fc, kb=8):
    n1, n2, C = ar.shape
    kb = min(kb, n1)
    blk = pl.BlockSpec((kb, n2, C), lambda k: (k, 0, 0))
    tblk = pl.BlockSpec((kb, n2, LANES), lambda k: (k, 0, 0))
    wblk = pl.BlockSpec((2, 2 * n2, 2 * n2), lambda k: (0, 0, 0))
    return pl.pallas_call(
        _spec_conv_kernel,
        grid=(n1 // kb,),
        in_specs=[blk, blk, blk, blk, tblk, tblk, wblk, wblk],
        out_specs=[blk, blk],
        out_shape=[jax.ShapeDtypeStruct((n1, n2, C), F32), jax.ShapeDtypeStruct((n1, n2, C), F32)],
        compiler_params=_params("parallel"),
        name="spectrum_conv",
    )(ar, ai, fr, fi, fc["twc"], fc["tws"], fc["wf"], fc["wi"])


def _idft_out_kernel(ma_ref, mb_ref, br_ref, bi_ref, z_ref, x0_ref, bias_ref, o_ref):
    brh, brl = _split(br_ref[...])
    bih, bil = _split(bi_ref[...])
    y = _dot3(ma_ref[0], ma_ref[1], brh, brl) + _dot3(mb_ref[0], mb_ref[1], bih, bil)
    o_ref[...] = ((y + z_ref[...] * bias_ref[...]) * x0_ref[...]).astype(o_ref.dtype)


def _idft_out(ma, mb, br, bi, z2, x02, bias_t, tc=2048):
    R = ma.shape[1]
    K, C = br.shape
    col = lambda rws: pl.BlockSpec((rws, tc), lambda j: (0, j))
    return pl.pallas_call(
        _idft_out_kernel,
        grid=(C // tc,),
        in_specs=[pl.BlockSpec(ma.shape, lambda j: (0, 0, 0)), pl.BlockSpec(mb.shape, lambda j: (0, 0, 0)),
                  col(K), col(K), col(R), col(R), pl.BlockSpec((1, tc), lambda j: (0, 0))],
        out_specs=col(R),
        out_shape=jax.ShapeDtypeStruct((R, C), BF16),
        compiler_params=_params("parallel"),
        name="idft_outer",
    )(ma, mb, br, bi, z2, x02, bias_t)


def _split_const(m):
    m32 = jnp.asarray(np.asarray(m, np.float32))
    hi, lo = _split(m32)
    return jnp.stack([hi, lo])


def _fft_constants(L):
    N = 2 * L
    n2 = FFT_N2
    n1 = N // n2
    h = n1 // 2
    k1 = np.arange(n1)[:, None].astype(np.float64)
    ang = 2.0 * np.pi * k1 * np.arange(n1)[None, :] / n1
    c, s = np.cos(ang), np.sin(ang)
    za = np.concatenate([c[:, :h], s[:, :h]], axis=1)
    zb = np.concatenate([-s[:, :h], c[:, :h]], axis=1)
    ga, gb = c, -s
    ya = np.concatenate([c[:h], s[:h]], axis=0)
    yb = np.concatenate([-s[:h], c[:h]], axis=0)
    kk = np.arange(n2)[:, None].astype(np.float64)
    nn = np.arange(n2)[None, :].astype(np.float64)
    base = 2.0 * np.pi * kk * nn / n2
    cb, sb = np.cos(base), np.sin(base)
    wf = np.block([[cb, sb], [-sb, cb]])
    wi = np.block([[cb.T, -sb.T], [sb.T, cb.T]]) / N
    tw = 2.0 * np.pi * np.arange(n1)[:, None] * nn / N
    rep = lambda m: jnp.broadcast_to(jnp.asarray(np.asarray(m, np.float32))[:, :, None], (n1, n2, LANES))
    sp = lambda m: _split_const(m)
    return dict(za=sp(za), zb=sp(zb), ga=sp(ga), gb=sp(gb), ya=sp(ya), yb=sp(yb), wf=sp(wf), wi=sp(wi),
                twc=rep(np.cos(tw)), tws=rep(np.sin(tw)), n1=n1)


def _hy_pos_tables(L):
    ext = lambda a: jnp.concatenate([a, a[0:1], jnp.flip(a[1:], axis=0)], axis=0)
    t = ext(jnp.linspace(0.0, 1.0, L, dtype=F32)[:, None])
    w = ext(2.0 * math.pi * jnp.arange(L, dtype=F32)[:, None] / L)
    f = jnp.linspace(1e-4, HY_POS_BANDS - 1, HY_POS_BANDS, dtype=F32)[None, :]
    z = jnp.concatenate([t, jnp.cos(f * w), -jnp.sin(f * w)], axis=-1)
    max_decay = math.log(HY_DECAY_TARGET) / HY_FAST_PCT
    min_decay = math.log(HY_DECAY_TARGET) / HY_SLOW_PCT
    deltas = jnp.linspace(min_decay, max_decay, HY_WIDTH, dtype=F32)
    zp = jnp.pad(z, ((0, 0), (0, HY_FFN_HIDDEN - HY_POS_DIM)))
    return zp, jnp.abs(deltas)[None, :]


_TRI_ROWS = 16 + 8 * 7 + 8


def _tri_tables(tm):
    flat = [b for b in range(16)]
    for a in range(1, 8):
        flat += [16 * a + b for b in range(8)]
    flat += [16 * a for a in range(8, 16)]
    return jnp.asarray(np.tile(np.asarray(flat, np.float32)[:, None], (1, tm)))


def _sort16_pairs():
    def merge(lo, hi, r):
        step = r * 2
        if step < hi - lo:
            yield from merge(lo, hi, step)
            yield from merge(lo + r, hi, step)
            yield from [(i, i + r) for i in range(lo + r, hi - r, step)]
        else:
            yield (lo, lo + r)

    def sort(lo, hi):
        if hi - lo >= 1:
            mid = lo + (hi - lo) // 2
            yield from sort(lo, mid)
            yield from sort(mid + 1, hi)
            yield from merge(lo, hi, 1)
    return list(sort(0, 15))


_SORT16 = _sort16_pairs()
_BITONIC16 = [(i, i + d) for d in (8, 4, 2, 1) for i in range(16) if not i & d]


def _cmpx(x, pairs):
    x = list(x)
    for i, j in pairs:
        x[i], x[j] = jnp.maximum(x[i], x[j]), jnp.minimum(x[i], x[j])
    return x


def _top16_sorted(s):
    n, tm = s.shape
    sub = n // PEER_TOPK
    x = _cmpx([s[r * sub:(r + 1) * sub] for r in range(PEER_TOPK)], _SORT16)
    shift = sub // 2
    while shift:
        y = [pltpu.roll(v, shift, axis=0) for v in x]
        x = _cmpx([jnp.maximum(x[i], y[PEER_TOPK - 1 - i]) for i in range(PEER_TOPK)], _BITONIC16)
        shift //= 2
    rowk = lax.broadcasted_iota(jnp.int32, (PEER_TOPK, tm), 0)
    vals = jnp.zeros((PEER_TOPK, tm), F32)
    for r in range(PEER_TOPK):
        vals = jnp.where(rowk == r, jnp.concatenate([x[r], x[r]], axis=0), vals)
    return vals


def _topk_rows(scores, k, exact):
    n, tm = scores[0].shape
    if not exact:
        return [(_top16_sorted(s), None) for s in scores]
    rowk = lax.broadcasted_iota(jnp.int32, (k, tm), 0)
    rowf = lax.broadcasted_iota(jnp.int32, (n, tm), 0).astype(F32)
    state = [(s, jnp.zeros((k, tm), F32), jnp.full((n, tm), float(k), F32)) for s in scores]
    for r in range(k):
        nxt = []
        for s, vals, rank in state:
            m = jnp.max(s, axis=0, keepdims=True)
            sel = rowf == jnp.min(jnp.where(s == m, rowf, float(n)), axis=0, keepdims=True)
            nxt.append((jnp.where(sel, -jnp.inf, s), jnp.where(rowk == r, m, vals),
                        jnp.where(sel, float(r), rank)))
        state = nxt
    return [(vals, rank) for _, vals, rank in state]


def _stair_cells(t1, t2, op):
    pieces = [op(t1[0:1], t2)]
    pieces += [op(t1[a:a + 1], t2[0:8]) for a in range(1, 8)]
    pieces += [op(t1[8:16], t2[0:1])]
    return jnp.concatenate(pieces, axis=0)


def _peer_tables(scores, flat, exact):
    K = PEER_TOPK
    tm = scores[0].shape[1]
    (sv1, rank1), (sv2, rank2) = _topk_rows(scores, K, exact)
    cand = _stair_cells(sv1, sv2, lambda x, y: x + y)
    if exact:
        row16 = lax.broadcasted_iota(jnp.int32, (K, tm), 0).astype(F32)
        length = jnp.zeros((K, tm), F32)
        for r in range(K):
            m = jnp.max(cand, axis=0, keepdims=True)
            f = jnp.min(jnp.where(cand == m, flat, 1e9), axis=0, keepdims=True)
            cand = jnp.where(flat == f, -jnp.inf, cand)
            length = jnp.where(row16 == jnp.floor(f * (1.0 / K)), length + 1.0, length)
        picked = cand == -jnp.inf
    else:
        for r in range(K):
            cand = jnp.where(cand == jnp.max(cand, axis=0, keepdims=True), -jnp.inf, cand)
        picked = cand == -jnp.inf
        cnt = jnp.where(picked, 1.0, 0.0)
        rows = [jnp.sum(cnt[0:16], axis=0, keepdims=True)]
        rows += [jnp.sum(cnt[8 + 8 * a:16 + 8 * a], axis=0, keepdims=True) for a in range(1, 8)]
        length = jnp.concatenate(rows + [cnt[72:80]], axis=0)
    s1, s2 = scores
    lfull = jnp.zeros_like(s1)
    if exact:
        ok = jnp.ones((1, tm), jnp.bool_)
        for r in range(K):
            lfull = jnp.where(rank1 == float(r), length[r:r + 1], lfull)
    else:
        rank2 = jnp.zeros_like(s2)
        for r in range(K):
            lfull = jnp.where(s1 == sv1[r:r + 1], length[r:r + 1], lfull)
            rank2 = jnp.where(sv2[r:r + 1] > s2, float(r + 1), rank2)
        reach = lambda s, sv: jnp.sum(jnp.where(s >= sv[K - 1:K], 1.0, 0.0), axis=0, keepdims=True)
        ok = (reach(s1, sv1) == float(K)) & (reach(s2, sv2) == float(K)) & \
             (jnp.sum(length, axis=0, keepdims=True) == float(K))
    es1 = jnp.exp(sv1 - sv1[0:1])
    es2 = jnp.exp(sv2 - sv2[0:1])
    ecand = _stair_cells(es1, es2, lambda x, y: x * y)
    z = jnp.sum(jnp.where(picked, ecand, 0.0), axis=0, keepdims=True)
    e1 = jnp.exp(scores[0] - sv1[0:1]) / z
    e2 = jnp.exp(scores[1] - sv2[0:1])
    return (lfull, e1, rank2, e2), jnp.where(ok, 1.0, 0.0)


def _peer_topk_kernel(hnT_ref, wq_ref, sk_ref, flat_ref, l_ref, e1_ref, rb_ref, e2_ref):
    flat = flat_ref[...]

    def head(h, carry):
        qT = _dot(wq_ref[h], hnT_ref[...])
        scores = []
        for p in range(2):
            qh, ql = _split(qT[p * PEER_HALF:(p + 1) * PEER_HALF])
            kh, kl = _split(sk_ref[h, p])
            scores.append(_dot3(kh, kl, qh, ql))

        def write(tabs, cs):
            l_ref[h, :, cs], e1_ref[h, :, cs] = tabs[0], tabs[1]
            rb_ref[h, :, cs], e2_ref[h, :, cs] = tabs[2].astype(rb_ref.dtype), tabs[3].astype(e2_ref.dtype)

        wd = flat.shape[1]
        for c in range(qT.shape[1] // wd):
            cs = slice(c * wd, (c + 1) * wd)
            sc = [s[:, cs] for s in scores]
            tabs, ok = _peer_tables(sc, flat, exact=False)
            write(tabs, cs)

            @pl.when(jnp.min(ok) < 0.5)
            def _():
                write(_peer_tables(sc, flat, exact=True)[0], cs)
        return carry

    lax.fori_loop(0, PEER_HEADS, head, 0)


def _peer_topk(hnT, wq_heads, subkeys, tm=512, wd=256):
    D, T = hnT.shape
    H = wq_heads.shape[0]
    tm = min(tm, T)
    wd = min(wd, tm)
    out = jax.ShapeDtypeStruct((H, PEER_NKEYS, T), F32)
    outb = jax.ShapeDtypeStruct((H, PEER_NKEYS, T), BF16)
    oblk = pl.BlockSpec((H, PEER_NKEYS, tm), lambda i: (0, 0, i))
    return pl.pallas_call(
        _peer_topk_kernel,
        grid=(T // tm,),
        in_specs=[pl.BlockSpec((D, tm), lambda i: (0, i)),
                  pl.BlockSpec(wq_heads.shape, lambda i: (0, 0, 0)),
                  pl.BlockSpec((H, 2, PEER_NKEYS, PEER_HALF), lambda i: (0, 0, 0, 0)),
                  pl.BlockSpec((_TRI_ROWS, wd), lambda i: (0, 0))],
        out_specs=[oblk, oblk, oblk, oblk],
        out_shape=[out, out, outb, outb],
        compiler_params=_params("parallel"),
        name="peer_topk",
    )(hnT, wq_heads, subkeys, _tri_tables(wd))


def _gelu_tanh(x):
    c = math.sqrt(2.0 / math.pi)
    h = 0.5 * x
    return h + h * jnp.tanh(x * (c + (c * 0.044715) * (x * x)))


def _peer_dense_kernel(ni, hnT_ref, u_ref, vT_ref, l_ref, e1_ref, rb_ref, e2_ref, x_ref, o_ref,
                       acc_ref, act_ref, a_ref):
    j = pl.program_id(1)
    nk = PEER_NKEYS

    @pl.when(j == 0)
    def _():
        acc_ref[...] = jnp.zeros_like(acc_ref)

    tm = hnT_ref.shape[1]
    i0 = pl.multiple_of(j * ni, ni)
    zero = jnp.zeros((), BF16)
    a_all = _dot(u_ref[...], hnT_ref[...])
    for ii in range(ni):
        rows = slice(ii * nk, (ii + 1) * nk)
        w = None
        for h in range(PEER_HEADS):
            lrow = jnp.broadcast_to(l_ref[h, pl.ds(i0 + ii, 1), :], (BF16_ROWS, tm)).astype(BF16)
            erow = jnp.broadcast_to(e1_ref[h, pl.ds(i0 + ii, 1), :], (BF16_ROWS, tm)).astype(BF16)
            wh = jnp.where(rb_ref[h] < lrow[None], e2_ref[h], zero) * erow[None]
            w = wh if w is None else w + wh
        w = w.reshape(nk, tm)
        act_ref[rows, :] = w
        a_ref[rows, :] = a_all[rows, :].astype(BF16) + jnp.minimum(w, zero)
    for ii in range(ni):
        rows = slice(ii * nk, (ii + 1) * nk)
        act_ref[rows, :] = _gelu_tanh(a_ref[rows, :]) * act_ref[rows, :]
    acc_ref[...] += _dot(vT_ref[...], act_ref[...])

    @pl.when(j == pl.num_programs(1) - 1)
    def _():
        o_ref[...] = x_ref[...] + acc_ref[...].T


def _peer_dense(hnT, u_bf, vT_bf, tabs, x, tm=512, ni=16):
    D, T = hnT.shape
    NE = u_bf.shape[0]
    te = ni * PEER_NKEYS
    tm = min(tm, T)
    tab = pl.BlockSpec((PEER_HEADS, PEER_NKEYS, tm), lambda i, j: (0, 0, i))
    grp = PEER_NKEYS // BF16_ROWS
    tabb = pl.BlockSpec((PEER_HEADS, grp, BF16_ROWS, tm), lambda i, j: (0, 0, 0, i))
    lt, e1, rb, e2 = tabs
    rb, e2 = [t.reshape(PEER_HEADS, grp, BF16_ROWS, T) for t in (rb, e2)]
    return pl.pallas_call(
        functools.partial(_peer_dense_kernel, ni),
        grid=(T // tm, NE // te),
        in_specs=[pl.BlockSpec((D, tm), lambda i, j: (0, i)),
                  pl.BlockSpec((te, D), lambda i, j: (j, 0)),
                  pl.BlockSpec((D, te), lambda i, j: (0, j)),
                  tab, tab, tabb, tabb,
                  pl.BlockSpec((tm, D), lambda i, j: (i, 0))],
        out_specs=pl.BlockSpec((tm, D), lambda i, j: (i, 0)),
        out_shape=jax.ShapeDtypeStruct((T, D), F32),
        scratch_shapes=[pltpu.VMEM((D, tm), F32), pltpu.VMEM((te, tm), BF16), pltpu.VMEM((te, tm), BF16)],
        compiler_params=_params("parallel", "arbitrary"),
        name="peer_experts",
    )(hnT, u_bf, vT_bf, lt, e1, rb, e2, x)


def kernel(x, norm1_g, w_in, gate_b, na_q_g, na_k_g, na_rpb, hy_conv_w, hy_conv_b, hy_w1, hy_b1, hy_freq,
           hy_w2, hy_b2, hy_w3, hy_bias, w_up_na, w_up_hy, w_out, norm2_g, peer_wq, peer_subkeys, peer_u,
           peer_v):
    B, S, D = x.shape
    assert B == 2, "the long convolution packs the two batches as one complex sequence"
    depth = w_in.shape[0]
    T = B * S
    xt = x.reshape(T, D)

    fc = _fft_constants(S)
    n1 = fc["n1"]
    pos_ext, dec_ext = _hy_pos_tables(S)
    head_sum = jnp.asarray(np.kron(np.eye(NA_HEADS), np.ones((NA_HEAD_DIM, NA_HEAD_DIM))), BF16)
    o_qk, o_v, o_hy = 2 * NA_WIDTH, 3 * NA_WIDTH, 3 * NA_WIDTH + 3 * HY_WIDTH
    tc = 4096
    ncol = FFT_N2 * HY_WIDTH

    for l in range(depth):
        qk_gain = jnp.concatenate([jnp.tile(na_q_g[l], NA_HEADS) * (NA_HEAD_DIM ** -0.5),
                                   jnp.tile(na_k_g[l], NA_HEADS)]).reshape(1, o_qk).astype(F32)
        qk, v, hy, gates = _proj_in(xt, norm1_g[l], w_in, l, qk_gain, head_sum, gate_b[l].reshape(1, 2 * D),
                                    (o_qk, o_v - o_qk, o_hy - o_v, w_in.shape[2] - o_hy))

        a_out = _na(qk, v, _na_bias_table(na_rpb[l]), B, S)

        z, x0 = _hy_prep(hy, hy_conv_w[l], hy_conv_b[l], S)
        w1p = jnp.pad(hy_w1[l], ((0, HY_FFN_HIDDEN - HY_POS_DIM), (0, 0)))
        g = _hy_filter(S, pos_ext, dec_ext, w1p, hy_b1[l], hy_freq[l], hy_w2[l], hy_b2[l], hy_w3[l])
        gr, gi = _dft_pair(fc["ga"], fc["gb"], g.reshape(n1, ncol), tc)
        z2 = z.reshape(n1, ncol)
        ar, ai = _dft_pair(fc["za"], fc["zb"], z2, tc)
        cube = lambda t: t.reshape(n1, FFT_N2, HY_WIDTH)
        br, bi = _spec_conv(cube(ar), cube(ai), cube(gr), cube(gi), fc)
        bias_t = jnp.tile(hy_bias[l], tc // HY_WIDTH).reshape(1, tc)
        b_out = _idft_out(fc["ya"], fc["yb"], br.reshape(n1, ncol), bi.reshape(n1, ncol),
                          z2, x0.reshape(n1, ncol), bias_t, tc).reshape(T, HY_WIDTH)

        xt = _merge_out(a_out, b_out, w_up_na, w_up_hy, gates, w_out, l, xt)

        hn2T = _rmsnorm_t(xt, norm2_g[l])
        wq_heads = peer_wq[l].reshape(D, PEER_HEADS, 2 * PEER_HALF).transpose(1, 2, 0).astype(BF16)
        tabs = _peer_topk(hn2T, wq_heads, peer_subkeys[l])
        xt = _peer_dense(hn2T, peer_u[l].astype(BF16), peer_v[l].T.astype(BF16), tabs, xt)
    return xt.reshape(B, S, D)
```
